```python
import jax, jax.numpy as jnp
from jax import lax
import numpy as np

D_MODEL = 2048
BATCH = 4
SEQ = 2048
DEPTH = 2

GRID_W = 64
CTX_LEN = 256
F32 = jnp.float32
EPS = 1e-6
ROPE_THETA = 10000.0
Q_BLOCK = 128

HEAD_DIM = 128
GQA_HEADS = 8
GQA_KV_HEADS = 2
GQA_GROUP = GQA_HEADS // GQA_KV_HEADS
HGRN_HEADS = 4
HGRN_DK = 128
HGRN_DV = 128
HGRN_W = HGRN_HEADS * HGRN_DK
HGRN_CHUNK = 64
MLA_HEADS = 4
MLA_Q_RANK = 512
MLA_KV_RANK = 256
MLA_NOPE = 128
MLA_ROPE = 64
MLA_V = 128
MLA_QK = MLA_NOPE + MLA_ROPE

MIX_WIDTH = GQA_HEADS * HEAD_DIM + HGRN_HEADS * HGRN_DV + MLA_HEADS * MLA_V
IN_SIZES = (GQA_HEADS * HEAD_DIM, GQA_KV_HEADS * HEAD_DIM, GQA_KV_HEADS * HEAD_DIM,
            HGRN_W, HGRN_HEADS * HGRN_DV, HGRN_W, HGRN_W, HGRN_HEADS * HGRN_DV,
            MLA_Q_RANK, MLA_KV_RANK, MLA_ROPE)
IN_WIDTH = sum(IN_SIZES)
FFN_HIDDEN = 5504
N_MOD = 9

kernel_name = 'hybrid_dit_gqa_hgrn2_mla_macaron'


def rms_norm(x, gain=None):
    xf = x.astype(F32)
    y = xf * lax.rsqrt(jnp.mean(xf * xf, axis=-1, keepdims=True) + EPS)
    if gain is not None:
        y = y * gain.astype(F32)
    return y.astype(x.dtype)


def modulation(cvec, w_mod, b_mod):
    m = jax.nn.silu(cvec) @ w_mod + b_mod
    m = m.reshape(m.shape[:-1] + (N_MOD, D_MODEL))
    return [m[..., None, i, :] for i in range(N_MOD)]


def modulate(x, shift, scale):
    return rms_norm(x) * (1.0 + scale) + shift


def swiglu(h, w_in, w_out):
    gate, up = jnp.split(h @ w_in, 2, axis=-1)
    return (jax.nn.silu(gate) * up) @ w_out


def axial_rope_tables(rows, rot_dim):
    row = jnp.repeat(jnp.arange(rows, dtype=F32), GRID_W)
    col = jnp.tile(jnp.arange(GRID_W, dtype=F32), rows)
    axis_dim = rot_dim // 2
    inv_freq = ROPE_THETA ** (-jnp.arange(0, axis_dim, 2, dtype=F32) / axis_dim)
    ang_r = row[:, None] * inv_freq
    ang_c = col[:, None] * inv_freq
    ang = jnp.concatenate([ang_r, ang_r, ang_c, ang_c], axis=-1)
    return jnp.cos(ang), jnp.sin(ang)


def apply_rope(x, cos, sin):
    a1, a2, b1, b2 = jnp.split(x, 4, axis=-1)
    rot = jnp.concatenate([-a2, a1, -b2, b1], axis=-1)
    y = x.astype(F32) * cos[:, None, :] + rot.astype(F32) * sin[:, None, :]
    return y.astype(x.dtype)


def block_attention(q, k, v, scale):
    b, kh, g, tq, d = q.shape
    nb = tq // Q_BLOCK
    qb = q.reshape(b, kh, g, nb, Q_BLOCK, d).transpose(3, 0, 1, 2, 4, 5)

    def one_block(q_blk):
        s = jnp.einsum('bkgqd,bksd->bkgqs', q_blk, k).astype(F32) * scale
        p = jax.nn.softmax(s, axis=-1).astype(v.dtype)
        return jnp.einsum('bkgqs,bkse->bkgqe', p, v)

    ob = lax.map(one_block, qb)
    return ob.transpose(1, 2, 3, 0, 4, 5).reshape(b, kh, g, tq, v.shape[-1])


def gla_chunkwise(q, k, v, log_f, s0):
    b, h, t, dk = q.shape
    dv = v.shape[-1]
    n = t // HGRN_CHUNK

    def chunks(a):
        return a.astype(F32).reshape(b, h, n, HGRN_CHUNK, a.shape[-1]).transpose(2, 0, 1, 3, 4)

    lower_tri = jnp.tril(jnp.ones((HGRN_CHUNK, HGRN_CHUNK), dtype=bool))[:, :, None]

    def step(state, inp):
        qc, kc, vc, gc = inp
        cum = jnp.cumsum(gc, axis=-2)
        o_inter = jnp.einsum('bhtk,bhkv->bhtv', qc * jnp.exp(cum), state)
        rel = jnp.where(lower_tri, cum[:, :, :, None, :] - cum[:, :, None, :, :], -jnp.inf)
        scores = jnp.einsum('bhtk,bhtsk,bhsk->bhts', qc, jnp.exp(rel), kc)
        o = o_inter + jnp.einsum('bhts,bhsv->bhtv', scores, vc)
        last = cum[:, :, -1:, :]
        new_state = (jnp.exp(last[:, :, 0, :])[..., None] * state
                     + jnp.einsum('bhsk,bhsv->bhkv', kc * jnp.exp(last - cum), vc))
        return new_state, o

    s_fin, o = lax.scan(step, s0, (chunks(q), chunks(k), chunks(v), chunks(log_f)))
    return o.transpose(1, 2, 0, 3, 4).reshape(b, h, t, dv), s_fin


def hgrn_direction(q, i, z, lb, s0):
    zf = z.astype(F32)
    lbf = lb.reshape(HGRN_HEADS, HGRN_DK)
    log_f = jnp.logaddexp(jnp.log(lbf), jnp.log1p(-lbf) + jax.nn.log_sigmoid(zf))
    k = (1.0 - lbf) * jax.nn.sigmoid(-zf)
    o, s = gla_chunkwise(jnp.swapaxes(q, 1, 2) * HGRN_DK ** -0.5, jnp.swapaxes(k, 1, 2),
                         jnp.swapaxes(i, 1, 2), jnp.swapaxes(log_f, 1, 2), s0)
    return jnp.swapaxes(o, 1, 2).astype(i.dtype), s


def hgrn2_bidirectional(q_l, i_l, zf_l, zb_l, q_c, i_c, zf_c, zb_c, lb):
    s0 = jnp.zeros((q_l.shape[0], HGRN_HEADS, HGRN_DK, HGRN_DV), F32)
    flip = lambda a: jnp.flip(a, axis=1)
    o_cf, s_f = hgrn_direction(q_c, i_c, zf_c, lb[0], s0)
    o_lf, _ = hgrn_direction(q_l, i_l, zf_l, lb[0], s_f)
    o_cb, s_b = hgrn_direction(flip(q_c), flip(i_c), flip(zb_c), lb[1], s0)
    o_lb, _ = hgrn_direction(flip(q_l), flip(i_l), flip(zb_l), lb[1], s_b)
    return o_lf + flip(o_lb), o_cf + flip(o_cb)


def split_in(p):
    offsets = []
    acc = 0
    for s in IN_SIZES[:-1]:
        acc += s
        offsets.append(acc)
    return jnp.split(p, offsets, axis=-1)


def heads(a, n):
    return a.reshape(a.shape[0], a.shape[1], n, -1)


def kv_layout(a):
    return jnp.swapaxes(a, 1, 2)


def gqa_q_layout(a):
    return a.reshape(a.shape[0], a.shape[1], GQA_KV_HEADS, GQA_GROUP, HEAD_DIM).transpose(0, 2, 3, 1, 4)


def merge_heads(o):
    return o.transpose(0, 3, 1, 2, 4).reshape(o.shape[0], o.shape[3], -1)


def token_mixers(h_lat, h_ctx, w_in, w_uq, w_ukv, w_out, gqa_q_gain, gqa_k_gain, mla_q_gain,
                 mla_kv_gain, lb, hgrn_norm_gain, rope_h, rope_r, need_ctx):
    t_lat = h_lat.shape[1]
    gq_l, gk_l, gv_l, hq_l, hi_l, hf_l, hb_l, hg_l, cq_l, ckv_l, kr_l = split_in(h_lat @ w_in)
    gq_c, gk_c, gv_c, hq_c, hi_c, hf_c, hb_c, hg_c, cq_c, ckv_c, kr_c = split_in(h_ctx @ w_in)

    k_a = jnp.concatenate([kv_layout(apply_rope(rms_norm(heads(gk_l, GQA_KV_HEADS), gqa_k_gain), *rope_h)),
                           kv_layout(rms_norm(heads(gk_c, GQA_KV_HEADS), gqa_k_gain))], axis=2)
    v_a = jnp.concatenate([kv_layout(heads(gv_l, GQA_KV_HEADS)), kv_layout(heads(gv_c, GQA_KV_HEADS))], axis=2)
    q_a = gqa_q_layout(apply_rope(rms_norm(heads(gq_l, GQA_HEADS), gqa_q_gain), *rope_h))
    o_a = merge_heads(block_attention(q_a, k_a, v_a, HEAD_DIM ** -0.5))

    def mla_q(cq):
        r = heads(rms_norm(cq, mla_q_gain) @ w_uq, MLA_HEADS)
        return r[..., :MLA_NOPE], r[..., MLA_NOPE:]

    def mla_kv(ckv, kr):
        r = heads(rms_norm(ckv, mla_kv_gain) @ w_ukv, MLA_HEADS)
        k = jnp.concatenate([r[..., :MLA_NOPE], jnp.broadcast_to(kr, r.shape[:-1] + (MLA_ROPE,))], axis=-1)
        return k, r[..., MLA_NOPE:]

    k_cl, v_cl = mla_kv(ckv_l, apply_rope(kr_l[:, :, None, :], *rope_r))
    k_cc, v_cc = mla_kv(ckv_c, kr_c[:, :, None, :])
    k_m = jnp.concatenate([kv_layout(k_cl), kv_layout(k_cc)], axis=2)
    v_m = jnp.concatenate([kv_layout(v_cl), kv_layout(v_cc)], axis=2)
    qn_l, qr_l = mla_q(cq_l)
    q_m = kv_layout(jnp.concatenate([qn_l, apply_rope(qr_l, *rope_r)], axis=-1))[:, :, None]
    o_m = merge_heads(block_attention(q_m, k_m, v_m, MLA_QK ** -0.5))

    o_bl, o_bc = hgrn2_bidirectional(heads(hq_l, HGRN_HEADS), heads(hi_l, HGRN_HEADS), heads(hf_l, HGRN_HEADS),
                                     heads(hb_l, HGRN_HEADS), heads(hq_c, HGRN_HEADS), heads(hi_c, HGRN_HEADS),
                                     heads(hf_c, HGRN_HEADS), heads(hb_c, HGRN_HEADS), lb)

    def hgrn_out(o, g):
        y = rms_norm(o, hgrn_norm_gain) * jax.nn.silu(heads(g, HGRN_HEADS))
        return y.reshape(y.shape[0], y.shape[1], -1)

    y_lat = jnp.concatenate([o_a, hgrn_out(o_bl, hg_l), o_m], axis=-1) @ w_out
    if not need_ctx:
        return y_lat, None

    q_ac = gqa_q_layout(rms_norm(heads(gq_c, GQA_HEADS), gqa_q_gain))
    o_ac = merge_heads(block_attention(q_ac, k_a[:, :, t_lat:], v_a[:, :, t_lat:], HEAD_DIM ** -0.5))
    qn_c, qr_c = mla_q(cq_c)
    q_mc = kv_layout(jnp.concatenate([qn_c, qr_c], axis=-1))[:, :, None]
    o_mc = merge_heads(block_attention(q_mc, k_m[:, :, t_lat:], v_m[:, :, t_lat:], MLA_QK ** -0.5))
    y_ctx = jnp.concatenate([o_ac, hgrn_out(o_bc, hg_c), o_mc], axis=-1) @ w_out
    return y_lat, y_ctx


def setup_inputs(seed: int = 0) -> dict:
    key = jax.random.key(seed)
    ks = jax.random.split(key, 24)

    def nrm(k, shape, scale):
        return jax.random.normal(k, shape, F32) * scale

    d = D_MODEL
    return {
        'x': nrm(ks[0], (BATCH, SEQ, d), 1.0),
        'c': nrm(ks[1], (BATCH, d), 1.0),
        'ctx': nrm(ks[2], (BATCH, CTX_LEN, d), 1.0),
        'c_ctx': nrm(ks[3], (d,), 1.0),
        'w_mod': nrm(ks[4], (DEPTH, d, N_MOD * d), 0.5 * d ** -0.5),
        'b_mod': nrm(ks[5], (DEPTH, N_MOD * d), 0.02),
        'w_ffn1_in': nrm(ks[6], (DEPTH, d, 2 * FFN_HIDDEN), d ** -0.5),
        'w_ffn1_out': nrm(ks[7], (DEPTH, FFN_HIDDEN, d), FFN_HIDDEN ** -0.5),
        'w_in': nrm(ks[8], (DEPTH, d, IN_WIDTH), d ** -0.5),
        'w_uq': nrm(ks[9], (DEPTH, MLA_Q_RANK, MLA_HEADS * MLA_QK), MLA_Q_RANK ** -0.5),
        'w_ukv': nrm(ks[10], (DEPTH, MLA_KV_RANK, MLA_HEADS * (MLA_NOPE + MLA_V)), MLA_KV_RANK ** -0.5),
        'w_out': nrm(ks[11], (DEPTH, MIX_WIDTH, d), MIX_WIDTH ** -0.5),
        'gqa_q_gain': 1.0 + nrm(ks[12], (DEPTH, HEAD_DIM), 0.02),
        'gqa_k_gain': 1.0 + nrm(ks[13], (DEPTH, HEAD_DIM), 0.02),
        'mla_q_gain': 1.0 + nrm(ks[14], (DEPTH, MLA_Q_RANK), 0.02),
        'mla_kv_gain': 1.0 + nrm(ks[15], (DEPTH, MLA_KV_RANK), 0.02),
        'hgrn_lb_logits': nrm(ks[16], (2, DEPTH, HGRN_W), 0.5),
        'hgrn_norm_gain': 1.0 + nrm(ks[17], (DEPTH, HGRN_DV), 0.02),
        'w_ffn2_in': nrm(ks[18], (DEPTH, d, 2 * FFN_HIDDEN), d ** -0.5),
        'w_ffn2_out': nrm(ks[19], (DEPTH, FFN_HIDDEN, d), FFN_HIDDEN ** -0.5),
        'final_gain': 1.0 + nrm(ks[20], (d,), 0.02),
    }


def reference(x, c, ctx, c_ctx, w_mod, b_mod, w_ffn1_in, w_ffn1_out, w_in, w_uq, w_ukv, w_out,
              gqa_q_gain, gqa_k_gain, mla_q_gain, mla_kv_gain, hgrn_lb_logits, hgrn_norm_gain,
              w_ffn2_in, w_ffn2_out, final_gain):
    rows = x.shape[1] // GRID_W
    rope_h = axial_rope_tables(rows, HEAD_DIM)
    rope_r = axial_rope_tables(rows, MLA_ROPE)
    lb_cum = jnp.cumsum(jax.nn.softmax(hgrn_lb_logits.astype(F32), axis=1), axis=1)
    lower_bounds = lb_cum - lb_cum[:, :1]

    x_lat, x_ctx = x, ctx
    for l in range(DEPTH):
        last = l == DEPTH - 1
        m_l = modulation(c, w_mod[l], b_mod[l])
        m_c = modulation(c_ctx, w_mod[l], b_mod[l])
        x_lat = x_lat + 0.5 * m_l[2] * swiglu(modulate(x_lat, m_l[0], m_l[1]), w_ffn1_in[l], w_ffn1_out[l])
        x_ctx = x_ctx + 0.5 * m_c[2] * swiglu(modulate(x_ctx, m_c[0], m_c[1]), w_ffn1_in[l], w_ffn1_out[l])
        y_lat, y_ctx = token_mixers(modulate(x_lat, m_l[3], m_l[4]), modulate(x_ctx, m_c[3], m_c[4]),
                                    w_in[l], w_uq[l], w_ukv[l], w_out[l], gqa_q_gain[l], gqa_k_gain[l],
                                    mla_q_gain[l], mla_kv_gain[l], lower_bounds[:, l], hgrn_norm_gain[l],
                                    rope_h, rope_r, not last)
        x_lat = x_lat + m_l[5] * y_lat
        x_lat = x_lat + 0.5 * m_l[8] * swiglu(modulate(x_lat, m_l[6], m_l[7]), w_ffn2_in[l], w_ffn2_out[l])
        if not last:
            x_ctx = x_ctx + m_c[5] * y_ctx
            x_ctx = x_ctx + 0.5 * m_c[8] * swiglu(modulate(x_ctx, m_c[6], m_c[7]), w_ffn2_in[l], w_ffn2_out[l])
    return rms_norm(x_lat, final_gain)
```

```python
import functools
import math

import jax
import jax.numpy as jnp
from jax import lax
from jax.experimental import pallas as pl
from jax.experimental.pallas import tpu as pltpu

F32 = jnp.float32
BF16 = jnp.bfloat16

D = 2048
BATCH = 4
SEQ = 2048
DEPTH = 2
GRID_W = 64
CTX = 256
EPS = 1e-6
ROPE_THETA = 10000.0

HEAD_DIM = 128
GQA_HEADS = 8
GQA_KV = 2
GQA_GROUP = GQA_HEADS // GQA_KV
HG_HEADS = 4
HG_DK = 128
HG_W = HG_HEADS * HG_DK
MLA_HEADS = 4
MLA_Q_RANK = 512
MLA_KV_RANK = 256
MLA_NOPE = 128
MLA_ROPE = 64
MLA_V = 128
MLA_QK = MLA_NOPE + MLA_ROPE
MLA_PAD = 256
FFN_H = 5504
N_MOD = 9

TB = CTX + SEQ
M = BATCH * TB
IN_MAIN = 4864
OFF_GQ, OFF_GK, OFF_GV = 0, 1024, 1280
OFF_HQ, OFF_HI, OFF_HF, OFF_HB, OFF_HG = 1536, 2048, 2560, 3072, 3584
OFF_CQ, OFF_CKV, OFF_KR = 4096, 4608, 4864

LANES = 128
SUBLANES = 8
VMEM_BYTES_V7X = 64 * 1024 * 1024

TM = TB // 2
TILES_PER_BATCH = TB // TM
ROWC = 128
TQ = 256
QT_PER_BATCH = TB // TQ
HCHUNK = 64
FFN_TH = 256
FFN_STEPS = -(-FFN_H // FFN_TH)
FFN_RC = 384
MOD_TN = 1024


def _cparams(sem, vmem_mb):
    return pltpu.CompilerParams(dimension_semantics=sem, vmem_limit_bytes=vmem_mb * 1024 * 1024)


def _sigmoid(x):
    return 1.0 / (1.0 + jnp.exp(-x))


def _mod_kernel(c_ref, w_ref, b_ref, o_ref):
    c = c_ref[...]
    a = (c * _sigmoid(c)).astype(BF16)
    o_ref[...] = jnp.dot(a, w_ref[...].astype(BF16), preferred_element_type=F32) + b_ref[...]


def _modulation(cvec8, w_mod, b_mod):
    n = N_MOD * D
    return pl.pallas_call(
        _mod_kernel,
        grid=(DEPTH, n // MOD_TN),
        in_specs=[
            pl.BlockSpec((SUBLANES, D), lambda l, j: (0, 0)),
            pl.BlockSpec((None, D, MOD_TN), lambda l, j: (l, 0, j)),
            pl.BlockSpec((None, 1, MOD_TN), lambda l, j: (l, 0, j)),
        ],
        out_specs=pl.BlockSpec((None, SUBLANES, MOD_TN), lambda l, j: (l, 0, j)),
        out_shape=jax.ShapeDtypeStruct((DEPTH, SUBLANES, n), F32),
        compiler_params=_cparams(("parallel", "parallel"), 40),
        name="modulation",
    )(cvec8, w_mod, b_mod.reshape(DEPTH, 1, n))


def _mod_rows(ref, batch):
    return ref[pl.ds(batch, 1), :], ref[4:5, :]


def _modulate_tile(x_ref, shift_ref, scale_ref, h_ref, tile):
    batch = tile // TILES_PER_BATCH
    part = tile % TILES_PER_BATCH
    sh_b, sh_c = _mod_rows(shift_ref, batch)
    sc_b, sc_c = _mod_rows(scale_ref, batch)

    def body(c, carry):
        is_ctx = (part * TM + c * ROWC) < CTX
        sh = jnp.where(is_ctx, sh_c, sh_b)
        sc = 1.0 + jnp.where(is_ctx, sc_c, sc_b)
        r0 = pl.multiple_of(c * ROWC, ROWC)
        xc = x_ref[pl.ds(r0, ROWC), :]
        ms = jnp.mean(xc * xc, axis=-1, keepdims=True)
        h_ref[pl.ds(r0, ROWC), :] = (xc * lax.rsqrt(ms + EPS) * sc + sh).astype(BF16)
        return carry

    lax.fori_loop(0, TM // ROWC, body, 0)


def _ffn_kernel(x_ref, sh_ref, sc_ref, gt_ref, wg_ref, wul_ref, wuh_ref, wol_ref, woh_ref,
                o_ref, h_ref, wi_ref, wo_ref):
    i = pl.program_id(0)
    j = pl.program_id(1)
    half = FFN_TH // 2

    @pl.when(j == 0)
    def _():
        _modulate_tile(x_ref, sh_ref, sc_ref, h_ref, i)
        o_ref[...] = jnp.zeros_like(o_ref)

    wi_ref[:, 0:FFN_TH] = wg_ref[...].astype(BF16)
    wi_ref[:, FFN_TH:FFN_TH + half] = wul_ref[...].astype(BF16)
    wi_ref[:, FFN_TH + half:] = wuh_ref[...].astype(BF16)
    wo_ref[0:half, :] = wol_ref[...].astype(BF16)
    wo_ref[half:, :] = woh_ref[...].astype(BF16)

    valid = jnp.where(j == FFN_STEPS - 1, FFN_H - (FFN_STEPS - 1) * FFN_TH, FFN_TH)
    col = lax.broadcasted_iota(jnp.int32, (FFN_RC, FFN_TH), 1)

    def rows(rc, carry):
        r0 = pl.multiple_of(rc * FFN_RC, FFN_RC)
        r = jnp.dot(h_ref[pl.ds(r0, FFN_RC), :], wi_ref[...], preferred_element_type=F32)
        g = r[:, :FFN_TH]
        u = r[:, FFN_TH:]
        a = jnp.where(col < valid, g * _sigmoid(g) * u, 0.0).astype(BF16)
        o_ref[pl.ds(r0, FFN_RC), :] += jnp.dot(a, wo_ref[...], preferred_element_type=F32)
        return carry

    lax.fori_loop(0, TM // FFN_RC, rows, 0)

    @pl.when(j == FFN_STEPS - 1)
    def _():
        batch = i // TILES_PER_BATCH
        part = i % TILES_PER_BATCH
        g_b, g_c = _mod_rows(gt_ref, batch)

        def body(c, carry):
            is_ctx = (part * TM + c * ROWC) < CTX
            gate = 0.5 * jnp.where(is_ctx, g_c, g_b)
            r0 = pl.multiple_of(c * ROWC, ROWC)
            o_ref[pl.ds(r0, ROWC), :] = x_ref[pl.ds(r0, ROWC), :] + gate * o_ref[pl.ds(r0, ROWC), :]
            return carry

        lax.fori_loop(0, TM // ROWC, body, 0)


def _ffn(x, mods, w_in, w_out, layer, mod0):
    half = FFN_TH // 2
    n_half = FFN_H // half
    last_half = 2 * n_half - 1

    def mod_spec(k):
        return pl.BlockSpec((None, SUBLANES, D), lambda i, j: (layer, 0, mod0 + k))

    return pl.pallas_call(
        _ffn_kernel,
        grid=(M // TM, FFN_STEPS),
        in_specs=[
            pl.BlockSpec((TM, D), lambda i, j: (i, 0), pipeline_mode=pl.Buffered(1)),
            mod_spec(0), mod_spec(1), mod_spec(2),
            pl.BlockSpec((None, D, FFN_TH), lambda i, j: (layer, 0, j)),
            pl.BlockSpec((None, D, half), lambda i, j: (layer, 0, n_half + 2 * j)),
            pl.BlockSpec((None, D, half),
                         lambda i, j: (layer, 0, jnp.minimum(n_half + 2 * j + 1, last_half))),
            pl.BlockSpec((None, half, D), lambda i, j: (layer, 2 * j, 0)),
            pl.BlockSpec((None, half, D), lambda i, j: (layer, jnp.minimum(2 * j + 1, n_half - 1), 0)),
        ],
        out_specs=pl.BlockSpec((TM, D), lambda i, j: (i, 0)),
        out_shape=jax.ShapeDtypeStruct((M, D), F32),
        scratch_shapes=[
            pltpu.VMEM((TM, D), BF16),
            pltpu.VMEM((D, 2 * FFN_TH), BF16),
            pltpu.VMEM((FFN_TH, D), BF16),
        ],
        compiler_params=_cparams(("parallel", "arbitrary"), 58),
        name="ffn",
    )(x, mods, mods, mods, w_in, w_in, w_in, w_out, w_out)


IN_TN = 256


def _inproj_kernel(x_ref, sh_ref, sc_ref, w_ref, wkr_ref, p_ref, kr_ref, h_ref):
    i = pl.program_id(0)
    j = pl.program_id(1)

    @pl.when(j == 0)
    def _():
        _modulate_tile(x_ref, sh_ref, sc_ref, h_ref, i)
        kr_ref[...] = jnp.dot(h_ref[...], wkr_ref[...].astype(BF16), preferred_element_type=F32)

    p_ref[...] = jnp.dot(h_ref[...], w_ref[...].astype(BF16), preferred_element_type=F32)


def _inproj(x, mods, w_in, w_kr, layer):
    def mod_spec(k):
        return pl.BlockSpec((None, SUBLANES, D), lambda i, j: (layer, 0, k))

    return pl.pallas_call(
        _inproj_kernel,
        grid=(M // TM, IN_MAIN // IN_TN),
        in_specs=[
            pl.BlockSpec((TM, D), lambda i, j: (i, 0)),
            mod_spec(3), mod_spec(4),
            pl.BlockSpec((None, D, IN_TN), lambda i, j: (layer, 0, j)),
            pl.BlockSpec((D, LANES), lambda i, j: (0, 0)),
        ],
        out_specs=[
            pl.BlockSpec((TM, IN_TN), lambda i, j: (i, j)),
            pl.BlockSpec((TM, LANES), lambda i, j: (i, 0)),
        ],
        out_shape=[jax.ShapeDtypeStruct((M, IN_MAIN), F32), jax.ShapeDtypeStruct((M, LANES), F32)],
        scratch_shapes=[pltpu.VMEM((TM, D), BF16)],
        compiler_params=_cparams(("parallel", "arbitrary"), 48),
        name="inproj",
    )(x, mods, mods, w_in, w_kr)


def _rope_tables(rot_dim):
    rows = SEQ // GRID_W
    row = jnp.repeat(jnp.arange(rows, dtype=F32), GRID_W)
    colp = jnp.tile(jnp.arange(GRID_W, dtype=F32), rows)
    axis_dim = rot_dim // 2
    inv_freq = ROPE_THETA ** (-jnp.arange(0, axis_dim, 2, dtype=F32) / axis_dim)
    ang_r = row[:, None] * inv_freq
    ang_c = colp[:, None] * inv_freq
    ang = jnp.concatenate([ang_r, ang_r, ang_c, ang_c], axis=-1)
    cos, sin = jnp.cos(ang), jnp.sin(ang)
    quarter = rot_dim // 4
    lane = jnp.arange(rot_dim)
    first = (lane % (2 * quarter)) < quarter
    sin_up = jnp.where(first, -sin, 0.0)
    sin_dn = jnp.where(first, 0.0, sin)
    pad = LANES - rot_dim
    if pad:
        cos = jnp.pad(cos, ((0, 0), (0, pad)), constant_values=1.0)
        sin_up = jnp.pad(sin_up, ((0, 0), (0, pad)))
        sin_dn = jnp.pad(sin_dn, ((0, 0), (0, pad)))
    return cos, sin_up, sin_dn


def _rope(x, cos, sin_up, sin_dn, quarter):
    up = pltpu.roll(x, LANES - quarter, 1)
    dn = pltpu.roll(x, quarter, 1)
    return x * cos + up * sin_up + dn * sin_dn


def _head_norm(x, gain):
    return x * lax.rsqrt(jnp.mean(x * x, axis=-1, keepdims=True) + EPS) * gain


def _gqa_prep_kernel(p_ref, qg_ref, kg_ref, cos_ref, su_ref, sd_ref, q_ref, k_ref):
    t = pl.program_id(0)
    is_ctx = (t % QT_PER_BATCH) == 0
    cos, su, sd = cos_ref[...], su_ref[...], sd_ref[...]
    scale = HEAD_DIM ** -0.5
    for hh in range(GQA_HEADS + GQA_KV):
        xh = p_ref[:, hh * HEAD_DIM:(hh + 1) * HEAD_DIM]
        is_q = hh < GQA_HEADS
        n = _head_norm(xh, qg_ref[...] if is_q else kg_ref[...])
        y = jnp.where(is_ctx, n, _rope(n, cos, su, sd, HEAD_DIM // 4))
        if is_q:
            q_ref[:, hh * HEAD_DIM:(hh + 1) * HEAD_DIM] = (y * scale).astype(BF16)
        else:
            kk = hh - GQA_HEADS
            k_ref[:, kk * HEAD_DIM:(kk + 1) * HEAD_DIM] = y.astype(BF16)


def _rope_idx(t):
    return jnp.maximum(t % QT_PER_BATCH - 1, 0)


def _gqa_prep(p, q_gain, k_gain, tabs):
    width = (GQA_HEADS + GQA_KV) * HEAD_DIM
    tab_spec = pl.BlockSpec((TQ, LANES), lambda t: (_rope_idx(t), 0))
    vec_spec = pl.BlockSpec((1, HEAD_DIM), lambda t: (0, 0))
    return pl.pallas_call(
        _gqa_prep_kernel,
        grid=(M // TQ,),
        in_specs=[pl.BlockSpec((TQ, width), lambda t: (t, 0)), vec_spec, vec_spec,
                  tab_spec, tab_spec, tab_spec],
        out_specs=[pl.BlockSpec((TQ, GQA_HEADS * HEAD_DIM), lambda t: (t, 0)),
                   pl.BlockSpec((TQ, GQA_KV * HEAD_DIM), lambda t: (t, 0))],
        out_shape=[jax.ShapeDtypeStruct((M, GQA_HEADS * HEAD_DIM), BF16),
                   jax.ShapeDtypeStruct((M, GQA_KV * HEAD_DIM), BF16)],
        compiler_params=_cparams(("parallel",), 32),
        name="gqa_prep",
    )(p, q_gain.reshape(1, HEAD_DIM), k_gain.reshape(1, HEAD_DIM), *tabs)


def _softmax_pv(s, v):
    m = jnp.max(s, axis=-1, keepdims=True)
    e = jnp.exp(s - m)
    l = jnp.sum(e, axis=-1, keepdims=True)
    return jnp.dot(e.astype(BF16), v, preferred_element_type=F32) / l


def _nt_dot(a, b):
    return lax.dot_general(a, b, (((1,), (1,)), ((), ())), preferred_element_type=F32)


def _gqa_attn_kernel(q_ref, k_ref, v_ref, o_ref):
    qi = pl.program_id(2)
    q = jnp.concatenate([q_ref[:, g * HEAD_DIM:(g + 1) * HEAD_DIM] for g in range(GQA_GROUP)], axis=0)

    def attend(n_keys):
        o = _softmax_pv(_nt_dot(q, k_ref[0:n_keys, :]), v_ref[0:n_keys, :].astype(BF16))
        for g in range(GQA_GROUP):
            o_ref[:, g * HEAD_DIM:(g + 1) * HEAD_DIM] = o[g * TQ:(g + 1) * TQ, :].astype(BF16)

    @pl.when(qi == 0)
    def _():
        attend(CTX)

    @pl.when(qi > 0)
    def _():
        attend(TB)


def _gqa_attn(q, k, p):
    gw = GQA_GROUP * HEAD_DIM
    return pl.pallas_call(
        _gqa_attn_kernel,
        grid=(BATCH, GQA_KV, QT_PER_BATCH),
        in_specs=[
            pl.BlockSpec((TQ, gw), lambda b, h, i: (b * QT_PER_BATCH + i, h)),
            pl.BlockSpec((TB, HEAD_DIM), lambda b, h, i: (b, h)),
            pl.BlockSpec((TB, HEAD_DIM), lambda b, h, i: (b, OFF_GV // HEAD_DIM + h)),
        ],
        out_specs=pl.BlockSpec((TQ, gw), lambda b, h, i: (b * QT_PER_BATCH + i, h)),
        out_shape=jax.ShapeDtypeStruct((M, GQA_HEADS * HEAD_DIM), BF16),
        compiler_params=_cparams(("parallel", "parallel", "arbitrary"), 48),
        name="gqa_attn",
    )(q, k, p)


def _mla_prep_kernel(cq_ref, ckv_ref, kr_ref, qg_ref, kvg_ref, wuq_ref, wukv_ref,
                     cos_ref, su_ref, sd_ref, q_ref, k_ref, v_ref):
    t = pl.program_id(0)
    is_ctx = (t % QT_PER_BATCH) == 0
    cos, su, sd = cos_ref[...], su_ref[...], sd_ref[...]
    quarter = MLA_ROPE // 4
    scale = MLA_QK ** -0.5

    cq = _head_norm(cq_ref[...], qg_ref[...]).astype(BF16)
    qf = jnp.dot(cq, wuq_ref[...].astype(BF16), preferred_element_type=F32)
    ckv = _head_norm(ckv_ref[...], kvg_ref[...]).astype(BF16)
    kvf = jnp.dot(ckv, wukv_ref[...].astype(BF16), preferred_element_type=F32)
    kr = kr_ref[...]
    kr = jnp.where(is_ctx, kr, _rope(kr, cos, su, sd, quarter)).astype(BF16)
    for hh in range(MLA_HEADS):
        base = hh * MLA_PAD
        q_ref[:, base:base + MLA_NOPE] = (qf[:, base:base + MLA_NOPE] * scale).astype(BF16)
        qr = qf[:, base + MLA_NOPE:base + MLA_PAD]
        qr = jnp.where(is_ctx, qr, _rope(qr, cos, su, sd, quarter))
        q_ref[:, base + MLA_NOPE:base + MLA_PAD] = (qr * scale).astype(BF16)
        k_ref[:, base:base + MLA_NOPE] = kvf[:, base:base + MLA_NOPE].astype(BF16)
        k_ref[:, base + MLA_NOPE:base + MLA_PAD] = kr
        v_ref[:, hh * MLA_V:(hh + 1) * MLA_V] = kvf[:, base + MLA_NOPE:base + MLA_PAD].astype(BF16)


def _mla_prep(p, pkr, q_gain, kv_gain, w_uq_pad, w_ukv, tabs):
    tab_spec = pl.BlockSpec((TQ, LANES), lambda t: (_rope_idx(t), 0))
    qk_w = MLA_HEADS * MLA_PAD
    return pl.pallas_call(
        _mla_prep_kernel,
        grid=(M // TQ,),
        in_specs=[
            pl.BlockSpec((TQ, MLA_Q_RANK), lambda t: (t, OFF_CQ // MLA_Q_RANK)),
            pl.BlockSpec((TQ, MLA_KV_RANK), lambda t: (t, OFF_CKV // MLA_KV_RANK)),
            pl.BlockSpec((TQ, LANES), lambda t: (t, 0)),
            pl.BlockSpec((1, MLA_Q_RANK), lambda t: (0, 0)),
            pl.BlockSpec((1, MLA_KV_RANK), lambda t: (0, 0)),
            pl.BlockSpec((MLA_Q_RANK, qk_w), lambda t: (0, 0)),
            pl.BlockSpec((MLA_KV_RANK, qk_w), lambda t: (0, 0)),
            tab_spec, tab_spec, tab_spec,
        ],
        out_specs=[pl.BlockSpec((TQ, qk_w), lambda t: (t, 0)),
                   pl.BlockSpec((TQ, qk_w), lambda t: (t, 0)),
                   pl.BlockSpec((TQ, MLA_HEADS * MLA_V), lambda t: (t, 0))],
        out_shape=[jax.ShapeDtypeStruct((M, qk_w), BF16),
                   jax.ShapeDtypeStruct((M, qk_w), BF16),
                   jax.ShapeDtypeStruct((M, MLA_HEADS * MLA_V), BF16)],
        compiler_params=_cparams(("parallel",), 32),
        name="mla_prep",
    )(p, p, pkr, q_gain.reshape(1, -1), kv_gain.reshape(1, -1), w_uq_pad, w_ukv, *tabs)


def _mla_attn_kernel(q_ref, k_ref, v_ref, o_ref):
    qi = pl.program_id(2)

    def attend(n_keys):
        o = _softmax_pv(_nt_dot(q_ref[...], k_ref[0:n_keys, :]), v_ref[0:n_keys, :])
        o_ref[...] = o.astype(BF16)

    @pl.when(qi == 0)
    def _():
        attend(CTX)

    @pl.when(qi > 0)
    def _():
        attend(TB)


def _mla_attn(q, k, v):
    return pl.pallas_call(
        _mla_attn_kernel,
        grid=(BATCH, MLA_HEADS, QT_PER_BATCH),
        in_specs=[
            pl.BlockSpec((TQ, MLA_PAD), lambda b, h, i: (b * QT_PER_BATCH + i, h)),
            pl.BlockSpec((TB, MLA_PAD), lambda b, h, i: (b, h)),
            pl.BlockSpec((TB, MLA_V), lambda b, h, i: (b, h)),
        ],
        out_specs=pl.BlockSpec((TQ, MLA_V), lambda b, h, i: (b * QT_PER_BATCH + i, h)),
        out_shape=jax.ShapeDtypeStruct((M, MLA_HEADS * MLA_V), BF16),
        compiler_params=_cparams(("parallel", "parallel", "arbitrary"), 32),
        name="mla_attn",
    )(q, k, v)


N_CHUNK = TB // HCHUNK
CTX_CHUNKS = CTX // HCHUNK
DIAG = 8
LEVELS = (8, 16, 32)


def _split3(x):
    hi = x.astype(BF16)
    r1 = x - hi.astype(F32)
    mid = r1.astype(BF16)
    lo = (r1 - mid.astype(F32)).astype(BF16)
    return hi, mid, lo


def _hgrn_kernel(q_ref, v_ref, z_ref, lg_ref, o_ref, cum_ref, k_ref, *, reverse, layer):
    lg = [lg_ref[l:l + 1, :] for l in range(DEPTH)]
    mx = functools.reduce(jnp.maximum, lg)
    ex = [jnp.exp(r - mx) for r in lg]
    den = functools.reduce(lambda a_, b_: a_ + b_, ex)
    lb = jnp.zeros((1, HG_DK), F32)
    for l in range(1, layer + 1):
        lb = lb + ex[l] / den
    log_lb = jnp.log(lb)
    log_1m = jnp.log1p(-lb)

    ti = lax.broadcasted_iota(jnp.int32, (HCHUNK, HCHUNK), 0)
    si = lax.broadcasted_iota(jnp.int32, (HCHUNK, HCHUNK), 1)
    causal = (si >= ti) if reverse else (si <= ti)
    tri = causal.astype(BF16)
    diag_mask = causal & ((ti // DIAG) == (si // DIAG))
    level_masks = []
    for hs in LEVELS:
        same = (ti // (2 * hs)) == (si // (2 * hs))
        t_hi = (ti // hs) % 2 == 1
        s_hi = (si // hs) % 2 == 1
        level_masks.append(same & (s_hi & ~t_hi if reverse else t_hi & ~s_hi))
    ri = lax.broadcasted_iota(jnp.int32, (DIAG * HG_DK, LANES), 0)
    ci = lax.broadcasted_iota(jnp.int32, (DIAG * HG_DK, LANES), 1)
    sel = ((ri // HG_DK) == (ci % DIAG)).astype(BF16)
    edge = 0 if reverse else HCHUNK - 1

    def chunk(jstep, state):
        if reverse:
            c = jnp.where(jstep < CTX_CHUNKS, CTX_CHUNKS - 1 - jstep, N_CHUNK + CTX_CHUNKS - 1 - jstep)
        else:
            c = jstep
        r0 = pl.multiple_of(c * HCHUNK, HCHUNK)
        z = z_ref[pl.ds(r0, HCHUNK), :]
        q = q_ref[pl.ds(r0, HCHUNK), :] * (HG_DK ** -0.5)
        v = v_ref[pl.ds(r0, HCHUNK), :].astype(BF16)
        log_sig = jnp.minimum(z, 0.0) - jnp.log1p(jnp.exp(-jnp.abs(z)))
        b_ = log_1m + log_sig
        hi_ = jnp.maximum(log_lb, b_)
        log_f = hi_ + jnp.log1p(jnp.exp(-jnp.abs(log_lb - b_)))
        k = (1.0 - lb) * _sigmoid(-z)

        parts = jnp.concatenate(_split3(log_f), axis=1)
        cum3 = jnp.dot(tri, parts, preferred_element_type=F32)
        cum = cum3[:, :HG_DK] + cum3[:, HG_DK:2 * HG_DK] + cum3[:, 2 * HG_DK:]
        cum_ref[...] = cum
        k_ref[...] = k
        tot = cum_ref[edge:edge + 1, :]

        cols = []
        for s in range(DIAG):
            cs = jnp.concatenate(
                [jnp.broadcast_to(cum_ref[blk * DIAG + s:blk * DIAG + s + 1, :], (DIAG, HG_DK))
                 for blk in range(HCHUNK // DIAG)], axis=0)
            ks = jnp.concatenate(
                [jnp.broadcast_to(k_ref[blk * DIAG + s:blk * DIAG + s + 1, :], (DIAG, HG_DK))
                 for blk in range(HCHUNK // DIAG)], axis=0)
            cols.append((q * jnp.exp(jnp.minimum(cum - cs, 0.0)) * ks).astype(BF16))
        pair = jnp.dot(jnp.concatenate(cols, axis=1), sel, preferred_element_type=F32)
        a = jnp.where(diag_mask, pair[:, :HCHUNK], 0.0)

        for hs, msk in zip(LEVELS, level_masks):
            refs = []
            for blk in range(HCHUNK // (2 * hs)):
                row = blk * 2 * hs + (hs if reverse else hs - 1)
                refs.append(jnp.broadcast_to(cum_ref[row:row + 1, :], (2 * hs, HG_DK)))
            rr = jnp.concatenate(refs, axis=0)
            qs = (q * jnp.exp(jnp.minimum(cum - rr, 0.0))).astype(BF16)
            ks = (k * jnp.exp(jnp.minimum(rr - cum, 0.0))).astype(BF16)
            a = jnp.where(msk, _nt_dot(qs, ks), a)

        o = jnp.dot(a.astype(BF16), v, preferred_element_type=F32)
        qe = (q * jnp.exp(cum)).astype(BF16)
        o = o + _nt_dot(qe, state.astype(BF16))
        o_ref[pl.ds(r0, HCHUNK), :] = o
        ke = (k * jnp.exp(tot - cum)).astype(BF16)
        upd = lax.dot_general(v, ke, (((0,), (0,)), ((), ())), preferred_element_type=F32)
        return state * jnp.exp(tot) + upd

    lax.fori_loop(0, N_CHUNK, chunk, jnp.zeros((HG_DK, HG_DK), F32))


def _hgrn(p, lb_logits_dir, layer, reverse):
    zoff = OFF_HB if reverse else OFF_HF
    return pl.pallas_call(
        functools.partial(_hgrn_kernel, reverse=reverse, layer=layer),
        grid=(BATCH, HG_HEADS),
        in_specs=[
            pl.BlockSpec((TB, HG_DK), lambda b, h: (b, OFF_HQ // HG_DK + h)),
            pl.BlockSpec((TB, HG_DK), lambda b, h: (b, OFF_HI // HG_DK + h)),
            pl.BlockSpec((TB, HG_DK), lambda b, h: (b, zoff // HG_DK + h)),
            pl.BlockSpec((DEPTH, HG_DK), lambda b, h: (0, h)),
        ],
        out_specs=pl.BlockSpec((TB, HG_DK), lambda b, h: (b, h)),
        out_shape=jax.ShapeDtypeStruct((M, HG_W), F32),
        scratch_shapes=[pltpu.VMEM((HCHUNK, HG_DK), F32), pltpu.VMEM((HCHUNK, HG_DK), F32)],
        compiler_params=_cparams(("parallel", "parallel"), 32),
        name="hgrn_bwd" if reverse else "hgrn_fwd",
    )(p, p, p, lb_logits_dir)


OUT_TN = 256


def _outproj_kernel(x_ref, gt_ref, oa_ref, of_ref, ob_ref, hg_ref, om_ref, ng_ref, w_ref, o_ref, lhs_ref):
    i = pl.program_id(0)
    j = pl.program_id(1)
    a_w = GQA_HEADS * HEAD_DIM

    @pl.when(j == 0)
    def _():
        lhs_ref[:, 0:a_w] = oa_ref[...]
        lhs_ref[:, a_w + HG_W:] = om_ref[...]
        gain = ng_ref[...]

        def body(c, carry):
            r0 = pl.multiple_of(c * ROWC, ROWC)
            for hh in range(HG_HEADS):
                sl = slice(hh * HG_DK, (hh + 1) * HG_DK)
                o = of_ref[pl.ds(r0, ROWC), sl] + ob_ref[pl.ds(r0, ROWC), sl]
                g = hg_ref[pl.ds(r0, ROWC), sl]
                y = _head_norm(o, gain) * (g * _sigmoid(g))
                lhs_ref[pl.ds(r0, ROWC), a_w + hh * HG_DK:a_w + (hh + 1) * HG_DK] = y.astype(BF16)
            return carry

        lax.fori_loop(0, TM // ROWC, body, 0)

    y = jnp.dot(lhs_ref[...], w_ref[...].astype(BF16), preferred_element_type=F32)
    batch = i // TILES_PER_BATCH
    part = i % TILES_PER_BATCH
    g_b, g_c = _mod_rows(gt_ref, batch)
    for c in range(TM // ROWC):
        is_ctx = (part * TM + c * ROWC) < CTX
        gate = jnp.where(is_ctx, g_c, g_b)
        rows = slice(c * ROWC, (c + 1) * ROWC)
        o_ref[rows, :] = x_ref[rows, :] + gate * y[rows, :]


def _outproj(x, mods, o_a, o_f, o_b, p, o_m, norm_gain, w_out, layer):
    a_w = GQA_HEADS * HEAD_DIM
    return pl.pallas_call(
        _outproj_kernel,
        grid=(M // TM, D // OUT_TN),
        in_specs=[
            pl.BlockSpec((TM, OUT_TN), lambda i, j: (i, j)),
            pl.BlockSpec((None, SUBLANES, OUT_TN), lambda i, j: (layer, 0, 5 * (D // OUT_TN) + j)),
            pl.BlockSpec((TM, a_w), lambda i, j: (i, 0)),
            pl.BlockSpec((TM, HG_W), lambda i, j: (i, 0)),
            pl.BlockSpec((TM, HG_W), lambda i, j: (i, 0)),
            pl.BlockSpec((TM, HG_W), lambda i, j: (i, OFF_HG // HG_W)),
            pl.BlockSpec((TM, MLA_HEADS * MLA_V), lambda i, j: (i, 0)),
            pl.BlockSpec((1, HG_DK), lambda i, j: (0, 0)),
            pl.BlockSpec((None, D, OUT_TN), lambda i, j: (layer, 0, j)),
        ],
        out_specs=pl.BlockSpec((TM, OUT_TN), lambda i, j: (i, j)),
        out_shape=jax.ShapeDtypeStruct((M, D), F32),
        scratch_shapes=[pltpu.VMEM((TM, D), BF16)],
        compiler_params=_cparams(("parallel", "arbitrary"), 48),
        name="outproj",
    )(x, mods, o_a, o_f, o_b, p, o_m, norm_gain.reshape(1, HG_DK), w_out)


def _final_kernel(x_ref, g_ref, o_ref):
    x = x_ref[...]
    o_ref[...] = x * lax.rsqrt(jnp.mean(x * x, axis=-1, keepdims=True) + EPS) * g_ref[...]


def _final_norm(x, gain):
    lat_tiles = SEQ // TQ
    return pl.pallas_call(
        _final_kernel,
        grid=(BATCH, lat_tiles),
        in_specs=[pl.BlockSpec((TQ, D), lambda b, i: (b * QT_PER_BATCH + CTX // TQ + i, 0)),
                  pl.BlockSpec((1, D), lambda b, i: (0, 0))],
        out_specs=pl.BlockSpec((None, TQ, D), lambda b, i: (b, i, 0)),
        out_shape=jax.ShapeDtypeStruct((BATCH, SEQ, D), F32),
        compiler_params=_cparams(("parallel", "parallel"), 32),
        name="final_norm",
    )(x, gain.reshape(1, D))


def _pad_w_uq(w_uq):
    w = w_uq.reshape(DEPTH, MLA_Q_RANK, MLA_HEADS, MLA_QK)
    w = jnp.pad(w, ((0, 0), (0, 0), (0, 0), (0, MLA_PAD - MLA_QK)))
    return w.reshape(DEPTH, MLA_Q_RANK, MLA_HEADS * MLA_PAD)


def kernel(x, c, ctx, c_ctx, w_mod, b_mod, w_ffn1_in, w_ffn1_out, w_in, w_uq, w_ukv, w_out,
           gqa_q_gain, gqa_k_gain, mla_q_gain, mla_kv_gain, hgrn_lb_logits, hgrn_norm_gain,
           w_ffn2_in, w_ffn2_out, final_gain):
    xa = jnp.concatenate([ctx, x], axis=1).reshape(M, D)
    cvec8 = jnp.concatenate([c, c_ctx[None, :], jnp.zeros((SUBLANES - BATCH - 1, D), F32)], axis=0)
    mods = _modulation(cvec8, w_mod, b_mod)
    tabs_h = _rope_tables(HEAD_DIM)
    tabs_r = _rope_tables(MLA_ROPE)
    w_uq_pad = _pad_w_uq(w_uq)
    w_kr = jnp.pad(w_in[:, :, OFF_KR:], ((0, 0), (0, 0), (0, LANES - MLA_ROPE)))

    for l in range(DEPTH):
        xa = _ffn(xa, mods, w_ffn1_in, w_ffn1_out, l, 0)
        p, pkr = _inproj(xa, mods, w_in, w_kr[l], l)
        q_a, k_a = _gqa_prep(p, gqa_q_gain[l], gqa_k_gain[l], tabs_h)
        o_a = _gqa_attn(q_a, k_a, p)
        q_m, k_m, v_m = _mla_prep(p, pkr, mla_q_gain[l], mla_kv_gain[l], w_uq_pad[l], w_ukv[l], tabs_r)
        o_m = _mla_attn(q_m, k_m, v_m)
        o_f = _hgrn(p, hgrn_lb_logits[0], l, False)
        o_b = _hgrn(p, hgrn_lb_logits[1], l, True)
        xa = _outproj(xa, mods, o_a, o_f, o_b, p, o_m, hgrn_norm_gain[l], w_out, l)
        xa = _ffn(xa, mods, w_ffn2_in, w_ffn2_out, l, 6)
    return _final_norm(xa, final_gain)
```

```python
import functools

import jax
import jax.numpy as jnp
from jax import lax
from jax.experimental import pallas as pl
from jax.experimental.pallas import tpu as pltpu

F32 = jnp.float32
BF16 = jnp.bfloat16

D = 2048
BATCH = 4
SEQ = 2048
DEPTH = 2
GRID_W = 64
CTX = 256
EPS = 1e-6
ROPE_THETA = 10000.0

HEAD_DIM = 128
GQA_HEADS = 8
GQA_KV = 2
GQA_GROUP = GQA_HEADS // GQA_KV
HG_HEADS = 4
HG_DK = 128
HG_W = HG_HEADS * HG_DK
MLA_HEADS = 4
MLA_Q_RANK = 512
MLA_KV_RANK = 256
MLA_NOPE = 128
MLA_ROPE = 64
MLA_V = 128
MLA_QK = MLA_NOPE + MLA_ROPE
MLA_PAD = 256
FFN_H = 5504
N_MOD = 9

TB = CTX + SEQ
M = BATCH * TB
OFF_GQ, OFF_GK, OFF_GV = 0, 1024, 1280
OFF_HQ, OFF_HI, OFF_HF, OFF_HB, OFF_HG = 1536, 2048, 2560, 3072, 3584
OFF_CQ, OFF_CKV, OFF_KR = 4096, 4608, 4864
IN_MAIN = OFF_KR
PA_W, PZ_W, PB_W = OFF_HF, OFF_HG - OFF_HF, OFF_KR - OFF_HG

LANES = 128
SUBLANES = 8
MXU_W = 256

TM = TB // 2
TMP = TB // 4
ROWC = 64
TQ = 256
QT_PER_BATCH = TB // TQ
HCHUNK = 64
FFN_TH = 512
FFN_HP = -(-FFN_H // FFN_TH) * FFN_TH
FFN_STEPS = FFN_HP // FFN_TH
FFN_RC = 384
MOD_TN = 1024
PROJ_TN = MXU_W


def _cparams(sem, vmem_mb):
    return pltpu.CompilerParams(dimension_semantics=sem, vmem_limit_bytes=vmem_mb * 1024 * 1024)


def _sigmoid(x):
    return 1.0 / (1.0 + jnp.exp(-x))


def _mod_kernel(c_ref, w_ref, b_ref, o_ref):
    c = c_ref[...]
    a = (c * _sigmoid(c)).astype(BF16)
    o_ref[...] = jnp.dot(a, w_ref[...].astype(BF16), preferred_element_type=F32) + b_ref[...]


def _modulation(cvec8, w_mod, b_mod):
    n = N_MOD * D
    return pl.pallas_call(
        _mod_kernel,
        grid=(DEPTH, n // MOD_TN),
        in_specs=[
            pl.BlockSpec((SUBLANES, D), lambda l, j: (0, 0)),
            pl.BlockSpec((None, D, MOD_TN), lambda l, j: (l, 0, j)),
            pl.BlockSpec((None, 1, MOD_TN), lambda l, j: (l, 0, j)),
        ],
        out_specs=pl.BlockSpec((None, SUBLANES, MOD_TN), lambda l, j: (l, 0, j)),
        out_shape=jax.ShapeDtypeStruct((DEPTH, SUBLANES, n), F32),
        compiler_params=_cparams(("parallel", "parallel"), 40),
        name="modulation",
    )(cvec8, w_mod, b_mod.reshape(DEPTH, 1, n))


def _mod_rows(ref, batch):
    return ref[pl.ds(batch, 1), :], ref[4:5, :]


def _chunk_is_ctx(tile, tm, c):
    return ((tile % (TB // tm)) * tm + c * ROWC) < CTX


def _modulate_tile(x_ref, shift_ref, scale_ref, h_ref, tile, tm):
    batch = tile // (TB // tm)
    sh_b, sh_c = _mod_rows(shift_ref, batch)
    sc_b, sc_c = _mod_rows(scale_ref, batch)

    def body(c, carry):
        is_ctx = _chunk_is_ctx(tile, tm, c)
        sh = jnp.where(is_ctx, sh_c, sh_b)
        sc = 1.0 + jnp.where(is_ctx, sc_c, sc_b)
        r0 = pl.multiple_of(c * ROWC, ROWC)
        xc = x_ref[pl.ds(r0, ROWC), :]
        ms = jnp.mean(xc * xc, axis=-1, keepdims=True)
        h_ref[pl.ds(r0, ROWC), :] = (xc * lax.rsqrt(ms + EPS) * sc + sh).astype(BF16)
        return carry

    lax.fori_loop(0, tm // ROWC, body, 0)


def _ffn_kernel(x_ref, sh_ref, sc_ref, gt_ref, wi_ref, wo_ref, o_ref, h_ref):
    i = pl.program_id(0)
    j = pl.program_id(1)

    @pl.when(j == 0)
    def _():
        _modulate_tile(x_ref, sh_ref, sc_ref, h_ref, i, TM)
        o_ref[...] = jnp.zeros_like(o_ref)

    for rc in range(TM // FFN_RC):
        rows = slice(rc * FFN_RC, (rc + 1) * FFN_RC)
        r = jnp.dot(h_ref[rows, :], wi_ref[...], preferred_element_type=F32)
        g = r[:, :FFN_TH]
        u = r[:, FFN_TH:]
        a = (g * _sigmoid(g) * u).astype(BF16)
        o_ref[rows, :] += jnp.dot(a, wo_ref[...], preferred_element_type=F32)

    @pl.when(j == FFN_STEPS - 1)
    def _():
        g_b, g_c = _mod_rows(gt_ref, i // (TB // TM))

        def body(c, carry):
            gate = 0.5 * jnp.where(_chunk_is_ctx(i, TM, c), g_c, g_b)
            r0 = pl.multiple_of(c * ROWC, ROWC)
            o_ref[pl.ds(r0, ROWC), :] = x_ref[pl.ds(r0, ROWC), :] + gate * o_ref[pl.ds(r0, ROWC), :]
            return carry

        lax.fori_loop(0, TM // ROWC, body, 0)


def _ffn_weights(w_in, w_out):
    pad = FFN_HP - FFN_H
    gate = jnp.pad(w_in[:, :, :FFN_H], ((0, 0), (0, 0), (0, pad))).reshape(DEPTH, D, FFN_STEPS, FFN_TH)
    up = jnp.pad(w_in[:, :, FFN_H:], ((0, 0), (0, 0), (0, pad))).reshape(DEPTH, D, FFN_STEPS, FFN_TH)
    wi = jnp.concatenate([gate, up], axis=-1).transpose(0, 2, 1, 3).astype(BF16)
    wo = jnp.pad(w_out, ((0, 0), (0, pad), (0, 0))).reshape(DEPTH, FFN_STEPS, FFN_TH, D).astype(BF16)
    return wi, wo


def _ffn(x, mods, wi, wo, layer, mod0):
    def mod_spec(k):
        return pl.BlockSpec((None, SUBLANES, D), lambda i, j: (layer, 0, mod0 + k))

    return pl.pallas_call(
        _ffn_kernel,
        grid=(M // TM, FFN_STEPS),
        in_specs=[
            pl.BlockSpec((TM, D), lambda i, j: (i, 0), pipeline_mode=pl.Buffered(1)),
            mod_spec(0), mod_spec(1), mod_spec(2),
            pl.BlockSpec((None, None, D, 2 * FFN_TH), lambda i, j: (layer, j, 0, 0)),
            pl.BlockSpec((None, None, FFN_TH, D), lambda i, j: (layer, j, 0, 0)),
        ],
        out_specs=pl.BlockSpec((TM, D), lambda i, j: (i, 0)),
        out_shape=jax.ShapeDtypeStruct((M, D), F32),
        scratch_shapes=[pltpu.VMEM((TM, D), BF16)],
        compiler_params=_cparams(("parallel", "arbitrary"), 58),
        name="ffn",
    )(x, mods, mods, mods, wi, wo)


def _proj_tiles(w):
    k, n = w.shape[1], w.shape[2]
    return w.reshape(DEPTH, k, n // PROJ_TN, PROJ_TN).transpose(0, 2, 1, 3).astype(BF16)


def _inproj_kernel(x_ref, sh_ref, sc_ref, w_ref, wkr_ref, pa_ref, pz_ref, pb_ref, kr_ref, h_ref):
    _modulate_tile(x_ref, sh_ref, sc_ref, h_ref, pl.program_id(0), TMP)
    h = h_ref[...]
    kr_ref[...] = jnp.dot(h, wkr_ref[...], preferred_element_type=F32)
    for t in range(IN_MAIN // PROJ_TN):
        y = jnp.dot(h, w_ref[t], preferred_element_type=F32)
        c0 = t * PROJ_TN
        if c0 < OFF_HF:
            pa_ref[:, c0:c0 + PROJ_TN] = y.astype(BF16)
        elif c0 < OFF_HG:
            pz_ref[:, c0 - OFF_HF:c0 - OFF_HF + PROJ_TN] = y
        else:
            pb_ref[:, c0 - OFF_HG:c0 - OFF_HG + PROJ_TN] = y.astype(BF16)


def _inproj(x, mods, w_tiles, w_kr, layer):
    def mod_spec(k):
        return pl.BlockSpec((None, SUBLANES, D), lambda i: (layer, 0, k))

    def out_spec(w):
        return pl.BlockSpec((TMP, w), lambda i: (i, 0))

    n_tiles = IN_MAIN // PROJ_TN
    return pl.pallas_call(
        _inproj_kernel,
        grid=(M // TMP,),
        in_specs=[
            pl.BlockSpec((TMP, D), lambda i: (i, 0)),
            mod_spec(3), mod_spec(4),
            pl.BlockSpec((None, n_tiles, D, PROJ_TN), lambda i: (layer, 0, 0, 0), pipeline_mode=pl.Buffered(1)),
            pl.BlockSpec((None, D, LANES), lambda i: (layer, 0, 0)),
        ],
        out_specs=[out_spec(PA_W), out_spec(PZ_W), out_spec(PB_W), out_spec(LANES)],
        out_shape=[jax.ShapeDtypeStruct((M, PA_W), BF16), jax.ShapeDtypeStruct((M, PZ_W), F32),
                   jax.ShapeDtypeStruct((M, PB_W), BF16), jax.ShapeDtypeStruct((M, LANES), F32)],
        scratch_shapes=[pltpu.VMEM((TMP, D), BF16)],
        compiler_params=_cparams(("parallel",), 56),
        name="inproj",
    )(x, mods, mods, w_tiles, w_kr)


def _rope_tables(rot_dim):
    rows = SEQ // GRID_W
    row = jnp.repeat(jnp.arange(rows, dtype=F32), GRID_W)
    colp = jnp.tile(jnp.arange(GRID_W, dtype=F32), rows)
    axis_dim = rot_dim // 2
    inv_freq = ROPE_THETA ** (-jnp.arange(0, axis_dim, 2, dtype=F32) / axis_dim)
    ang_r = row[:, None] * inv_freq
    ang_c = colp[:, None] * inv_freq
    ang = jnp.concatenate([ang_r, ang_r, ang_c, ang_c], axis=-1)
    cos, sin = jnp.cos(ang), jnp.sin(ang)
    quarter = rot_dim // 4
    lane = jnp.arange(rot_dim)
    first = (lane % (2 * quarter)) < quarter
    sin_up = jnp.where(first, -sin, 0.0)
    sin_dn = jnp.where(first, 0.0, sin)
    pad = LANES - rot_dim
    if pad:
        cos = jnp.pad(cos, ((0, 0), (0, pad)), constant_values=1.0)
        sin_up = jnp.pad(sin_up, ((0, 0), (0, pad)))
        sin_dn = jnp.pad(sin_dn, ((0, 0), (0, pad)))
    return cos, sin_up, sin_dn


def _rope(x, cos, sin_up, sin_dn, quarter):
    up = pltpu.roll(x, LANES - quarter, 1)
    dn = pltpu.roll(x, quarter, 1)
    return x * cos + up * sin_up + dn * sin_dn


def _head_norm(x, gain):
    return x * lax.rsqrt(jnp.mean(x * x, axis=-1, keepdims=True) + EPS) * gain


def _gqa_prep_kernel(p_ref, qg_ref, kg_ref, cos_ref, su_ref, sd_ref, q_ref, k_ref):
    t = pl.program_id(0)
    is_ctx = (t % QT_PER_BATCH) == 0
    cos, su, sd = cos_ref[...], su_ref[...], sd_ref[...]
    scale = HEAD_DIM ** -0.5
    for hh in range(GQA_HEADS + GQA_KV):
        xh = p_ref[:, hh * HEAD_DIM:(hh + 1) * HEAD_DIM].astype(F32)
        is_q = hh < GQA_HEADS
        n = _head_norm(xh, qg_ref[...] if is_q else kg_ref[...])
        y = jnp.where(is_ctx, n, _rope(n, cos, su, sd, HEAD_DIM // 4))
        if is_q:
            q_ref[:, hh * HEAD_DIM:(hh + 1) * HEAD_DIM] = (y * scale).astype(BF16)
        else:
            kk = hh - GQA_HEADS
            k_ref[:, kk * HEAD_DIM:(kk + 1) * HEAD_DIM] = y.astype(BF16)


def _rope_idx(t):
    return jnp.maximum(t % QT_PER_BATCH - 1, 0)


def _gqa_prep(pa, q_gain, k_gain, tabs):
    width = (GQA_HEADS + GQA_KV) * HEAD_DIM
    tab_spec = pl.BlockSpec((TQ, LANES), lambda t: (_rope_idx(t), 0))
    vec_spec = pl.BlockSpec((1, HEAD_DIM), lambda t: (0, 0))
    return pl.pallas_call(
        _gqa_prep_kernel,
        grid=(M // TQ,),
        in_specs=[pl.BlockSpec((TQ, width), lambda t: (t, 0)), vec_spec, vec_spec,
                  tab_spec, tab_spec, tab_spec],
        out_specs=[pl.BlockSpec((TQ, GQA_HEADS * HEAD_DIM), lambda t: (t, 0)),
                   pl.BlockSpec((TQ, GQA_KV * HEAD_DIM), lambda t: (t, 0))],
        out_shape=[jax.ShapeDtypeStruct((M, GQA_HEADS * HEAD_DIM), BF16),
                   jax.ShapeDtypeStruct((M, GQA_KV * HEAD_DIM), BF16)],
        compiler_params=_cparams(("parallel",), 32),
        name="gqa_prep",
    )(pa, q_gain.reshape(1, HEAD_DIM), k_gain.reshape(1, HEAD_DIM), *tabs)


def _softmax_pv(s, v):
    m = jnp.max(s, axis=-1, keepdims=True)
    e = jnp.exp(s - m)
    l = jnp.sum(e, axis=-1, keepdims=True)
    return jnp.dot(e.astype(BF16), v, preferred_element_type=F32) / l


def _nt_dot(a, b):
    return lax.dot_general(a, b, (((1,), (1,)), ((), ())), preferred_element_type=F32)


def _gqa_attn_kernel(q_ref, k_ref, v_ref, o_ref):
    def attend(n_keys):
        for hh in range(GQA_HEADS):
            kv = slice((hh // GQA_GROUP) * HEAD_DIM, (hh // GQA_GROUP + 1) * HEAD_DIM)
            hs = slice(hh * HEAD_DIM, (hh + 1) * HEAD_DIM)
            s = _nt_dot(q_ref[:, hs], k_ref[0:n_keys, kv])
            o_ref[:, hs] = _softmax_pv(s, v_ref[0:n_keys, kv]).astype(BF16)

    @pl.when(pl.program_id(1) == 0)
    def _():
        attend(CTX)

    @pl.when(pl.program_id(1) > 0)
    def _():
        attend(TB)


def _gqa_attn(q, k, pa):
    qw, kw = GQA_HEADS * HEAD_DIM, GQA_KV * HEAD_DIM
    return pl.pallas_call(
        _gqa_attn_kernel,
        grid=(BATCH, QT_PER_BATCH),
        in_specs=[
            pl.BlockSpec((TQ, qw), lambda b, i: (b * QT_PER_BATCH + i, 0)),
            pl.BlockSpec((TB, kw), lambda b, i: (b, 0)),
            pl.BlockSpec((TB, kw), lambda b, i: (b, OFF_GV // kw)),
        ],
        out_specs=pl.BlockSpec((TQ, qw), lambda b, i: (b * QT_PER_BATCH + i, 0)),
        out_shape=jax.ShapeDtypeStruct((M, qw), BF16),
        compiler_params=_cparams(("parallel", "arbitrary"), 48),
        name="gqa_attn",
    )(q, k, pa)


def _mla_prep_kernel(cq_ref, ckv_ref, kr_ref, qg_ref, kvg_ref, wuq_ref, wukv_ref,
                     cos_ref, su_ref, sd_ref, q_ref, k_ref, v_ref):
    t = pl.program_id(0)
    is_ctx = (t % QT_PER_BATCH) == 0
    cos, su, sd = cos_ref[...], su_ref[...], sd_ref[...]
    quarter = MLA_ROPE // 4
    scale = MLA_QK ** -0.5

    cq = _head_norm(cq_ref[...].astype(F32), qg_ref[...]).astype(BF16)
    qf = jnp.dot(cq, wuq_ref[...], preferred_element_type=F32)
    ckv = _head_norm(ckv_ref[...].astype(F32), kvg_ref[...]).astype(BF16)
    kvf = jnp.dot(ckv, wukv_ref[...], preferred_element_type=F32)
    kr = kr_ref[...]
    kr = jnp.where(is_ctx, kr, _rope(kr, cos, su, sd, quarter)).astype(BF16)
    for hh in range(MLA_HEADS):
        base = hh * MLA_PAD
        q_ref[:, base:base + MLA_NOPE] = (qf[:, base:base + MLA_NOPE] * scale).astype(BF16)
        qr = qf[:, base + MLA_NOPE:base + MLA_PAD]
        qr = jnp.where(is_ctx, qr, _rope(qr, cos, su, sd, quarter))
        q_ref[:, base + MLA_NOPE:base + MLA_PAD] = (qr * scale).astype(BF16)
        k_ref[:, base:base + MLA_NOPE] = kvf[:, base:base + MLA_NOPE].astype(BF16)
        k_ref[:, base + MLA_NOPE:base + MLA_PAD] = kr
        v_ref[:, hh * MLA_V:(hh + 1) * MLA_V] = kvf[:, base + MLA_NOPE:base + MLA_PAD].astype(BF16)


def _mla_prep(pb, pkr, q_gain, kv_gain, w_uq_pad, w_ukv, tabs):
    tab_spec = pl.BlockSpec((TQ, LANES), lambda t: (_rope_idx(t), 0))
    qk_w = MLA_HEADS * MLA_PAD
    return pl.pallas_call(
        _mla_prep_kernel,
        grid=(M // TQ,),
        in_specs=[
            pl.BlockSpec((TQ, MLA_Q_RANK), lambda t: (t, (OFF_CQ - OFF_HG) // MLA_Q_RANK)),
            pl.BlockSpec((TQ, MLA_KV_RANK), lambda t: (t, (OFF_CKV - OFF_HG) // MLA_KV_RANK)),
            pl.BlockSpec((TQ, LANES), lambda t: (t, 0)),
            pl.BlockSpec((1, MLA_Q_RANK), lambda t: (0, 0)),
            pl.BlockSpec((1, MLA_KV_RANK), lambda t: (0, 0)),
            pl.BlockSpec((MLA_Q_RANK, qk_w), lambda t: (0, 0)),
            pl.BlockSpec((MLA_KV_RANK, qk_w), lambda t: (0, 0)),
            tab_spec, tab_spec, tab_spec,
        ],
        out_specs=[pl.BlockSpec((TQ, qk_w), lambda t: (t, 0)),
                   pl.BlockSpec((TQ, qk_w), lambda t: (t, 0)),
                   pl.BlockSpec((TQ, MLA_HEADS * MLA_V), lambda t: (t, 0))],
        out_shape=[jax.ShapeDtypeStruct((M, qk_w), BF16),
                   jax.ShapeDtypeStruct((M, qk_w), BF16),
                   jax.ShapeDtypeStruct((M, MLA_HEADS * MLA_V), BF16)],
        compiler_params=_cparams(("parallel",), 32),
        name="mla_prep",
    )(pb, pb, pkr, q_gain.reshape(1, -1), kv_gain.reshape(1, -1), w_uq_pad, w_ukv, *tabs)


def _mla_attn_kernel(q_ref, k_ref, v_ref, o_ref):
    def attend(n_keys):
        for hh in range(MLA_HEADS):
            qs = slice(hh * MLA_PAD, (hh + 1) * MLA_PAD)
            vs = slice(hh * MLA_V, (hh + 1) * MLA_V)
            s = _nt_dot(q_ref[:, qs], k_ref[0:n_keys, qs])
            o_ref[:, vs] = _softmax_pv(s, v_ref[0:n_keys, vs]).astype(BF16)

    @pl.when(pl.program_id(1) == 0)
    def _():
        attend(CTX)

    @pl.when(pl.program_id(1) > 0)
    def _():
        attend(TB)


def _mla_attn(q, k, v):
    qw, vw = MLA_HEADS * MLA_PAD, MLA_HEADS * MLA_V
    return pl.pallas_call(
        _mla_attn_kernel,
        grid=(BATCH, QT_PER_BATCH),
        in_specs=[
            pl.BlockSpec((TQ, qw), lambda b, i: (b * QT_PER_BATCH + i, 0)),
            pl.BlockSpec((TB, qw), lambda b, i: (b, 0)),
            pl.BlockSpec((TB, vw), lambda b, i: (b, 0)),
        ],
        out_specs=pl.BlockSpec((TQ, vw), lambda b, i: (b * QT_PER_BATCH + i, 0)),
        out_shape=jax.ShapeDtypeStruct((M, vw), BF16),
        compiler_params=_cparams(("parallel", "arbitrary"), 48),
        name="mla_attn",
    )(q, k, v)


N_CHUNK = TB // HCHUNK
CTX_CHUNKS = CTX // HCHUNK
DIAG = 8
LEVELS = (8, 16, 32)


def _split3(x):
    hi = x.astype(BF16)
    r1 = x - hi.astype(F32)
    mid = r1.astype(BF16)
    lo = (r1 - mid.astype(F32)).astype(BF16)
    return hi, mid, lo


def _hgrn_kernel(q_ref, v_ref, z_ref, lg_ref, o_ref, cum_ref, k_ref, s_ref, *, reverse, layer):
    lg = [lg_ref[l:l + 1, :] for l in range(DEPTH)]
    mx = functools.reduce(jnp.maximum, lg)
    ex = [jnp.exp(r - mx) for r in lg]
    den = functools.reduce(lambda a_, b_: a_ + b_, ex)
    lb_all = jnp.zeros((1, HG_W), F32)
    for l in range(1, layer + 1):
        lb_all = lb_all + ex[l] / den
    log_lb_all = jnp.log(lb_all)
    log_1m_all = jnp.log1p(-lb_all)

    ti = lax.broadcasted_iota(jnp.int32, (HCHUNK, HCHUNK), 0)
    si = lax.broadcasted_iota(jnp.int32, (HCHUNK, HCHUNK), 1)
    causal = (si >= ti) if reverse else (si <= ti)
    tri = causal.astype(BF16)
    diag_mask = causal & ((ti // DIAG) == (si // DIAG))
    level_masks = []
    for hs in LEVELS:
        same = (ti // (2 * hs)) == (si // (2 * hs))
        t_hi = (ti // hs) % 2 == 1
        s_hi = (si // hs) % 2 == 1
        level_masks.append(same & (s_hi & ~t_hi if reverse else t_hi & ~s_hi))
    ri = lax.broadcasted_iota(jnp.int32, (DIAG * HG_DK, LANES), 0)
    ci = lax.broadcasted_iota(jnp.int32, (DIAG * HG_DK, LANES), 1)
    sel = ((ri // HG_DK) == (ci % DIAG)).astype(BF16)
    edge = 0 if reverse else HCHUNK - 1

    s_ref[...] = jnp.zeros_like(s_ref)

    def head_chunk(hh, r0):
        hs_ = slice(hh * HG_DK, (hh + 1) * HG_DK)
        lb, log_lb, log_1m = lb_all[:, hs_], log_lb_all[:, hs_], log_1m_all[:, hs_]
        z = z_ref[pl.ds(r0, HCHUNK), hs_]
        q = q_ref[pl.ds(r0, HCHUNK), hs_].astype(F32) * (HG_DK ** -0.5)
        v = v_ref[pl.ds(r0, HCHUNK), hs_]
        log_sig = jnp.minimum(z, 0.0) - jnp.log1p(jnp.exp(-jnp.abs(z)))
        b_ = log_1m + log_sig
        log_f = jnp.maximum(log_lb, b_) + jnp.log1p(jnp.exp(-jnp.abs(log_lb - b_)))
        k = (1.0 - lb) * _sigmoid(-z)

        parts = jnp.concatenate(_split3(log_f), axis=1)
        cum3 = jnp.dot(tri, parts, preferred_element_type=F32)
        cum = cum3[:, :HG_DK] + cum3[:, HG_DK:2 * HG_DK] + cum3[:, 2 * HG_DK:]
        cum_ref[hh] = cum
        k_ref[hh] = k
        tot = cum_ref[hh, edge:edge + 1, :]

        cols = []
        for s in range(DIAG):
            cs = jnp.concatenate(
                [jnp.broadcast_to(cum_ref[hh, blk * DIAG + s:blk * DIAG + s + 1, :], (DIAG, HG_DK))
                 for blk in range(HCHUNK // DIAG)], axis=0)
            ks = jnp.concatenate(
                [jnp.broadcast_to(k_ref[hh, blk * DIAG + s:blk * DIAG + s + 1, :], (DIAG, HG_DK))
                 for blk in range(HCHUNK // DIAG)], axis=0)
            cols.append((q * jnp.exp(jnp.minimum(cum - cs, 0.0)) * ks).astype(BF16))
        pair = jnp.dot(jnp.concatenate(cols, axis=1), sel, preferred_element_type=F32)
        a = jnp.where(diag_mask, pair[:, :HCHUNK], 0.0)

        for hs, msk in zip(LEVELS, level_masks):
            refs = []
            for blk in range(HCHUNK // (2 * hs)):
                row = blk * 2 * hs + (hs if reverse else hs - 1)
                refs.append(jnp.broadcast_to(cum_ref[hh, row:row + 1, :], (2 * hs, HG_DK)))
            rr = jnp.concatenate(refs, axis=0)
            qs = (q * jnp.exp(jnp.minimum(cum - rr, 0.0))).astype(BF16)
            ks = (k * jnp.exp(jnp.minimum(rr - cum, 0.0))).astype(BF16)
            a = jnp.where(msk, _nt_dot(qs, ks), a)

        o = jnp.dot(a.astype(BF16), v, preferred_element_type=F32)
        state = s_ref[hh]
        qe = (q * jnp.exp(cum)).astype(BF16)
        o_ref[pl.ds(r0, HCHUNK), hs_] = o + _nt_dot(qe, state.astype(BF16))
        ke = (k * jnp.exp(tot - cum)).astype(BF16)
        upd = lax.dot_general(v, ke, (((0,), (0,)), ((), ())), preferred_element_type=F32)
        s_ref[hh] = state * jnp.exp(tot) + upd

    def chunk(jstep, carry):
        if reverse:
            c = jnp.where(jstep < CTX_CHUNKS, CTX_CHUNKS - 1 - jstep, N_CHUNK + CTX_CHUNKS - 1 - jstep)
        else:
            c = jstep
        r0 = pl.multiple_of(c * HCHUNK, HCHUNK)
        for hh in range(HG_HEADS):
            head_chunk(hh, r0)
        return carry

    lax.fori_loop(0, N_CHUNK, chunk, 0)


def _hgrn(pa, pz, lb_logits_dir, layer, reverse):
    return pl.pallas_call(
        functools.partial(_hgrn_kernel, reverse=reverse, layer=layer),
        grid=(BATCH,),
        in_specs=[
            pl.BlockSpec((TB, HG_W), lambda b: (b, OFF_HQ // HG_W)),
            pl.BlockSpec((TB, HG_W), lambda b: (b, OFF_HI // HG_W)),
            pl.BlockSpec((TB, HG_W), lambda b: (b, 1 if reverse else 0)),
            pl.BlockSpec((DEPTH, HG_W), lambda b: (0, 0)),
        ],
        out_specs=pl.BlockSpec((TB, HG_W), lambda b: (b, 0)),
        out_shape=jax.ShapeDtypeStruct((M, HG_W), F32),
        scratch_shapes=[pltpu.VMEM((HG_HEADS, HCHUNK, HG_DK), F32),
                        pltpu.VMEM((HG_HEADS, HCHUNK, HG_DK), F32),
                        pltpu.VMEM((HG_HEADS, HG_DK, HG_DK), F32)],
        compiler_params=_cparams(("parallel",), 48),
        name="hgrn_bwd" if reverse else "hgrn_fwd",
    )(pa, pa, pz, lb_logits_dir)


def _outproj_kernel(x_ref, gt_ref, oa_ref, of_ref, ob_ref, hg_ref, om_ref, ng_ref, w_ref, o_ref, lhs_ref):
    i = pl.program_id(0)
    a_w = GQA_HEADS * HEAD_DIM
    lhs_ref[:, 0:a_w] = oa_ref[...]
    lhs_ref[:, a_w + HG_W:] = om_ref[...]
    gain = ng_ref[...]
    for hh in range(HG_HEADS):
        sl = slice(hh * HG_DK, (hh + 1) * HG_DK)
        g = hg_ref[:, sl].astype(F32)
        y = _head_norm(of_ref[:, sl] + ob_ref[:, sl], gain) * (g * _sigmoid(g))
        lhs_ref[:, a_w + hh * HG_DK:a_w + (hh + 1) * HG_DK] = y.astype(BF16)

    lhs = lhs_ref[...]
    g_b, g_c = _mod_rows(gt_ref, i // (TB // TMP))
    for n in range(D // PROJ_TN):
        cols = slice(n * PROJ_TN, (n + 1) * PROJ_TN)
        y = jnp.dot(lhs, w_ref[n], preferred_element_type=F32)
        for c in range(TMP // ROWC):
            gate = jnp.where(_chunk_is_ctx(i, TMP, c), g_c[:, cols], g_b[:, cols])
            rows = slice(c * ROWC, (c + 1) * ROWC)
            o_ref[rows, cols] = x_ref[rows, cols] + gate * y[rows, :]


def _outproj(x, mods, o_a, o_f, o_b, pb, o_m, norm_gain, w_tiles, layer):
    def row_spec(w, col=0):
        return pl.BlockSpec((TMP, w), lambda i: (i, col))

    return pl.pallas_call(
        _outproj_kernel,
        grid=(M // TMP,),
        in_specs=[
            row_spec(D),
            pl.BlockSpec((None, SUBLANES, D), lambda i: (layer, 0, 5)),
            row_spec(GQA_HEADS * HEAD_DIM), row_spec(HG_W), row_spec(HG_W),
            row_spec(HG_W, 0),
            row_spec(MLA_HEADS * MLA_V),
            pl.BlockSpec((1, HG_DK), lambda i: (0, 0)),
            pl.BlockSpec((None, D // PROJ_TN, D, PROJ_TN), lambda i: (layer, 0, 0, 0),
                         pipeline_mode=pl.Buffered(1)),
        ],
        out_specs=row_spec(D),
        out_shape=jax.ShapeDtypeStruct((M, D), F32),
        scratch_shapes=[pltpu.VMEM((TMP, D), BF16)],
        compiler_params=_cparams(("parallel",), 56),
        name="outproj",
    )(x, mods, o_a, o_f, o_b, pb, o_m, norm_gain.reshape(1, HG_DK), w_tiles)


def _final_kernel(x_ref, g_ref, o_ref):
    x = x_ref[...]
    o_ref[...] = x * lax.rsqrt(jnp.mean(x * x, axis=-1, keepdims=True) + EPS) * g_ref[...]


def _final_norm(x, gain):
    lat_tiles = SEQ // TQ
    return pl.pallas_call(
        _final_kernel,
        grid=(BATCH, lat_tiles),
        in_specs=[pl.BlockSpec((TQ, D), lambda b, i: (b * QT_PER_BATCH + CTX // TQ + i, 0)),
                  pl.BlockSpec((1, D), lambda b, i: (0, 0))],
        out_specs=pl.BlockSpec((None, TQ, D), lambda b, i: (b, i, 0)),
        out_shape=jax.ShapeDtypeStruct((BATCH, SEQ, D), F32),
        compiler_params=_cparams(("parallel", "parallel"), 32),
        name="final_norm",
    )(x, gain.reshape(1, D))


def _pad_w_uq(w_uq):
    w = w_uq.reshape(DEPTH, MLA_Q_RANK, MLA_HEADS, MLA_QK)
    w = jnp.pad(w, ((0, 0), (0, 0), (0, 0), (0, MLA_PAD - MLA_QK)))
    return w.reshape(DEPTH, MLA_Q_RANK, MLA_HEADS * MLA_PAD).astype(BF16)


def kernel(x, c, ctx, c_ctx, w_mod, b_mod, w_ffn1_in, w_ffn1_out, w_in, w_uq, w_ukv, w_out,
           gqa_q_gain, gqa_k_gain, mla_q_gain, mla_kv_gain, hgrn_lb_logits, hgrn_norm_gain,
           w_ffn2_in, w_ffn2_out, final_gain):
    xa = jnp.concatenate([ctx, x], axis=1).reshape(M, D)
    cvec8 = jnp.concatenate([c, c_ctx[None, :], jnp.zeros((SUBLANES - BATCH - 1, D), F32)], axis=0)
    mods = _modulation(cvec8, w_mod, b_mod)
    tabs_h = _rope_tables(HEAD_DIM)
    tabs_r = _rope_tables(MLA_ROPE)
    f1_in, f1_out = _ffn_weights(w_ffn1_in, w_ffn1_out)
    f2_in, f2_out = _ffn_weights(w_ffn2_in, w_ffn2_out)
    w_in_t = _proj_tiles(w_in[:, :, :IN_MAIN])
    w_out_t = _proj_tiles(w_out)
    w_kr = jnp.pad(w_in[:, :, OFF_KR:], ((0, 0), (0, 0), (0, LANES - MLA_ROPE))).astype(BF16)
    w_uq_pad = _pad_w_uq(w_uq)
    w_ukv_b = w_ukv.astype(BF16)

    for l in range(DEPTH):
        xa = _ffn(xa, mods, f1_in, f1_out, l, 0)
        pa, pz, pb, pkr = _inproj(xa, mods, w_in_t, w_kr, l)
        q_a, k_a = _gqa_prep(pa, gqa_q_gain[l], gqa_k_gain[l], tabs_h)
        o_a = _gqa_attn(q_a, k_a, pa)
        q_m, k_m, v_m = _mla_prep(pb, pkr, mla_q_gain[l], mla_kv_gain[l], w_uq_pad[l], w_ukv_b[l], tabs_r)
        o_m = _mla_attn(q_m, k_m, v_m)
        o_f = _hgrn(pa, pz, hgrn_lb_logits[0], l, False)
        o_b = _hgrn(pa, pz, hgrn_lb_logits[1], l, True)
        xa = _outproj(xa, mods, o_a, o_f, o_b, pb, o_m, hgrn_norm_gain[l], w_out_t, l)
        xa = _ffn(xa, mods, f2_in, f2_out, l, 6)
    return _final_norm(xa, final_gain)
```

```python
import functools

import jax
import jax.numpy as jnp
from jax import lax
from jax.experimental import pallas as pl
from jax.experimental.pallas import tpu as pltpu

F32 = jnp.float32
BF16 = jnp.bfloat16

D = 2048
BATCH = 4
SEQ = 2048
DEPTH = 2
GRID_W = 64
CTX = 256
EPS = 1e-6
ROPE_THETA = 10000.0

HEAD_DIM = 128
GQA_HEADS = 8
GQA_KV = 2
GQA_GROUP = GQA_HEADS // GQA_KV
HG_HEADS = 4
HG_DK = 128
HG_W = HG_HEADS * HG_DK
MLA_HEADS = 4
MLA_Q_RANK = 512
MLA_KV_RANK = 256
MLA_NOPE = 128
MLA_ROPE = 64
MLA_V = 128
MLA_QK = MLA_NOPE + MLA_ROPE
MLA_PAD = 256
FFN_H = 5504
N_MOD = 9

TB = CTX + SEQ
M = BATCH * TB
OFF_GQ, OFF_GK, OFF_GV = 0, 1024, 1280
OFF_HQ, OFF_HI, OFF_HF, OFF_HB, OFF_HG = 1536, 2048, 2560, 3072, 3584
OFF_CQ, OFF_CKV, OFF_KR = 4096, 4608, 4864
IN_MAIN = OFF_KR
PA_W, PZ_W, PB_W = OFF_HF, OFF_HG - OFF_HF, OFF_KR - OFF_HG

LANES = 128
SUBLANES = 8
MXU_W = 256

TM = TB // 2
TMP = TB // 4
ROWC = 64
TQ = 256
QT_PER_BATCH = TB // TQ
HCHUNK = 64
FFN_TH = 512
FFN_HP = -(-FFN_H // FFN_TH) * FFN_TH
FFN_STEPS = FFN_HP // FFN_TH
FFN_RC = 576
MOD_TN = 1024
PROJ_TN = MXU_W
PREP_ROWS = 128


def _cparams(sem, vmem_mb):
    return pltpu.CompilerParams(dimension_semantics=sem, vmem_limit_bytes=vmem_mb * 1024 * 1024)


def _sigmoid(x):
    return 1.0 / (1.0 + jnp.exp(-x))


def _mod_kernel(c_ref, w_ref, b_ref, o_ref):
    c = c_ref[...]
    a = (c * _sigmoid(c)).astype(BF16)
    o_ref[...] = jnp.dot(a, w_ref[...].astype(BF16), preferred_element_type=F32) + b_ref[...]


def _modulation(cvec8, w_mod, b_mod):
    n = N_MOD * D
    return pl.pallas_call(
        _mod_kernel,
        grid=(DEPTH, n // MOD_TN),
        in_specs=[
            pl.BlockSpec((SUBLANES, D), lambda l, j: (0, 0)),
            pl.BlockSpec((None, D, MOD_TN), lambda l, j: (l, 0, j)),
            pl.BlockSpec((None, 1, MOD_TN), lambda l, j: (l, 0, j)),
        ],
        out_specs=pl.BlockSpec((None, SUBLANES, MOD_TN), lambda l, j: (l, 0, j)),
        out_shape=jax.ShapeDtypeStruct((DEPTH, SUBLANES, n), F32),
        compiler_params=_cparams(("parallel", "parallel"), 40),
        name="modulation",
    )(cvec8, w_mod, b_mod.reshape(DEPTH, 1, n))


def _mod_rows(ref, batch):
    return ref[pl.ds(batch, 1), :], ref[4:5, :]


def _chunk_is_ctx(tile, tm, c):
    return ((tile % (TB // tm)) * tm + c * ROWC) < CTX


def _modulate_tile(x_ref, shift_ref, scale_ref, h_ref, tile, tm):
    batch = tile // (TB // tm)
    sh_b, sh_c = _mod_rows(shift_ref, batch)
    sc_b, sc_c = _mod_rows(scale_ref, batch)

    def body(c, carry):
        is_ctx = _chunk_is_ctx(tile, tm, c)
        sh = jnp.where(is_ctx, sh_c, sh_b)
        sc = 1.0 + jnp.where(is_ctx, sc_c, sc_b)
        r0 = pl.multiple_of(c * ROWC, ROWC)
        xc = x_ref[pl.ds(r0, ROWC), :]
        ms = jnp.mean(xc * xc, axis=-1, keepdims=True)
        h_ref[pl.ds(r0, ROWC), :] = (xc * lax.rsqrt(ms + EPS) * sc + sh).astype(BF16)
        return carry

    lax.fori_loop(0, tm // ROWC, body, 0)


def _ffn_kernel(x_ref, sh_ref, sc_ref, gt_ref, wi_ref, wo_ref, o_ref, h_ref):
    i = pl.program_id(0)
    j = pl.program_id(1)

    @pl.when(j == 0)
    def _():
        _modulate_tile(x_ref, sh_ref, sc_ref, h_ref, i, TM)
        o_ref[...] = jnp.zeros_like(o_ref)

    for rc in range(TM // FFN_RC):
        rows = slice(rc * FFN_RC, (rc + 1) * FFN_RC)
        r = jnp.dot(h_ref[rows, :], wi_ref[...], preferred_element_type=F32)
        g = r[:, :FFN_TH]
        u = r[:, FFN_TH:]
        a = (g * _sigmoid(g) * u).astype(BF16)
        o_ref[rows, :] += jnp.dot(a, wo_ref[...], preferred_element_type=F32)

    @pl.when(j == FFN_STEPS - 1)
    def _():
        g_b, g_c = _mod_rows(gt_ref, i // (TB // TM))

        def body(c, carry):
            gate = 0.5 * jnp.where(_chunk_is_ctx(i, TM, c), g_c, g_b)
            r0 = pl.multiple_of(c * ROWC, ROWC)
            o_ref[pl.ds(r0, ROWC), :] = x_ref[pl.ds(r0, ROWC), :] + gate * o_ref[pl.ds(r0, ROWC), :]
            return carry

        lax.fori_loop(0, TM // ROWC, body, 0)


def _ffn_win_tiles_kernel(w_ref, o_ref):
    for j in range(FFN_STEPS):
        valid = min(FFN_TH, FFN_H - j * FFN_TH)
        for part, base in ((0, 0), (1, FFN_H)):
            c0 = part * FFN_TH
            o_ref[j, :, c0:c0 + valid] = w_ref[:, base + j * FFN_TH:base + j * FFN_TH + valid].astype(BF16)
            if valid < FFN_TH:
                o_ref[j, :, c0 + valid:c0 + FFN_TH] = jnp.zeros((PREP_ROWS, FFN_TH - valid), BF16)


def _ffn_wout_tiles_kernel(w_ref, o_ref):
    real = pl.program_id(1) < FFN_H // PREP_ROWS
    o_ref[...] = jnp.where(real, w_ref[...], 0.0).astype(BF16)


def _ffn_weights(w_in, w_out):
    wi = pl.pallas_call(
        _ffn_win_tiles_kernel,
        grid=(DEPTH, D // PREP_ROWS),
        in_specs=[pl.BlockSpec((None, PREP_ROWS, 2 * FFN_H), lambda l, r: (l, r, 0))],
        out_specs=pl.BlockSpec((None, FFN_STEPS, PREP_ROWS, 2 * FFN_TH), lambda l, r: (l, 0, r, 0)),
        out_shape=jax.ShapeDtypeStruct((DEPTH, FFN_STEPS, D, 2 * FFN_TH), BF16),
        compiler_params=_cparams(("parallel", "parallel"), 32),
        name="ffn_win_tiles",
    )(w_in)
    last = FFN_H // PREP_ROWS - 1
    wo = pl.pallas_call(
        _ffn_wout_tiles_kernel,
        grid=(DEPTH, FFN_HP // PREP_ROWS),
        in_specs=[pl.BlockSpec((None, PREP_ROWS, D), lambda l, r: (l, jnp.minimum(r, last), 0))],
        out_specs=pl.BlockSpec((None, PREP_ROWS, D), lambda l, r: (l, r, 0)),
        out_shape=jax.ShapeDtypeStruct((DEPTH, FFN_HP, D), BF16),
        compiler_params=_cparams(("parallel", "parallel"), 32),
        name="ffn_wout_tiles",
    )(w_out)
    return wi, wo.reshape(DEPTH, FFN_STEPS, FFN_TH, D)


def _ffn(x, mods, wi, wo, layer, mod0):
    def mod_spec(k):
        return pl.BlockSpec((None, SUBLANES, D), lambda i, j: (layer, 0, mod0 + k))

    return pl.pallas_call(
        _ffn_kernel,
        grid=(M // TM, FFN_STEPS),
        in_specs=[
            pl.BlockSpec((TM, D), lambda i, j: (i, 0), pipeline_mode=pl.Buffered(1)),
            mod_spec(0), mod_spec(1), mod_spec(2),
            pl.BlockSpec((None, None, D, 2 * FFN_TH), lambda i, j: (layer, j, 0, 0)),
            pl.BlockSpec((None, None, FFN_TH, D), lambda i, j: (layer, j, 0, 0)),
        ],
        out_specs=pl.BlockSpec((TM, D), lambda i, j: (i, 0)),
        out_shape=jax.ShapeDtypeStruct((M, D), F32),
        scratch_shapes=[pltpu.VMEM((TM, D), BF16)],
        compiler_params=_cparams(("parallel", "arbitrary"), 58),
        name="ffn",
    )(x, mods, mods, mods, wi, wo)


def _proj_tiles_kernel(w_ref, o_ref):
    for t in range(o_ref.shape[0]):
        o_ref[t] = w_ref[:, t * PROJ_TN:(t + 1) * PROJ_TN].astype(BF16)


def _proj_tiles(w, n):
    k = w.shape[1]
    return pl.pallas_call(
        _proj_tiles_kernel,
        grid=(DEPTH, k // PREP_ROWS),
        in_specs=[pl.BlockSpec((None, PREP_ROWS, n), lambda l, r: (l, r, 0))],
        out_specs=pl.BlockSpec((None, n // PROJ_TN, PREP_ROWS, PROJ_TN), lambda l, r: (l, 0, r, 0)),
        out_shape=jax.ShapeDtypeStruct((DEPTH, n // PROJ_TN, k, PROJ_TN), BF16),
        compiler_params=_cparams(("parallel", "parallel"), 32),
        name="proj_tiles",
    )(w)


def _inproj_kernel(x_ref, sh_ref, sc_ref, w_ref, wkr_ref, pa_ref, pz_ref, pb_ref, kr_ref, h_ref):
    _modulate_tile(x_ref, sh_ref, sc_ref, h_ref, pl.program_id(0), TMP)
    h = h_ref[...]
    kr_ref[...] = jnp.dot(h, wkr_ref[...], preferred_element_type=F32)
    for t in range(IN_MAIN // PROJ_TN):
        y = jnp.dot(h, w_ref[t], preferred_element_type=F32)
        c0 = t * PROJ_TN
        if c0 < OFF_HF:
            pa_ref[:, c0:c0 + PROJ_TN] = y.astype(BF16)
        elif c0 < OFF_HG:
            pz_ref[:, c0 - OFF_HF:c0 - OFF_HF + PROJ_TN] = y
        else:
            pb_ref[:, c0 - OFF_HG:c0 - OFF_HG + PROJ_TN] = y.astype(BF16)


def _inproj(x, mods, w_tiles, w_kr, layer):
    def mod_spec(k):
        return pl.BlockSpec((None, SUBLANES, D), lambda i: (layer, 0, k))

    def out_spec(w):
        return pl.BlockSpec((TMP, w), lambda i: (i, 0))

    n_tiles = IN_MAIN // PROJ_TN
    return pl.pallas_call(
        _inproj_kernel,
        grid=(M // TMP,),
        in_specs=[
            pl.BlockSpec((TMP, D), lambda i: (i, 0)),
            mod_spec(3), mod_spec(4),
            pl.BlockSpec((None, n_tiles, D, PROJ_TN), lambda i: (layer, 0, 0, 0), pipeline_mode=pl.Buffered(1)),
            pl.BlockSpec((None, D, LANES), lambda i: (layer, 0, 0)),
        ],
        out_specs=[out_spec(PA_W), out_spec(PZ_W), out_spec(PB_W), out_spec(LANES)],
        out_shape=[jax.ShapeDtypeStruct((M, PA_W), BF16), jax.ShapeDtypeStruct((M, PZ_W), F32),
                   jax.ShapeDtypeStruct((M, PB_W), BF16), jax.ShapeDtypeStruct((M, LANES), F32)],
        scratch_shapes=[pltpu.VMEM((TMP, D), BF16)],
        compiler_params=_cparams(("parallel",), 56),
        name="inproj",
    )(x, mods, mods, w_tiles, w_kr)


def _rope_tables(rot_dim):
    rows = SEQ // GRID_W
    row = jnp.repeat(jnp.arange(rows, dtype=F32), GRID_W)
    colp = jnp.tile(jnp.arange(GRID_W, dtype=F32), rows)
    axis_dim = rot_dim // 2
    inv_freq = ROPE_THETA ** (-jnp.arange(0, axis_dim, 2, dtype=F32) / axis_dim)
    ang_r = row[:, None] * inv_freq
    ang_c = colp[:, None] * inv_freq
    ang = jnp.concatenate([ang_r, ang_r, ang_c, ang_c], axis=-1)
    cos, sin = jnp.cos(ang), jnp.sin(ang)
    quarter = rot_dim // 4
    lane = jnp.arange(rot_dim)
    first = (lane % (2 * quarter)) < quarter
    sin_up = jnp.where(first, -sin, 0.0)
    sin_dn = jnp.where(first, 0.0, sin)
    pad = LANES - rot_dim
    if pad:
        cos = jnp.pad(cos, ((0, 0), (0, pad)), constant_values=1.0)
        sin_up = jnp.pad(sin_up, ((0, 0), (0, pad)))
        sin_dn = jnp.pad(sin_dn, ((0, 0), (0, pad)))
    return cos, sin_up, sin_dn


def _rope(x, cos, sin_up, sin_dn, quarter):
    up = pltpu.roll(x, LANES - quarter, 1)
    dn = pltpu.roll(x, quarter, 1)
    return x * cos + up * sin_up + dn * sin_dn


def _head_norm(x, gain):
    return x * lax.rsqrt(jnp.mean(x * x, axis=-1, keepdims=True) + EPS) * gain


def _gqa_prep_kernel(p_ref, qg_ref, kg_ref, cos_ref, su_ref, sd_ref, q_ref, k_ref):
    t = pl.program_id(0)
    is_ctx = (t % QT_PER_BATCH) == 0
    cos, su, sd = cos_ref[...], su_ref[...], sd_ref[...]
    scale = HEAD_DIM ** -0.5
    for hh in range(GQA_HEADS + GQA_KV):
        xh = p_ref[:, hh * HEAD_DIM:(hh + 1) * HEAD_DIM].astype(F32)
        is_q = hh < GQA_HEADS
        n = _head_norm(xh, qg_ref[...] if is_q else kg_ref[...])
        y = jnp.where(is_ctx, n, _rope(n, cos, su, sd, HEAD_DIM // 4))
        if is_q:
            q_ref[:, hh * HEAD_DIM:(hh + 1) * HEAD_DIM] = (y * scale).astype(BF16)
        else:
            kk = hh - GQA_HEADS
            k_ref[:, kk * HEAD_DIM:(kk + 1) * HEAD_DIM] = y.astype(BF16)


def _rope_idx(t):
    return jnp.maximum(t % QT_PER_BATCH - 1, 0)


def _gqa_prep(pa, q_gain, k_gain, tabs):
    width = (GQA_HEADS + GQA_KV) * HEAD_DIM
    tab_spec = pl.BlockSpec((TQ, LANES), lambda t: (_rope_idx(t), 0))
    vec_spec = pl.BlockSpec((1, HEAD_DIM), lambda t: (0, 0))
    return pl.pallas_call(
        _gqa_prep_kernel,
        grid=(M // TQ,),
        in_specs=[pl.BlockSpec((TQ, width), lambda t: (t, 0)), vec_spec, vec_spec,
                  tab_spec, tab_spec, tab_spec],
        out_specs=[pl.BlockSpec((TQ, GQA_HEADS * HEAD_DIM), lambda t: (t, 0)),
                   pl.BlockSpec((TQ, GQA_KV * HEAD_DIM), lambda t: (t, 0))],
        out_shape=[jax.ShapeDtypeStruct((M, GQA_HEADS * HEAD_DIM), BF16),
                   jax.ShapeDtypeStruct((M, GQA_KV * HEAD_DIM), BF16)],
        compiler_params=_cparams(("parallel",), 32),
        name="gqa_prep",
    )(pa, q_gain.reshape(1, HEAD_DIM), k_gain.reshape(1, HEAD_DIM), *tabs)


def _softmax_pv(s, v):
    m = jnp.max(s, axis=-1, keepdims=True)
    e = jnp.exp(s - m)
    l = jnp.sum(e, axis=-1, keepdims=True)
    return jnp.dot(e.astype(BF16), v, preferred_element_type=F32) / l


def _nt_dot(a, b):
    return lax.dot_general(a, b, (((1,), (1,)), ((), ())), preferred_element_type=F32)


def _gqa_attn_kernel(q_ref, k_ref, v_ref, o_ref):
    def attend(n_keys):
        for hh in range(GQA_HEADS):
            kv = slice((hh // GQA_GROUP) * HEAD_DIM, (hh // GQA_GROUP + 1) * HEAD_DIM)
            hs = slice(hh * HEAD_DIM, (hh + 1) * HEAD_DIM)
            s = _nt_dot(q_ref[:, hs], k_ref[0:n_keys, kv])
            o_ref[:, hs] = _softmax_pv(s, v_ref[0:n_keys, kv]).astype(BF16)

    @pl.when(pl.program_id(1) == 0)
    def _():
        attend(CTX)

    @pl.when(pl.program_id(1) > 0)
    def _():
        attend(TB)


def _gqa_attn(q, k, pa):
    qw, kw = GQA_HEADS * HEAD_DIM, GQA_KV * HEAD_DIM
    return pl.pallas_call(
        _gqa_attn_kernel,
        grid=(BATCH, QT_PER_BATCH),
        in_specs=[
            pl.BlockSpec((TQ, qw), lambda b, i: (b * QT_PER_BATCH + i, 0)),
            pl.BlockSpec((TB, kw), lambda b, i: (b, 0)),
            pl.BlockSpec((TB, kw), lambda b, i: (b, OFF_GV // kw)),
        ],
        out_specs=pl.BlockSpec((TQ, qw), lambda b, i: (b * QT_PER_BATCH + i, 0)),
        out_shape=jax.ShapeDtypeStruct((M, qw), BF16),
        compiler_params=_cparams(("parallel", "arbitrary"), 48),
        name="gqa_attn",
    )(q, k, pa)


def _mla_prep_kernel(cq_ref, ckv_ref, kr_ref, qg_ref, kvg_ref, wuq_ref, wukv_ref,
                     cos_ref, su_ref, sd_ref, q_ref, k_ref, v_ref):
    t = pl.program_id(0)
    is_ctx = (t % QT_PER_BATCH) == 0
    cos, su, sd = cos_ref[...], su_ref[...], sd_ref[...]
    quarter = MLA_ROPE // 4
    scale = MLA_QK ** -0.5

    cq = _head_norm(cq_ref[...].astype(F32), qg_ref[...]).astype(BF16)
    qf = jnp.dot(cq, wuq_ref[...], preferred_element_type=F32)
    ckv = _head_norm(ckv_ref[...].astype(F32), kvg_ref[...]).astype(BF16)
    kvf = jnp.dot(ckv, wukv_ref[...], preferred_element_type=F32)
    kr = kr_ref[...]
    kr = jnp.where(is_ctx, kr, _rope(kr, cos, su, sd, quarter)).astype(BF16)
    for hh in range(MLA_HEADS):
        base = hh * MLA_PAD
        q_ref[:, base:base + MLA_NOPE] = (qf[:, base:base + MLA_NOPE] * scale).astype(BF16)
        qr = qf[:, base + MLA_NOPE:base + MLA_PAD]
        qr = jnp.where(is_ctx, qr, _rope(qr, cos, su, sd, quarter))
        q_ref[:, base + MLA_NOPE:base + MLA_PAD] = (qr * scale).astype(BF16)
        k_ref[:, base:base + MLA_NOPE] = kvf[:, base:base + MLA_NOPE].astype(BF16)
        k_ref[:, base + MLA_NOPE:base + MLA_PAD] = kr
        v_ref[:, hh * MLA_V:(hh + 1) * MLA_V] = kvf[:, base + MLA_NOPE:base + MLA_PAD].astype(BF16)


def _mla_prep(pb, pkr, q_gain, kv_gain, w_uq_pad, w_ukv, tabs):
    tab_spec = pl.BlockSpec((TQ, LANES), lambda t: (_rope_idx(t), 0))
    qk_w = MLA_HEADS * MLA_PAD
    return pl.pallas_call(
        _mla_prep_kernel,
        grid=(M // TQ,),
        in_specs=[
            pl.BlockSpec((TQ, MLA_Q_RANK), lambda t: (t, (OFF_CQ - OFF_HG) // MLA_Q_RANK)),
            pl.BlockSpec((TQ, MLA_KV_RANK), lambda t: (t, (OFF_CKV - OFF_HG) // MLA_KV_RANK)),
            pl.BlockSpec((TQ, LANES), lambda t: (t, 0)),
            pl.BlockSpec((1, MLA_Q_RANK), lambda t: (0, 0)),
            pl.BlockSpec((1, MLA_KV_RANK), lambda t: (0, 0)),
            pl.BlockSpec((MLA_Q_RANK, qk_w), lambda t: (0, 0)),
            pl.BlockSpec((MLA_KV_RANK, qk_w), lambda t: (0, 0)),
            tab_spec, tab_spec, tab_spec,
        ],
        out_specs=[pl.BlockSpec((TQ, qk_w), lambda t: (t, 0)),
                   pl.BlockSpec((TQ, qk_w), lambda t: (t, 0)),
                   pl.BlockSpec((TQ, MLA_HEADS * MLA_V), lambda t: (t, 0))],
        out_shape=[jax.ShapeDtypeStruct((M, qk_w), BF16),
                   jax.ShapeDtypeStruct((M, qk_w), BF16),
                   jax.ShapeDtypeStruct((M, MLA_HEADS * MLA_V), BF16)],
        compiler_params=_cparams(("parallel",), 32),
        name="mla_prep",
    )(pb, pb, pkr, q_gain.reshape(1, -1), kv_gain.reshape(1, -1), w_uq_pad, w_ukv, *tabs)


def _mla_attn_kernel(q_ref, k_ref, v_ref, o_ref):
    def attend(n_keys):
        for hh in range(MLA_HEADS):
            qs = slice(hh * MLA_PAD, (hh + 1) * MLA_PAD)
            vs = slice(hh * MLA_V, (hh + 1) * MLA_V)
            s = _nt_dot(q_ref[:, qs], k_ref[0:n_keys, qs])
            o_ref[:, vs] = _softmax_pv(s, v_ref[0:n_keys, vs]).astype(BF16)

    @pl.when(pl.program_id(1) == 0)
    def _():
        attend(CTX)

    @pl.when(pl.program_id(1) > 0)
    def _():
        attend(TB)


def _mla_attn(q, k, v):
    qw, vw = MLA_HEADS * MLA_PAD, MLA_HEADS * MLA_V
    return pl.pallas_call(
        _mla_attn_kernel,
        grid=(BATCH, QT_PER_BATCH),
        in_specs=[
            pl.BlockSpec((TQ, qw), lambda b, i: (b * QT_PER_BATCH + i, 0)),
            pl.BlockSpec((TB, qw), lambda b, i: (b, 0)),
            pl.BlockSpec((TB, vw), lambda b, i: (b, 0)),
        ],
        out_specs=pl.BlockSpec((TQ, vw), lambda b, i: (b * QT_PER_BATCH + i, 0)),
        out_shape=jax.ShapeDtypeStruct((M, vw), BF16),
        compiler_params=_cparams(("parallel", "arbitrary"), 48),
        name="mla_attn",
    )(q, k, v)


N_CHUNK = TB // HCHUNK
CTX_CHUNKS = CTX // HCHUNK
DIAG = 8
LEVELS = (8, 16, 32)
LOG2E = 1.4426950408889634


def _split3(x):
    hi = x.astype(BF16)
    r1 = x - hi.astype(F32)
    mid = r1.astype(BF16)
    lo = (r1 - mid.astype(F32)).astype(BF16)
    return hi, mid, lo


def _hgrn_kernel(q_ref, v_ref, z_ref, lg_ref, o_ref, *scratch, reverse, layer):
    cum_refs = scratch[0:HG_HEADS]
    k_refs = scratch[HG_HEADS:2 * HG_HEADS]
    s_refs = scratch[2 * HG_HEADS:]

    lg = [lg_ref[l:l + 1, :] for l in range(DEPTH)]
    mx = functools.reduce(jnp.maximum, lg)
    ex = [jnp.exp(r - mx) for r in lg]
    den = functools.reduce(lambda a_, b_: a_ + b_, ex)
    lb_all = jnp.zeros((1, HG_W), F32)
    for l in range(1, layer + 1):
        lb_all = lb_all + ex[l] / den
    log_lb_all = jnp.log(lb_all)
    log_1m_all = jnp.log1p(-lb_all)

    ti = lax.broadcasted_iota(jnp.int32, (HCHUNK, HCHUNK), 0)
    si = lax.broadcasted_iota(jnp.int32, (HCHUNK, HCHUNK), 1)
    causal = (si >= ti) if reverse else (si <= ti)
    tri = causal.astype(BF16)
    diag_mask = (causal & ((ti // DIAG) == (si // DIAG))).astype(F32)
    ri = lax.broadcasted_iota(jnp.int32, (DIAG * HG_DK, LANES), 0)
    ci = lax.broadcasted_iota(jnp.int32, (DIAG * HG_DK, LANES), 1)
    sel = ((ri // HG_DK) == (ci % DIAG)).astype(BF16)
    edge = 0 if reverse else HCHUNK - 1

    for s_ref in s_refs:
        s_ref[...] = jnp.zeros_like(s_ref)

    heads = range(HG_HEADS)

    def lanes(hh):
        return slice(hh * HG_DK, (hh + 1) * HG_DK)

    def chunk_all_heads(r0):
        rows = pl.ds(r0, HCHUNK)
        q = [q_ref[rows, lanes(hh)].astype(F32) * (HG_DK ** -0.5) for hh in heads]
        v = [v_ref[rows, lanes(hh)] for hh in heads]

        z = z_ref[rows, :]
        u = jnp.exp(-jnp.abs(z))
        w = 1.0 + u
        b_ = log_1m_all + (jnp.minimum(z, 0.0) - jnp.log(w))
        log_f = jnp.maximum(log_lb_all, b_) + jnp.log(1.0 + jnp.exp(-jnp.abs(log_lb_all - b_)))
        k_all = (1.0 - lb_all) * (jnp.where(z >= 0.0, u, 1.0) / w)
        k = [k_all[:, lanes(hh)] for hh in heads]

        cum3 = jnp.dot(tri, jnp.concatenate(_split3(log_f), axis=1), preferred_element_type=F32)
        cum_all = (cum3[:, :HG_W] + cum3[:, HG_W:2 * HG_W] + cum3[:, 2 * HG_W:]) * LOG2E
        cum = [cum_all[:, lanes(hh)] for hh in heads]
        for hh in heads:
            cum_refs[hh][...] = cum[hh]
            k_refs[hh][...] = k[hh]
        tot = [cum_refs[hh][edge:edge + 1, :] for hh in heads]

        pair_lhs = []
        for hh in heads:
            cols = []
            for s in range(DIAG):
                cs = jnp.concatenate(
                    [jnp.broadcast_to(cum_refs[hh][blk * DIAG + s:blk * DIAG + s + 1, :], (DIAG, HG_DK))
                     for blk in range(HCHUNK // DIAG)], axis=0)
                ks = jnp.concatenate(
                    [jnp.broadcast_to(k_refs[hh][blk * DIAG + s:blk * DIAG + s + 1, :], (DIAG, HG_DK))
                     for blk in range(HCHUNK // DIAG)], axis=0)
                cols.append((q[hh] * jnp.exp2(jnp.minimum(cum[hh] - cs, 0.0)) * ks).astype(BF16))
            pair_lhs.append(jnp.concatenate(cols, axis=1))

        span_lhs, span_rhs = [], []
        for hh in heads:
            lhs, rhs = [], []
            for hs in LEVELS:
                for blk in range(HCHUNK // (2 * hs)):
                    lo, mid, hi = blk * 2 * hs, blk * 2 * hs + hs, (blk + 1) * 2 * hs
                    k_rows, q_rows = ((mid, hi), (lo, mid)) if reverse else ((lo, mid), (mid, hi))
                    ref_row = mid if reverse else mid - 1
                    rr = cum_refs[hh][ref_row:ref_row + 1, :]
                    qs = q[hh][q_rows[0]:q_rows[1], :] * jnp.exp2(cum[hh][q_rows[0]:q_rows[1], :] - rr)
                    ks = k[hh][k_rows[0]:k_rows[1], :] * jnp.exp2(rr - cum[hh][k_rows[0]:k_rows[1], :])
                    zq = [jnp.zeros((q_rows[0], HG_DK), F32), qs, jnp.zeros((HCHUNK - q_rows[1], HG_DK), F32)]
                    zk = [jnp.zeros((k_rows[0], HG_DK), F32), ks, jnp.zeros((HCHUNK - k_rows[1], HG_DK), F32)]
                    lhs.append(jnp.concatenate([p_ for p_ in zq if p_.shape[0]], axis=0))
                    rhs.append(jnp.concatenate([p_ for p_ in zk if p_.shape[0]], axis=0))
            span_lhs.append(jnp.concatenate(lhs, axis=1).astype(BF16))
            span_rhs.append(jnp.concatenate(rhs, axis=1).astype(BF16))

        state = [s_refs[hh][...] for hh in heads]
        qe = [(q[hh] * jnp.exp2(cum[hh])).astype(BF16) for hh in heads]
        ke = [(k[hh] * jnp.exp2(tot[hh] - cum[hh])).astype(BF16) for hh in heads]

        pair = [jnp.dot(pair_lhs[hh], sel, preferred_element_type=F32) for hh in heads]
        a_off = [_nt_dot(span_lhs[hh], span_rhs[hh]) for hh in heads]
        inter = [_nt_dot(qe[hh], state[hh].astype(BF16)) for hh in heads]
        upd = [lax.dot_general(v[hh], ke[hh], (((0,), (0,)), ((), ())), preferred_element_type=F32)
               for hh in heads]
        a = [(a_off[hh] + pair[hh][:, :HCHUNK] * diag_mask).astype(BF16) for hh in heads]
        o = [jnp.dot(a[hh], v[hh], preferred_element_type=F32) for hh in heads]
        for hh in heads:
            o_ref[rows, lanes(hh)] = o[hh] + inter[hh]
            s_refs[hh][...] = state[hh] * jnp.exp2(tot[hh]) + upd[hh]

    def chunk(jstep, carry):
        if reverse:
            c = jnp.where(jstep < CTX_CHUNKS, CTX_CHUNKS - 1 - jstep, N_CHUNK + CTX_CHUNKS - 1 - jstep)
        else:
            c = jstep
        chunk_all_heads(pl.multiple_of(c * HCHUNK, HCHUNK))
        return carry

    lax.fori_loop(0, N_CHUNK, chunk, 0, unroll=2)


def _hgrn(pa, pz, lb_logits_dir, layer, reverse):
    return pl.pallas_call(
        functools.partial(_hgrn_kernel, reverse=reverse, layer=layer),
        grid=(BATCH,),
        in_specs=[
            pl.BlockSpec((TB, HG_W), lambda b: (b, OFF_HQ // HG_W)),
            pl.BlockSpec((TB, HG_W), lambda b: (b, OFF_HI // HG_W)),
            pl.BlockSpec((TB, HG_W), lambda b: (b, 1 if reverse else 0)),
            pl.BlockSpec((DEPTH, HG_W), lambda b: (0, 0)),
        ],
        out_specs=pl.BlockSpec((TB, HG_W), lambda b: (b, 0)),
        out_shape=jax.ShapeDtypeStruct((M, HG_W), F32),
        scratch_shapes=([pltpu.VMEM((HCHUNK, HG_DK), F32)] * (2 * HG_HEADS)
                        + [pltpu.VMEM((HG_DK, HG_DK), F32)] * HG_HEADS),
        compiler_params=_cparams(("parallel",), 48),
        name="hgrn_bwd" if reverse else "hgrn_fwd",
    )(pa, pa, pz, lb_logits_dir)


def _outproj_kernel(x_ref, gt_ref, oa_ref, of_ref, ob_ref, hg_ref, om_ref, ng_ref, w_ref, o_ref, lhs_ref):
    i = pl.program_id(0)
    a_w = GQA_HEADS * HEAD_DIM
    lhs_ref[:, 0:a_w] = oa_ref[...]
    lhs_ref[:, a_w + HG_W:] = om_ref[...]
    gain = ng_ref[...]
    for hh in range(HG_HEADS):
        sl = slice(hh * HG_DK, (hh + 1) * HG_DK)
        g = hg_ref[:, sl].astype(F32)
        y = _head_norm(of_ref[:, sl] + ob_ref[:, sl], gain) * (g * _sigmoid(g))
        lhs_ref[:, a_w + hh * HG_DK:a_w + (hh + 1) * HG_DK] = y.astype(BF16)

    lhs = lhs_ref[...]
    g_b, g_c = _mod_rows(gt_ref, i // (TB // TMP))
    for n in range(D // PROJ_TN):
        cols = slice(n * PROJ_TN, (n + 1) * PROJ_TN)
        y = jnp.dot(lhs, w_ref[n], preferred_element_type=F32)
        for c in range(TMP // ROWC):
            gate = jnp.where(_chunk_is_ctx(i, TMP, c), g_c[:, cols], g_b[:, cols])
            rows = slice(c * ROWC, (c + 1) * ROWC)
            o_ref[rows, cols] = x_ref[rows, cols] + gate * y[rows, :]


def _outproj(x, mods, o_a, o_f, o_b, pb, o_m, norm_gain, w_tiles, layer):
    def row_spec(w, col=0):
        return pl.BlockSpec((TMP, w), lambda i: (i, col))

    return pl.pallas_call(
        _outproj_kernel,
        grid=(M // TMP,),
        in_specs=[
            row_spec(D),
            pl.BlockSpec((None, SUBLANES, D), lambda i: (layer, 0, 5)),
            row_spec(GQA_HEADS * HEAD_DIM), row_spec(HG_W), row_spec(HG_W),
            row_spec(HG_W, 0),
            row_spec(MLA_HEADS * MLA_V),
            pl.BlockSpec((1, HG_DK), lambda i: (0, 0)),
            pl.BlockSpec((None, D // PROJ_TN, D, PROJ_TN), lambda i: (layer, 0, 0, 0),
                         pipeline_mode=pl.Buffered(1)),
        ],
        out_specs=row_spec(D),
        out_shape=jax.ShapeDtypeStruct((M, D), F32),
        scratch_shapes=[pltpu.VMEM((TMP, D), BF16)],
        compiler_params=_cparams(("parallel",), 56),
        name="outproj",
    )(x, mods, o_a, o_f, o_b, pb, o_m, norm_gain.reshape(1, HG_DK), w_tiles)


def _final_kernel(x_ref, g_ref, o_ref):
    x = x_ref[...]
    o_ref[...] = x * lax.rsqrt(jnp.mean(x * x, axis=-1, keepdims=True) + EPS) * g_ref[...]


def _final_norm(x, gain):
    lat_tiles = SEQ // TQ
    return pl.pallas_call(
        _final_kernel,
        grid=(BATCH, lat_tiles),
        in_specs=[pl.BlockSpec((TQ, D), lambda b, i: (b * QT_PER_BATCH + CTX // TQ + i, 0)),
                  pl.BlockSpec((1, D), lambda b, i: (0, 0))],
        out_specs=pl.BlockSpec((None, TQ, D), lambda b, i: (b, i, 0)),
        out_shape=jax.ShapeDtypeStruct((BATCH, SEQ, D), F32),
        compiler_params=_cparams(("parallel", "parallel"), 32),
        name="final_norm",
    )(x, gain.reshape(1, D))


def _pad_w_uq(w_uq):
    w = w_uq.reshape(DEPTH, MLA_Q_RANK, MLA_HEADS, MLA_QK)
    w = jnp.pad(w, ((0, 0), (0, 0), (0, 0), (0, MLA_PAD - MLA_QK)))
    return w.reshape(DEPTH, MLA_Q_RANK, MLA_HEADS * MLA_PAD).astype(BF16)


def kernel(x, c, ctx, c_ctx, w_mod, b_mod, w_ffn1_in, w_ffn1_out, w_in, w_uq, w_ukv, w_out,
           gqa_q_gain, gqa_k_gain, mla_q_gain, mla_kv_gain, hgrn_lb_logits, hgrn_norm_gain,
           w_ffn2_in, w_ffn2_out, final_gain):
    xa = jnp.concatenate([ctx, x], axis=1).reshape(M, D)
    cvec8 = jnp.concatenate([c, c_ctx[None, :], jnp.zeros((SUBLANES - BATCH - 1, D), F32)], axis=0)
    mods = _modulation(cvec8, w_mod, b_mod)
    tabs_h = _rope_tables(HEAD_DIM)
    tabs_r = _rope_tables(MLA_ROPE)
    f1_in, f1_out = _ffn_weights(w_ffn1_in, w_ffn1_out)
    f2_in, f2_out = _ffn_weights(w_ffn2_in, w_ffn2_out)
    w_in_t = _proj_tiles(w_in, IN_MAIN)
    w_out_t = _proj_tiles(w_out, D)
    w_kr = jnp.pad(w_in[:, :, OFF_KR:], ((0, 0), (0, 0), (0, LANES - MLA_ROPE))).astype(BF16)
    w_uq_pad = _pad_w_uq(w_uq)
    w_ukv_b = w_ukv.astype(BF16)

    for l in range(DEPTH):
        xa = _ffn(xa, mods, f1_in, f1_out, l, 0)
        pa, pz, pb, pkr = _inproj(xa, mods, w_in_t, w_kr, l)
        q_a, k_a = _gqa_prep(pa, gqa_q_gain[l], gqa_k_gain[l], tabs_h)
        o_a = _gqa_attn(q_a, k_a, pa)
        q_m, k_m, v_m = _mla_prep(pb, pkr, mla_q_gain[l], mla_kv_gain[l], w_uq_pad[l], w_ukv_b[l], tabs_r)
        o_m = _mla_attn(q_m, k_m, v_m)
        o_f = _hgrn(pa, pz, hgrn_lb_logits[0], l, False)
        o_b = _hgrn(pa, pz, hgrn_lb_logits[1], l, True)
        xa = _outproj(xa, mods, o_a, o_f, o_b, pb, o_m, hgrn_norm_gain[l], w_out_t, l)
        xa = _ffn(xa, mods, f2_in, f2_out, l, 6)
    return _final_norm(xa, final_gain)
```

```python
import functools

import jax
import jax.numpy as jnp
from jax import lax
from jax.experimental import pallas as pl
from jax.experimental.pallas import tpu as pltpu

F32 = jnp.float32
BF16 = jnp.bfloat16

D = 2048
BATCH = 4
SEQ = 2048
DEPTH = 2
GRID_W = 64
CTX = 256
EPS = 1e-6
ROPE_THETA = 10000.0

HEAD_DIM = 128
GQA_HEADS = 8
GQA_KV = 2
GQA_GROUP = GQA_HEADS // GQA_KV
HG_HEADS = 4
HG_DK = 128
HG_W = HG_HEADS * HG_DK
MLA_HEADS = 4
MLA_Q_RANK = 512
MLA_KV_RANK = 256
MLA_NOPE = 128
MLA_ROPE = 64
MLA_V = 128
MLA_QK = MLA_NOPE + MLA_ROPE
MLA_PAD = 256
FFN_H = 5504
N_MOD = 9

LAT = BATCH * SEQ
CTXR = BATCH * CTX
M = LAT + CTXR
OFF_GQ, OFF_GK, OFF_GV = 0, 1024, 1280
OFF_HQ, OFF_HI, OFF_HF, OFF_HB, OFF_HG = 1536, 2048, 2560, 3072, 3584
OFF_CQ, OFF_CKV, OFF_KR = 4096, 4608, 4864
IN_MAIN = OFF_KR
PA_W, PZ_W, PB_W = OFF_HF, OFF_HG - OFF_HF, OFF_KR - OFF_HG

LANES = 128
SUBLANES = 8
MXU_W = 256

TM = 1024
TMP = 512
ROWC = 64
TQ = 256
QT_SEQ = SEQ // TQ
HCHUNK = 64
FFN_TH = 512
FFN_HP = -(-FFN_H // FFN_TH) * FFN_TH
FFN_STEPS = FFN_HP // FFN_TH
FFN_RC = 512
MOD_TN = 1024
PROJ_TN = MXU_W
PREP_ROWS = 128


def _cparams(sem, vmem_mb):
    return pltpu.CompilerParams(dimension_semantics=sem, vmem_limit_bytes=vmem_mb * 1024 * 1024)


def _sigmoid(x):
    return 1.0 / (1.0 + jnp.exp(-x))


def _mod_kernel(c_ref, w_ref, b_ref, o_ref):
    c = c_ref[...]
    a = (c * _sigmoid(c)).astype(BF16)
    o_ref[...] = jnp.dot(a, w_ref[...].astype(BF16), preferred_element_type=F32) + b_ref[...]


def _modulation(cvec8, w_mod, b_mod):
    n = N_MOD * D
    return pl.pallas_call(
        _mod_kernel,
        grid=(DEPTH, n // MOD_TN),
        in_specs=[
            pl.BlockSpec((SUBLANES, D), lambda l, j: (0, 0)),
            pl.BlockSpec((None, D, MOD_TN), lambda l, j: (l, 0, j)),
            pl.BlockSpec((None, 1, MOD_TN), lambda l, j: (l, 0, j)),
        ],
        out_specs=pl.BlockSpec((None, SUBLANES, MOD_TN), lambda l, j: (l, 0, j)),
        out_shape=jax.ShapeDtypeStruct((DEPTH, SUBLANES, n), F32),
        compiler_params=_cparams(("parallel", "parallel"), 40),
        name="modulation",
    )(cvec8, w_mod, b_mod.reshape(DEPTH, 1, n))


def _mod_vec(ref, tile, tm):
    row = jnp.where(tile < LAT // tm, tile // (SEQ // tm), BATCH)
    return ref[pl.ds(row, 1), :]


def _modulate_tile(x_ref, shift_ref, scale_ref, h_ref, tile, tm):
    sh = _mod_vec(shift_ref, tile, tm)
    sc = 1.0 + _mod_vec(scale_ref, tile, tm)

    def body(c, carry):
        r0 = pl.multiple_of(c * ROWC, ROWC)
        xc = x_ref[pl.ds(r0, ROWC), :]
        ms = jnp.mean(xc * xc, axis=-1, keepdims=True)
        h_ref[pl.ds(r0, ROWC), :] = (xc * lax.rsqrt(ms + EPS) * sc + sh).astype(BF16)
        return carry

    lax.fori_loop(0, tm // ROWC, body, 0)


def _ffn_kernel(*refs, two_inputs, final):
    refs = list(refs)
    x_ref = refs.pop(0)
    c_ref = refs.pop(0) if two_inputs else None
    sh_ref, sc_ref, gt_ref = refs.pop(0), refs.pop(0), refs.pop(0)
    fg_ref = refs.pop(0) if final else None
    wi_ref, wo_ref, o_ref, h_ref = refs
    i = pl.program_id(0)
    j = pl.program_id(1)

    def per_source(fn):
        if two_inputs:
            pl.when(i < LAT // TM)(lambda: fn(x_ref))
            pl.when(i >= LAT // TM)(lambda: fn(c_ref))
        else:
            fn(x_ref)

    @pl.when(j == 0)
    def _():
        per_source(lambda src: _modulate_tile(src, sh_ref, sc_ref, h_ref, i, TM))
        o_ref[...] = jnp.zeros_like(o_ref)

    for rc in range(TM // FFN_RC):
        rows = slice(rc * FFN_RC, (rc + 1) * FFN_RC)
        r = jnp.dot(h_ref[rows, :], wi_ref[...], preferred_element_type=F32)
        g = r[:, :FFN_TH]
        u = r[:, FFN_TH:]
        a = (g * _sigmoid(g) * u).astype(BF16)
        o_ref[rows, :] += jnp.dot(a, wo_ref[...], preferred_element_type=F32)

    @pl.when(j == FFN_STEPS - 1)
    def _():
        gate = 0.5 * _mod_vec(gt_ref, i, TM)

        def epilogue(src):
            def body(c, carry):
                r0 = pl.multiple_of(c * ROWC, ROWC)
                y = src[pl.ds(r0, ROWC), :] + gate * o_ref[pl.ds(r0, ROWC), :]
                if final:
                    y = y * lax.rsqrt(jnp.mean(y * y, axis=-1, keepdims=True) + EPS) * fg_ref[...]
                o_ref[pl.ds(r0, ROWC), :] = y
                return carry

            lax.fori_loop(0, TM // ROWC, body, 0)

        per_source(epilogue)


def _ffn(x, mods, wi, wo, layer, mod0, ctx=None, final_gain=None):
    two_inputs, final = ctx is not None, final_gain is not None
    rows = M if two_inputs else x.shape[0]
    lat_tiles = LAT // TM

    def mod_spec(k):
        return pl.BlockSpec((None, SUBLANES, D), lambda i, j: (layer, 0, mod0 + k))

    in_specs = [pl.BlockSpec((TM, D), lambda i, j: (jnp.minimum(i, lat_tiles - 1) if two_inputs else i, 0),
                             pipeline_mode=pl.Buffered(1))]
    args = [x]
    if two_inputs:
        in_specs.append(pl.BlockSpec((TM, D), lambda i, j: (0, 0), pipeline_mode=pl.Buffered(1)))
        args.append(ctx)
    in_specs += [mod_spec(0), mod_spec(1), mod_spec(2)]
    args += [mods, mods, mods]
    if final:
        in_specs.append(pl.BlockSpec((1, D), lambda i, j: (0, 0)))
        args.append(final_gain.reshape(1, D))
        out_spec = pl.BlockSpec((None, TM, D), lambda i, j: (i // (SEQ // TM), i % (SEQ // TM), 0))
        out_shape = jax.ShapeDtypeStruct((BATCH, SEQ, D), F32)
    else:
        out_spec = pl.BlockSpec((TM, D), lambda i, j: (i, 0))
        out_shape = jax.ShapeDtypeStruct((rows, D), F32)
    in_specs += [pl.BlockSpec((None, None, D, 2 * FFN_TH), lambda i, j: (layer, j, 0, 0)),
                 pl.BlockSpec((None, None, FFN_TH, D), lambda i, j: (layer, j, 0, 0))]
    args += [wi, wo]
    return pl.pallas_call(
        functools.partial(_ffn_kernel, two_inputs=two_inputs, final=final),
        grid=(rows // TM, FFN_STEPS),
        in_specs=in_specs,
        out_specs=out_spec,
        out_shape=out_shape,
        scratch_shapes=[pltpu.VMEM((TM, D), BF16)],
        compiler_params=_cparams(("parallel", "arbitrary"), 58),
        name="ffn",
    )(*args)


def _ffn_win_tiles_kernel(w_ref, o_ref):
    for j in range(FFN_STEPS):
        valid = min(FFN_TH, FFN_H - j * FFN_TH)
        for part, base in ((0, 0), (1, FFN_H)):
            c0 = part * FFN_TH
            o_ref[j, :, c0:c0 + valid] = w_ref[:, base + j * FFN_TH:base + j * FFN_TH + valid].astype(BF16)
            if valid < FFN_TH:
                o_ref[j, :, c0 + valid:c0 + FFN_TH] = jnp.zeros((PREP_ROWS, FFN_TH - valid), BF16)


def _ffn_wout_tiles_kernel(w_ref, *rest):
    tails, o_ref = rest[:-1], rest[-1]
    last = pl.program_id(1) == FFN_STEPS - 1

    @pl.when(jnp.logical_not(last))
    def _():
        o_ref[...] = w_ref[...].astype(BF16)

    @pl.when(last)
    def _():
        for r, t_ref in enumerate(tails):
            o_ref[r * PREP_ROWS:(r + 1) * PREP_ROWS, :] = t_ref[...].astype(BF16)
        n = len(tails) * PREP_ROWS
        o_ref[n:, :] = jnp.zeros((FFN_TH - n, D), BF16)


def _ffn_weights(w_in, w_out):
    wi = pl.pallas_call(
        _ffn_win_tiles_kernel,
        grid=(DEPTH, D // PREP_ROWS),
        in_specs=[pl.BlockSpec((None, PREP_ROWS, 2 * FFN_H), lambda l, r: (l, r, 0))],
        out_specs=pl.BlockSpec((None, FFN_STEPS, PREP_ROWS, 2 * FFN_TH), lambda l, r: (l, 0, r, 0)),
        out_shape=jax.ShapeDtypeStruct((DEPTH, FFN_STEPS, D, 2 * FFN_TH), BF16),
        compiler_params=_cparams(("parallel", "parallel"), 32),
        name="ffn_win_tiles",
    )(w_in)
    full_tiles = FFN_H // FFN_TH
    n_tail = (FFN_H - full_tiles * FFN_TH) // PREP_ROWS
    slab0 = full_tiles * FFN_TH // PREP_ROWS
    tail_specs = [pl.BlockSpec((None, PREP_ROWS, D), functools.partial(lambda l, j, r: (l, slab0 + r, 0), r=r))
                  for r in range(n_tail)]
    wo = pl.pallas_call(
        _ffn_wout_tiles_kernel,
        grid=(DEPTH, FFN_STEPS),
        in_specs=[pl.BlockSpec((None, FFN_TH, D), lambda l, j: (l, jnp.minimum(j, full_tiles - 1), 0))] + tail_specs,
        out_specs=pl.BlockSpec((None, None, FFN_TH, D), lambda l, j: (l, j, 0, 0)),
        out_shape=jax.ShapeDtypeStruct((DEPTH, FFN_STEPS, FFN_TH, D), BF16),
        compiler_params=_cparams(("parallel", "arbitrary"), 32),
        name="ffn_wout_tiles",
    )(w_out, *([w_out] * n_tail))
    return wi, wo


def _proj_tiles_kernel(w_ref, o_ref):
    for t in range(o_ref.shape[0]):
        o_ref[t] = w_ref[:, t * PROJ_TN:(t + 1) * PROJ_TN].astype(BF16)


def _proj_tiles(w, n):
    k = w.shape[1]
    return pl.pallas_call(
        _proj_tiles_kernel,
        grid=(DEPTH, k // PREP_ROWS),
        in_specs=[pl.BlockSpec((None, PREP_ROWS, n), lambda l, r: (l, r, 0))],
        out_specs=pl.BlockSpec((None, n // PROJ_TN, PREP_ROWS, PROJ_TN), lambda l, r: (l, 0, r, 0)),
        out_shape=jax.ShapeDtypeStruct((DEPTH, n // PROJ_TN, k, PROJ_TN), BF16),
        compiler_params=_cparams(("parallel", "parallel"), 32),
        name="proj_tiles",
    )(w)


def _inproj_kernel(x_ref, sh_ref, sc_ref, w_ref, wkr_ref, pa_ref, pz_ref, pb_ref, kr_ref, h_ref):
    _modulate_tile(x_ref, sh_ref, sc_ref, h_ref, pl.program_id(0), TMP)
    h = h_ref[...]
    kr_ref[...] = jnp.dot(h, wkr_ref[...], preferred_element_type=F32)
    for t in range(IN_MAIN // PROJ_TN):
        y = jnp.dot(h, w_ref[t], preferred_element_type=F32)
        c0 = t * PROJ_TN
        if c0 < OFF_HF:
            pa_ref[:, c0:c0 + PROJ_TN] = y.astype(BF16)
        elif c0 < OFF_HG:
            pz_ref[:, c0 - OFF_HF:c0 - OFF_HF + PROJ_TN] = y
        else:
            pb_ref[:, c0 - OFF_HG:c0 - OFF_HG + PROJ_TN] = y.astype(BF16)


def _inproj(x, mods, w_tiles, w_kr, layer):
    def mod_spec(k):
        return pl.BlockSpec((None, SUBLANES, D), lambda i: (layer, 0, k))

    def out_spec(w):
        return pl.BlockSpec((TMP, w), lambda i: (i, 0))

    n_tiles = IN_MAIN // PROJ_TN
    return pl.pallas_call(
        _inproj_kernel,
        grid=(M // TMP,),
        in_specs=[
            pl.BlockSpec((TMP, D), lambda i: (i, 0)),
            mod_spec(3), mod_spec(4),
            pl.BlockSpec((None, n_tiles, D, PROJ_TN), lambda i: (layer, 0, 0, 0), pipeline_mode=pl.Buffered(1)),
            pl.BlockSpec((None, D, LANES), lambda i: (layer, 0, 0)),
        ],
        out_specs=[out_spec(PA_W), out_spec(PZ_W), out_spec(PB_W), out_spec(LANES)],
        out_shape=[jax.ShapeDtypeStruct((M, PA_W), BF16), jax.ShapeDtypeStruct((M, PZ_W), F32),
                   jax.ShapeDtypeStruct((M, PB_W), BF16), jax.ShapeDtypeStruct((M, LANES), F32)],
        scratch_shapes=[pltpu.VMEM((TMP, D), BF16)],
        compiler_params=_cparams(("parallel",), 56),
        name="inproj",
    )(x, mods, mods, w_tiles, w_kr)


def _rope_tables(rot_dim):
    rows = SEQ // GRID_W
    row = jnp.repeat(jnp.arange(rows, dtype=F32), GRID_W)
    colp = jnp.tile(jnp.arange(GRID_W, dtype=F32), rows)
    axis_dim = rot_dim // 2
    inv_freq = ROPE_THETA ** (-jnp.arange(0, axis_dim, 2, dtype=F32) / axis_dim)
    ang_r = row[:, None] * inv_freq
    ang_c = colp[:, None] * inv_freq
    ang = jnp.concatenate([ang_r, ang_r, ang_c, ang_c], axis=-1)
    cos, sin = jnp.cos(ang), jnp.sin(ang)
    quarter = rot_dim // 4
    lane = jnp.arange(rot_dim)
    first = (lane % (2 * quarter)) < quarter
    sin_up = jnp.where(first, -sin, 0.0)
    sin_dn = jnp.where(first, 0.0, sin)
    pad = LANES - rot_dim
    if pad:
        cos = jnp.pad(cos, ((0, 0), (0, pad)), constant_values=1.0)
        sin_up = jnp.pad(sin_up, ((0, 0), (0, pad)))
        sin_dn = jnp.pad(sin_dn, ((0, 0), (0, pad)))
    return cos, sin_up, sin_dn


def _rope(x, cos, sin_up, sin_dn, quarter):
    up = pltpu.roll(x, LANES - quarter, 1)
    dn = pltpu.roll(x, quarter, 1)
    return x * cos + up * sin_up + dn * sin_dn


def _head_norm(x, gain):
    return x * lax.rsqrt(jnp.mean(x * x, axis=-1, keepdims=True) + EPS) * gain


def _tile_is_ctx(t):
    return t >= LAT // TQ


def _rope_idx(t):
    return jnp.where(_tile_is_ctx(t), 0, t % QT_SEQ)


def _gqa_prep_kernel(p_ref, qg_ref, kg_ref, cos_ref, su_ref, sd_ref, q_ref, k_ref):
    is_ctx = _tile_is_ctx(pl.program_id(0))
    cos, su, sd = cos_ref[...], su_ref[...], sd_ref[...]
    scale = HEAD_DIM ** -0.5
    for hh in range(GQA_HEADS + GQA_KV):
        xh = p_ref[:, hh * HEAD_DIM:(hh + 1) * HEAD_DIM].astype(F32)
        is_q = hh < GQA_HEADS
        n = _head_norm(xh, qg_ref[...] if is_q else kg_ref[...])
        y = jnp.where(is_ctx, n, _rope(n, cos, su, sd, HEAD_DIM // 4))
        if is_q:
            q_ref[:, hh * HEAD_DIM:(hh + 1) * HEAD_DIM] = (y * scale).astype(BF16)
        else:
            kk = hh - GQA_HEADS
            k_ref[:, kk * HEAD_DIM:(kk + 1) * HEAD_DIM] = y.astype(BF16)


def _gqa_prep(pa, q_gain, k_gain, tabs):
    width = (GQA_HEADS + GQA_KV) * HEAD_DIM
    tab_spec = pl.BlockSpec((TQ, LANES), lambda t: (_rope_idx(t), 0))
    vec_spec = pl.BlockSpec((1, HEAD_DIM), lambda t: (0, 0))
    return pl.pallas_call(
        _gqa_prep_kernel,
        grid=(M // TQ,),
        in_specs=[pl.BlockSpec((TQ, width), lambda t: (t, 0)), vec_spec, vec_spec,
                  tab_spec, tab_spec, tab_spec],
        out_specs=[pl.BlockSpec((TQ, GQA_HEADS * HEAD_DIM), lambda t: (t, 0)),
                   pl.BlockSpec((TQ, GQA_KV * HEAD_DIM), lambda t: (t, 0))],
        out_shape=[jax.ShapeDtypeStruct((M, GQA_HEADS * HEAD_DIM), BF16),
                   jax.ShapeDtypeStruct((M, GQA_KV * HEAD_DIM), BF16)],
        compiler_params=_cparams(("parallel",), 32),
        name="gqa_prep",
    )(pa, q_gain.reshape(1, HEAD_DIM), k_gain.reshape(1, HEAD_DIM), *tabs)


def _nt_dot(a, b):
    return lax.dot_general(a, b, (((1,), (1,)), ((), ())), preferred_element_type=F32)


def _attn_kernel(q_ref, kl_ref, kc_ref, vl_ref, vc_ref, o_ref, *, n_heads, head_cols, with_ctx):
    def latent_queries():
        for hh in range(n_heads):
            qs, ks, vs, os_ = head_cols(hh)
            q = q_ref[:, qs]
            s = jnp.concatenate([_nt_dot(q, kl_ref[:, ks]), _nt_dot(q, kc_ref[:, ks])], axis=1)
            m = jnp.max(s, axis=-1, keepdims=True)
            e = jnp.exp(s - m)
            l = jnp.sum(e, axis=-1, keepdims=True)
            eb = e.astype(BF16)
            pv = (jnp.dot(eb[:, :SEQ], vl_ref[:, vs], preferred_element_type=F32)
                  + jnp.dot(eb[:, SEQ:], vc_ref[:, vs], preferred_element_type=F32))
            o_ref[:, os_] = (pv / l).astype(BF16)

    def context_queries():
        for hh in range(n_heads):
            qs, ks, vs, os_ = head_cols(hh)
            s = _nt_dot(q_ref[:, qs], kc_ref[:, ks])
            m = jnp.max(s, axis=-1, keepdims=True)
            e = jnp.exp(s - m)
            l = jnp.sum(e, axis=-1, keepdims=True)
            pv = jnp.dot(e.astype(BF16), vc_ref[:, vs], preferred_element_type=F32)
            o_ref[:, os_] = (pv / l).astype(BF16)

    if with_ctx:
        pl.when(pl.program_id(1) < QT_SEQ)(latent_queries)
        pl.when(pl.program_id(1) == QT_SEQ)(context_queries)
    else:
        latent_queries()


def _attention(name, q, k, v, v_col, qw, kw, vw, ow, n_heads, head_cols, with_ctx):
    def q_row(b, i):
        return jnp.where(i < QT_SEQ, b * QT_SEQ + i, LAT // TQ + b)

    def ctx_row(b, i):
        return LAT // CTX + b

    return pl.pallas_call(
        functools.partial(_attn_kernel, n_heads=n_heads, head_cols=head_cols, with_ctx=with_ctx),
        grid=(BATCH, QT_SEQ + (1 if with_ctx else 0)),
        in_specs=[
            pl.BlockSpec((TQ, qw), lambda b, i: (q_row(b, i), 0)),
            pl.BlockSpec((SEQ, kw), lambda b, i: (b, 0)),
            pl.BlockSpec((CTX, kw), lambda b, i: (ctx_row(b, i), 0)),
            pl.BlockSpec((SEQ, vw), lambda b, i: (b, v_col)),
            pl.BlockSpec((CTX, vw), lambda b, i: (ctx_row(b, i), v_col)),
        ],
        out_specs=pl.BlockSpec((TQ, ow), lambda b, i: (q_row(b, i), 0)),
        out_shape=jax.ShapeDtypeStruct((M if with_ctx else LAT, ow), BF16),
        compiler_params=_cparams(("parallel", "arbitrary"), 48),
        name=name,
    )(q, k, k, v, v)


def _gqa_head_cols(hh):
    kv = slice((hh // GQA_GROUP) * HEAD_DIM, (hh // GQA_GROUP + 1) * HEAD_DIM)
    hs = slice(hh * HEAD_DIM, (hh + 1) * HEAD_DIM)
    return hs, kv, kv, hs


def _mla_head_cols(hh):
    qs = slice(hh * MLA_PAD, (hh + 1) * MLA_PAD)
    vs = slice(hh * MLA_V, (hh + 1) * MLA_V)
    return qs, qs, vs, vs


def _gqa_attn(q, k, pa, with_ctx):
    qw, kw = GQA_HEADS * HEAD_DIM, GQA_KV * HEAD_DIM
    return _attention("gqa_attn", q, k, pa, OFF_GV // kw, qw, kw, kw, qw, GQA_HEADS, _gqa_head_cols, with_ctx)


def _mla_attn(q, k, v, with_ctx):
    qw, vw = MLA_HEADS * MLA_PAD, MLA_HEADS * MLA_V
    return _attention("mla_attn", q, k, v, 0, qw, qw, vw, vw, MLA_HEADS, _mla_head_cols, with_ctx)


def _mla_prep_kernel(cq_ref, ckv_ref, kr_ref, qg_ref, kvg_ref, wuq_ref, wukv_ref,
                     cos_ref, su_ref, sd_ref, q_ref, k_ref, v_ref):
    is_ctx = _tile_is_ctx(pl.program_id(0))
    cos, su, sd = cos_ref[...], su_ref[...], sd_ref[...]
    quarter = MLA_ROPE // 4
    scale = MLA_QK ** -0.5

    cq = _head_norm(cq_ref[...].astype(F32), qg_ref[...]).astype(BF16)
    qf = jnp.dot(cq, wuq_ref[...], preferred_element_type=F32)
    ckv = _head_norm(ckv_ref[...].astype(F32), kvg_ref[...]).astype(BF16)
    kvf = jnp.dot(ckv, wukv_ref[...], preferred_element_type=F32)
    kr = kr_ref[...]
    kr = jnp.where(is_ctx, kr, _rope(kr, cos, su, sd, quarter)).astype(BF16)
    for hh in range(MLA_HEADS):
        base = hh * MLA_PAD
        q_ref[:, base:base + MLA_NOPE] = (qf[:, base:base + MLA_NOPE] * scale).astype(BF16)
        qr = qf[:, base + MLA_NOPE:base + MLA_PAD]
        qr = jnp.where(is_ctx, qr, _rope(qr, cos, su, sd, quarter))
        q_ref[:, base + MLA_NOPE:base + MLA_PAD] = (qr * scale).astype(BF16)
        k_ref[:, base:base + MLA_NOPE] = kvf[:, base:base + MLA_NOPE].astype(BF16)
        k_ref[:, base + MLA_NOPE:base + MLA_PAD] = kr
        v_ref[:, hh * MLA_V:(hh + 1) * MLA_V] = kvf[:, base + MLA_NOPE:base + MLA_PAD].astype(BF16)


def _mla_prep(pb, pkr, q_gain, kv_gain, w_uq_pad, w_ukv, tabs):
    tab_spec = pl.BlockSpec((TQ, LANES), lambda t: (_rope_idx(t), 0))
    qk_w = MLA_HEADS * MLA_PAD
    return pl.pallas_call(
        _mla_prep_kernel,
        grid=(M // TQ,),
        in_specs=[
            pl.BlockSpec((TQ, MLA_Q_RANK), lambda t: (t, (OFF_CQ - OFF_HG) // MLA_Q_RANK)),
            pl.BlockSpec((TQ, MLA_KV_RANK), lambda t: (t, (OFF_CKV - OFF_HG) // MLA_KV_RANK)),
            pl.BlockSpec((TQ, LANES), lambda t: (t, 0)),
            pl.BlockSpec((1, MLA_Q_RANK), lambda t: (0, 0)),
            pl.BlockSpec((1, MLA_KV_RANK), lambda t: (0, 0)),
            pl.BlockSpec((MLA_Q_RANK, qk_w), lambda t: (0, 0)),
            pl.BlockSpec((MLA_KV_RANK, qk_w), lambda t: (0, 0)),
            tab_spec, tab_spec, tab_spec,
        ],
        out_specs=[pl.BlockSpec((TQ, qk_w), lambda t: (t, 0)),
                   pl.BlockSpec((TQ, qk_w), lambda t: (t, 0)),
                   pl.BlockSpec((TQ, MLA_HEADS * MLA_V), lambda t: (t, 0))],
        out_shape=[jax.ShapeDtypeStruct((M, qk_w), BF16),
                   jax.ShapeDtypeStruct((M, qk_w), BF16),
                   jax.ShapeDtypeStruct((M, MLA_HEADS * MLA_V), BF16)],
        compiler_params=_cparams(("parallel",), 32),
        name="mla_prep",
    )(pb, pb, pkr, q_gain.reshape(1, -1), kv_gain.reshape(1, -1), w_uq_pad, w_ukv, *tabs)


TILE_CHUNKS = TQ // HCHUNK
DIAG = 8
LEVELS = (8, 16, 32)
LOG2E = 1.4426950408889634


def _split3(x):
    hi = x.astype(BF16)
    r1 = x - hi.astype(F32)
    mid = r1.astype(BF16)
    lo = (r1 - mid.astype(F32)).astype(BF16)
    return hi, mid, lo


def _hgrn_kernel(q_ref, v_ref, z_ref, lg_ref, o_ref, *scratch, reverse, layer):
    cum_refs = scratch[0:HG_HEADS]
    k_refs = scratch[HG_HEADS:2 * HG_HEADS]
    s_refs = scratch[2 * HG_HEADS:]

    lg = [lg_ref[l:l + 1, :] for l in range(DEPTH)]
    mx = functools.reduce(jnp.maximum, lg)
    ex = [jnp.exp(r - mx) for r in lg]
    den = functools.reduce(lambda a_, b_: a_ + b_, ex)
    lb_all = jnp.zeros((1, HG_W), F32)
    for l in range(1, layer + 1):
        lb_all = lb_all + ex[l] / den
    log_lb_all = jnp.log(lb_all)
    log_1m_all = jnp.log1p(-lb_all)

    ti = lax.broadcasted_iota(jnp.int32, (HCHUNK, HCHUNK), 0)
    si = lax.broadcasted_iota(jnp.int32, (HCHUNK, HCHUNK), 1)
    causal = (si >= ti) if reverse else (si <= ti)
    tri = causal.astype(BF16)
    diag_mask = (causal & ((ti // DIAG) == (si // DIAG))).astype(F32)
    ri = lax.broadcasted_iota(jnp.int32, (DIAG * HG_DK, LANES), 0)
    ci = lax.broadcasted_iota(jnp.int32, (DIAG * HG_DK, LANES), 1)
    sel = ((ri // HG_DK) == (ci % DIAG)).astype(BF16)
    edge = 0 if reverse else HCHUNK - 1

    @pl.when(pl.program_id(1) == 0)
    def _():
        for s_ref in s_refs:
            s_ref[...] = jnp.zeros_like(s_ref)

    heads = range(HG_HEADS)

    def lanes(hh):
        return slice(hh * HG_DK, (hh + 1) * HG_DK)

    def chunk_all_heads(r0):
        rows = pl.ds(r0, HCHUNK)
        q = [q_ref[rows, lanes(hh)].astype(F32) * (HG_DK ** -0.5) for hh in heads]
        v = [v_ref[rows, lanes(hh)] for hh in heads]

        z = z_ref[rows, :]
        u = jnp.exp(-jnp.abs(z))
        w = 1.0 + u
        b_ = log_1m_all + (jnp.minimum(z, 0.0) - jnp.log(w))
        log_f = jnp.maximum(log_lb_all, b_) + jnp.log(1.0 + jnp.exp(-jnp.abs(log_lb_all - b_)))
        k_all = (1.0 - lb_all) * (jnp.where(z >= 0.0, u, 1.0) / w)
        k = [k_all[:, lanes(hh)] for hh in heads]

        cum3 = jnp.dot(tri, jnp.concatenate(_split3(log_f), axis=1), preferred_element_type=F32)
        cum_all = (cum3[:, :HG_W] + cum3[:, HG_W:2 * HG_W] + cum3[:, 2 * HG_W:]) * LOG2E
        cum = [cum_all[:, lanes(hh)] for hh in heads]
        for hh in heads:
            cum_refs[hh][...] = cum[hh]
            k_refs[hh][...] = k[hh]
        tot = [cum_refs[hh][edge:edge + 1, :] for hh in heads]

        pair_lhs = []
        for hh in heads:
            cols = []
            for s in range(DIAG):
                cs = jnp.concatenate(
                    [jnp.broadcast_to(cum_refs[hh][blk * DIAG + s:blk * DIAG + s + 1, :], (DIAG, HG_DK))
                     for blk in range(HCHUNK // DIAG)], axis=0)
                ks = jnp.concatenate(
                    [jnp.broadcast_to(k_refs[hh][blk * DIAG + s:blk * DIAG + s + 1, :], (DIAG, HG_DK))
                     for blk in range(HCHUNK // DIAG)], axis=0)
                cols.append((q[hh] * jnp.exp2(jnp.minimum(cum[hh] - cs, 0.0)) * ks).astype(BF16))
            pair_lhs.append(jnp.concatenate(cols, axis=1))

        span_lhs, span_rhs = [], []
        for hh in heads:
            lhs, rhs = [], []
            for hs in LEVELS:
                for blk in range(HCHUNK // (2 * hs)):
                    lo, mid, hi = blk * 2 * hs, blk * 2 * hs + hs, (blk + 1) * 2 * hs
                    k_rows, q_rows = ((mid, hi), (lo, mid)) if reverse else ((lo, mid), (mid, hi))
                    ref_row = mid if reverse else mid - 1
                    rr = cum_refs[hh][ref_row:ref_row + 1, :]
                    qs = q[hh][q_rows[0]:q_rows[1], :] * jnp.exp2(cum[hh][q_rows[0]:q_rows[1], :] - rr)
                    ks = k[hh][k_rows[0]:k_rows[1], :] * jnp.exp2(rr - cum[hh][k_rows[0]:k_rows[1], :])
                    zq = [jnp.zeros((q_rows[0], HG_DK), F32), qs, jnp.zeros((HCHUNK - q_rows[1], HG_DK), F32)]
                    zk = [jnp.zeros((k_rows[0], HG_DK), F32), ks, jnp.zeros((HCHUNK - k_rows[1], HG_DK), F32)]
                    lhs.append(jnp.concatenate([p_ for p_ in zq if p_.shape[0]], axis=0))
                    rhs.append(jnp.concatenate([p_ for p_ in zk if p_.shape[0]], axis=0))
            span_lhs.append(jnp.concatenate(lhs, axis=1).astype(BF16))
            span_rhs.append(jnp.concatenate(rhs, axis=1).astype(BF16))

        state = [s_refs[hh][...] for hh in heads]
        qe = [(q[hh] * jnp.exp2(cum[hh])).astype(BF16) for hh in heads]
        ke = [(k[hh] * jnp.exp2(tot[hh] - cum[hh])).astype(BF16) for hh in heads]

        pair = [jnp.dot(pair_lhs[hh], sel, preferred_element_type=F32) for hh in heads]
        a_off = [_nt_dot(span_lhs[hh], span_rhs[hh]) for hh in heads]
        inter = [_nt_dot(qe[hh], state[hh].astype(BF16)) for hh in heads]
        upd = [lax.dot_general(v[hh], ke[hh], (((0,), (0,)), ((), ())), preferred_element_type=F32)
               for hh in heads]
        a = [(a_off[hh] + pair[hh][:, :HCHUNK] * diag_mask).astype(BF16) for hh in heads]
        o = [jnp.dot(a[hh], v[hh], preferred_element_type=F32) for hh in heads]
        for hh in heads:
            o_ref[rows, lanes(hh)] = o[hh] + inter[hh]
            s_refs[hh][...] = state[hh] * jnp.exp2(tot[hh]) + upd[hh]

    def chunk(jstep, carry):
        c = TILE_CHUNKS - 1 - jstep if reverse else jstep
        chunk_all_heads(pl.multiple_of(c * HCHUNK, HCHUNK))
        return carry

    lax.fori_loop(0, TILE_CHUNKS, chunk, 0, unroll=2)


def _hgrn(pa, pz, lb_logits_dir, layer, reverse):
    def tile_row(b, j):
        lat = b * QT_SEQ + (QT_SEQ - j if reverse else j - 1)
        return jnp.where(j == 0, LAT // TQ + b, lat)

    def spec(col):
        return pl.BlockSpec((TQ, HG_W), lambda b, j: (tile_row(b, j), col))

    return pl.pallas_call(
        functools.partial(_hgrn_kernel, reverse=reverse, layer=layer),
        grid=(BATCH, QT_SEQ + 1),
        in_specs=[spec(OFF_HQ // HG_W), spec(OFF_HI // HG_W), spec(1 if reverse else 0),
                  pl.BlockSpec((DEPTH, HG_W), lambda b, j: (0, 0))],
        out_specs=spec(0),
        out_shape=jax.ShapeDtypeStruct((M, HG_W), F32),
        scratch_shapes=([pltpu.VMEM((HCHUNK, HG_DK), F32)] * (2 * HG_HEADS)
                        + [pltpu.VMEM((HG_DK, HG_DK), F32)] * HG_HEADS),
        compiler_params=_cparams(("parallel", "arbitrary"), 32),
        name="hgrn_bwd" if reverse else "hgrn_fwd",
    )(pa, pa, pz, lb_logits_dir)


def _outproj_kernel(x_ref, gt_ref, oa_ref, of_ref, ob_ref, hg_ref, om_ref, ng_ref, w_ref, o_ref, lhs_ref):
    a_w = GQA_HEADS * HEAD_DIM
    lhs_ref[:, 0:a_w] = oa_ref[...]
    lhs_ref[:, a_w + HG_W:] = om_ref[...]
    gain = ng_ref[...]
    for hh in range(HG_HEADS):
        sl = slice(hh * HG_DK, (hh + 1) * HG_DK)
        g = hg_ref[:, sl].astype(F32)
        y = _head_norm(of_ref[:, sl] + ob_ref[:, sl], gain) * (g * _sigmoid(g))
        lhs_ref[:, a_w + hh * HG_DK:a_w + (hh + 1) * HG_DK] = y.astype(BF16)

    lhs = lhs_ref[...]
    gate = _mod_vec(gt_ref, pl.program_id(0), TMP)
    for n in range(D // PROJ_TN):
        cols = slice(n * PROJ_TN, (n + 1) * PROJ_TN)
        y = jnp.dot(lhs, w_ref[n], preferred_element_type=F32)
        o_ref[:, cols] = x_ref[:, cols] + gate[:, cols] * y


def _outproj(x, mods, o_a, o_f, o_b, pb, o_m, norm_gain, w_tiles, layer, rows):
    def row_spec(w, col=0):
        return pl.BlockSpec((TMP, w), lambda i: (i, col))

    return pl.pallas_call(
        _outproj_kernel,
        grid=(rows // TMP,),
        in_specs=[
            row_spec(D),
            pl.BlockSpec((None, SUBLANES, D), lambda i: (layer, 0, 5)),
            row_spec(GQA_HEADS * HEAD_DIM), row_spec(HG_W), row_spec(HG_W),
            row_spec(HG_W, 0),
            row_spec(MLA_HEADS * MLA_V),
            pl.BlockSpec((1, HG_DK), lambda i: (0, 0)),
            pl.BlockSpec((None, D // PROJ_TN, D, PROJ_TN), lambda i: (layer, 0, 0, 0),
                         pipeline_mode=pl.Buffered(1)),
        ],
        out_specs=row_spec(D),
        out_shape=jax.ShapeDtypeStruct((rows, D), F32),
        scratch_shapes=[pltpu.VMEM((TMP, D), BF16)],
        compiler_params=_cparams(("parallel",), 56),
        name="outproj",
    )(x, mods, o_a, o_f, o_b, pb, o_m, norm_gain.reshape(1, HG_DK), w_tiles)


def _pad_w_uq(w_uq):
    w = w_uq.reshape(DEPTH, MLA_Q_RANK, MLA_HEADS, MLA_QK)
    w = jnp.pad(w, ((0, 0), (0, 0), (0, 0), (0, MLA_PAD - MLA_QK)))
    return w.reshape(DEPTH, MLA_Q_RANK, MLA_HEADS * MLA_PAD).astype(BF16)


def kernel(x, c, ctx, c_ctx, w_mod, b_mod, w_ffn1_in, w_ffn1_out, w_in, w_uq, w_ukv, w_out,
           gqa_q_gain, gqa_k_gain, mla_q_gain, mla_kv_gain, hgrn_lb_logits, hgrn_norm_gain,
           w_ffn2_in, w_ffn2_out, final_gain):
    cvec8 = jnp.concatenate([c, c_ctx[None, :], jnp.zeros((SUBLANES - BATCH - 1, D), F32)], axis=0)
    mods = _modulation(cvec8, w_mod, b_mod)
    tabs_h = _rope_tables(HEAD_DIM)
    tabs_r = _rope_tables(MLA_ROPE)
    f1_in, f1_out = _ffn_weights(w_ffn1_in, w_ffn1_out)
    f2_in, f2_out = _ffn_weights(w_ffn2_in, w_ffn2_out)
    w_in_t = _proj_tiles(w_in, IN_MAIN)
    w_out_t = _proj_tiles(w_out, D)
    w_kr = jnp.pad(w_in[:, :, OFF_KR:], ((0, 0), (0, 0), (0, LANES - MLA_ROPE))).astype(BF16)
    w_uq_pad = _pad_w_uq(w_uq)
    w_ukv_b = w_ukv.astype(BF16)

    xa = None
    for l in range(DEPTH):
        last = l == DEPTH - 1
        if l == 0:
            xa = _ffn(x.reshape(LAT, D), mods, f1_in, f1_out, l, 0, ctx=ctx.reshape(CTXR, D))
        else:
            xa = _ffn(xa, mods, f1_in, f1_out, l, 0)
        pa, pz, pb, pkr = _inproj(xa, mods, w_in_t, w_kr, l)
        q_a, k_a = _gqa_prep(pa, gqa_q_gain[l], gqa_k_gain[l], tabs_h)
        o_a = _gqa_attn(q_a, k_a, pa, not last)
        q_m, k_m, v_m = _mla_prep(pb, pkr, mla_q_gain[l], mla_kv_gain[l], w_uq_pad[l], w_ukv_b[l], tabs_r)
        o_m = _mla_attn(q_m, k_m, v_m, not last)
        o_f = _hgrn(pa, pz, hgrn_lb_logits[0], l, False)
        o_b = _hgrn(pa, pz, hgrn_lb_logits[1], l, True)
        xa = _outproj(xa, mods, o_a, o_f, o_b, pb, o_m, hgrn_norm_gain[l], w_out_t, l, LAT if last else M)
        xa = _ffn(xa, mods, f2_in, f2_out, l, 6, final_gain=final_gain if last else None)
    return xa
```

```python
import functools

import jax
import jax.numpy as jnp
from jax import lax
from jax.experimental import pallas as pl
from jax.experimental.pallas import tpu as pltpu

F32 = jnp.float32
BF16 = jnp.bfloat16

D = 2048
BATCH = 4
SEQ = 2048
DEPTH = 2
GRID_W = 64
CTX = 256
EPS = 1e-6
ROPE_THETA = 10000.0

HEAD_DIM = 128
GQA_HEADS = 8
GQA_KV = 2
GQA_GROUP = GQA_HEADS // GQA_KV
HG_HEADS = 4
HG_DK = 128
HG_W = HG_HEADS * HG_DK
MLA_HEADS = 4
MLA_Q_RANK = 512
MLA_KV_RANK = 256
MLA_NOPE = 128
MLA_ROPE = 64
MLA_V = 128
MLA_QK = MLA_NOPE + MLA_ROPE
MLA_PAD = 256
FFN_H = 5504
N_MOD = 9

LAT = BATCH * SEQ
CTXR = BATCH * CTX
M = LAT + CTXR
OFF_GQ, OFF_GK, OFF_GV = 0, 1024, 1280
OFF_HQ, OFF_HI, OFF_HF, OFF_HB, OFF_HG = 1536, 2048, 2560, 3072, 3584
OFF_CQ, OFF_CKV, OFF_KR = 4096, 4608, 4864
IN_MAIN = OFF_KR
PA_W, PZ_W, PB_W = OFF_HF, OFF_HG - OFF_HF, OFF_KR - OFF_HG

LANES = 128
SUBLANES = 8
MXU_W = 256

TM = 1024
TMP = 512
ROWC = 64
TQ = 256
QT_SEQ = SEQ // TQ
HCHUNK = 64
FFN_TH = 512
FFN_HP = -(-FFN_H // FFN_TH) * FFN_TH
FFN_STEPS = FFN_HP // FFN_TH
FFN_RC = 512
MOD_TN = 1024
PROJ_TN = MXU_W
PREP_ROWS = 128
V_AUG = MXU_W
LOG2E = 1.4426950408889634


def _cparams(sem, vmem_mb):
    return pltpu.CompilerParams(dimension_semantics=sem, vmem_limit_bytes=vmem_mb * 1024 * 1024)


def _sigmoid(x):
    return 1.0 / (1.0 + jnp.exp(-x))


def _nt_dot(a, b):
    return lax.dot_general(a, b, (((1,), (1,)), ((), ())), preferred_element_type=F32)


def _mod_kernel(c_ref, w_ref, b_ref, o_ref):
    c = c_ref[...]
    a = (c * _sigmoid(c)).astype(BF16)
    o_ref[...] = jnp.dot(a, w_ref[...].astype(BF16), preferred_element_type=F32) + b_ref[...]


def _modulation(cvec8, w_mod, b_mod):
    n = N_MOD * D
    return pl.pallas_call(
        _mod_kernel,
        grid=(DEPTH, n // MOD_TN),
        in_specs=[
            pl.BlockSpec((SUBLANES, D), lambda l, j: (0, 0)),
            pl.BlockSpec((None, D, MOD_TN), lambda l, j: (l, 0, j)),
            pl.BlockSpec((None, 1, MOD_TN), lambda l, j: (l, 0, j)),
        ],
        out_specs=pl.BlockSpec((None, SUBLANES, MOD_TN), lambda l, j: (l, 0, j)),
        out_shape=jax.ShapeDtypeStruct((DEPTH, SUBLANES, n), F32),
        compiler_params=_cparams(("parallel", "parallel"), 40),
        name="modulation",
    )(cvec8, w_mod, b_mod.reshape(DEPTH, 1, n))


def _mod_vec(ref, tile, tm):
    row = jnp.where(tile < LAT // tm, tile // (SEQ // tm), BATCH)
    return ref[pl.ds(row, 1), :]


def _modulate_tile(x_ref, shift_ref, scale_ref, h_ref, tile, tm):
    sh = _mod_vec(shift_ref, tile, tm)
    sc = 1.0 + _mod_vec(scale_ref, tile, tm)

    def body(c, carry):
        r0 = pl.multiple_of(c * ROWC, ROWC)
        xc = x_ref[pl.ds(r0, ROWC), :]
        ms = jnp.mean(xc * xc, axis=-1, keepdims=True)
        h_ref[pl.ds(r0, ROWC), :] = (xc * lax.rsqrt(ms + EPS) * sc + sh).astype(BF16)
        return carry

    lax.fori_loop(0, tm // ROWC, body, 0)


def _ffn_kernel(*refs, two_inputs, final):
    refs = list(refs)
    x_ref = refs.pop(0)
    c_ref = refs.pop(0) if two_inputs else None
    sh_ref, sc_ref, gt_ref = refs.pop(0), refs.pop(0), refs.pop(0)
    fg_ref = refs.pop(0) if final else None
    wi_ref, wo_ref, o_ref, h_ref = refs
    i = pl.program_id(0)
    j = pl.program_id(1)

    def per_source(fn):
        if two_inputs:
            pl.when(i < LAT // TM)(lambda: fn(x_ref))
            pl.when(i >= LAT // TM)(lambda: fn(c_ref))
        else:
            fn(x_ref)

    @pl.when(j == 0)
    def _():
        per_source(lambda src: _modulate_tile(src, sh_ref, sc_ref, h_ref, i, TM))
        o_ref[...] = jnp.zeros_like(o_ref)

    for rc in range(TM // FFN_RC):
        rows = slice(rc * FFN_RC, (rc + 1) * FFN_RC)
        r = jnp.dot(h_ref[rows, :], wi_ref[...], preferred_element_type=F32)
        g = r[:, :FFN_TH]
        u = r[:, FFN_TH:]
        a = (g * _sigmoid(g) * u).astype(BF16)
        o_ref[rows, :] += jnp.dot(a, wo_ref[...], preferred_element_type=F32)

    @pl.when(j == FFN_STEPS - 1)
    def _():
        gate = 0.5 * _mod_vec(gt_ref, i, TM)

        def epilogue(src):
            def body(c, carry):
                r0 = pl.multiple_of(c * ROWC, ROWC)
                y = src[pl.ds(r0, ROWC), :] + gate * o_ref[pl.ds(r0, ROWC), :]
                if final:
                    y = y * lax.rsqrt(jnp.mean(y * y, axis=-1, keepdims=True) + EPS) * fg_ref[...]
                o_ref[pl.ds(r0, ROWC), :] = y
                return carry

            lax.fori_loop(0, TM // ROWC, body, 0)

        per_source(epilogue)


def _ffn(x, mods, wi, wo, layer, mod0, ctx=None, final_gain=None):
    two_inputs, final = ctx is not None, final_gain is not None
    rows = M if two_inputs else x.shape[0]
    lat_tiles = LAT // TM

    def mod_spec(k):
        return pl.BlockSpec((None, SUBLANES, D), lambda i, j: (layer, 0, mod0 + k))

    in_specs = [pl.BlockSpec((TM, D), lambda i, j: (jnp.minimum(i, lat_tiles - 1) if two_inputs else i, 0),
                             pipeline_mode=pl.Buffered(1))]
    args = [x]
    if two_inputs:
        in_specs.append(pl.BlockSpec((TM, D), lambda i, j: (0, 0), pipeline_mode=pl.Buffered(1)))
        args.append(ctx)
    in_specs += [mod_spec(0), mod_spec(1), mod_spec(2)]
    args += [mods, mods, mods]
    if final:
        in_specs.append(pl.BlockSpec((1, D), lambda i, j: (0, 0)))
        args.append(final_gain.reshape(1, D))
        out_spec = pl.BlockSpec((None, TM, D), lambda i, j: (i // (SEQ // TM), i % (SEQ // TM), 0))
        out_shape = jax.ShapeDtypeStruct((BATCH, SEQ, D), F32)
    else:
        out_spec = pl.BlockSpec((TM, D), lambda i, j: (i, 0))
        out_shape = jax.ShapeDtypeStruct((rows, D), F32)
    in_specs += [pl.BlockSpec((None, None, D, 2 * FFN_TH), lambda i, j: (layer, j, 0, 0)),
                 pl.BlockSpec((None, None, FFN_TH, D), lambda i, j: (layer, j, 0, 0))]
    args += [wi, wo]
    return pl.pallas_call(
        functools.partial(_ffn_kernel, two_inputs=two_inputs, final=final),
        grid=(rows // TM, FFN_STEPS),
        in_specs=in_specs,
        out_specs=out_spec,
        out_shape=out_shape,
        scratch_shapes=[pltpu.VMEM((TM, D), BF16)],
        compiler_params=_cparams(("parallel", "arbitrary"), 58),
        name="ffn",
    )(*args)


def _ffn_win_tiles_kernel(w_ref, o_ref):
    for j in range(FFN_STEPS):
        valid = min(FFN_TH, FFN_H - j * FFN_TH)
        for part, base in ((0, 0), (1, FFN_H)):
            c0 = part * FFN_TH
            o_ref[j, :, c0:c0 + valid] = w_ref[:, base + j * FFN_TH:base + j * FFN_TH + valid].astype(BF16)
            if valid < FFN_TH:
                o_ref[j, :, c0 + valid:c0 + FFN_TH] = jnp.zeros((PREP_ROWS, FFN_TH - valid), BF16)


def _ffn_wout_tiles_kernel(w_ref, *rest):
    tails, o_ref = rest[:-1], rest[-1]
    last = pl.program_id(1) == FFN_STEPS - 1

    @pl.when(jnp.logical_not(last))
    def _():
        o_ref[...] = w_ref[...].astype(BF16)

    @pl.when(last)
    def _():
        for r, t_ref in enumerate(tails):
            o_ref[r * PREP_ROWS:(r + 1) * PREP_ROWS, :] = t_ref[...].astype(BF16)
        n = len(tails) * PREP_ROWS
        o_ref[n:, :] = jnp.zeros((FFN_TH - n, D), BF16)


def _ffn_weights(w_in, w_out):
    wi = pl.pallas_call(
        _ffn_win_tiles_kernel,
        grid=(DEPTH, D // PREP_ROWS),
        in_specs=[pl.BlockSpec((None, PREP_ROWS, 2 * FFN_H), lambda l, r: (l, r, 0))],
        out_specs=pl.BlockSpec((None, FFN_STEPS, PREP_ROWS, 2 * FFN_TH), lambda l, r: (l, 0, r, 0)),
        out_shape=jax.ShapeDtypeStruct((DEPTH, FFN_STEPS, D, 2 * FFN_TH), BF16),
        compiler_params=_cparams(("parallel", "parallel"), 32),
        name="ffn_win_tiles",
    )(w_in)
    full_tiles = FFN_H // FFN_TH
    n_tail = (FFN_H - full_tiles * FFN_TH) // PREP_ROWS
    slab0 = full_tiles * FFN_TH // PREP_ROWS
    tail_specs = [pl.BlockSpec((None, PREP_ROWS, D), functools.partial(lambda l, j, r: (l, slab0 + r, 0), r=r))
                  for r in range(n_tail)]
    wo = pl.pallas_call(
        _ffn_wout_tiles_kernel,
        grid=(DEPTH, FFN_STEPS),
        in_specs=[pl.BlockSpec((None, FFN_TH, D), lambda l, j: (l, jnp.minimum(j, full_tiles - 1), 0))] + tail_specs,
        out_specs=pl.BlockSpec((None, None, FFN_TH, D), lambda l, j: (l, j, 0, 0)),
        out_shape=jax.ShapeDtypeStruct((DEPTH, FFN_STEPS, FFN_TH, D), BF16),
        compiler_params=_cparams(("parallel", "arbitrary"), 32),
        name="ffn_wout_tiles",
    )(w_out, *([w_out] * n_tail))
    return wi, wo


def _proj_tiles_kernel(w_ref, o_ref):
    for t in range(o_ref.shape[0]):
        o_ref[t] = w_ref[:, t * PROJ_TN:(t + 1) * PROJ_TN].astype(BF16)


def _proj_tiles(w, n):
    k = w.shape[1]
    return pl.pallas_call(
        _proj_tiles_kernel,
        grid=(DEPTH, k // PREP_ROWS),
        in_specs=[pl.BlockSpec((None, PREP_ROWS, n), lambda l, r: (l, r, 0))],
        out_specs=pl.BlockSpec((None, n // PROJ_TN, PREP_ROWS, PROJ_TN), lambda l, r: (l, 0, r, 0)),
        out_shape=jax.ShapeDtypeStruct((DEPTH, n // PROJ_TN, k, PROJ_TN), BF16),
        compiler_params=_cparams(("parallel", "parallel"), 32),
        name="proj_tiles",
    )(w)


def _cast_rows_kernel(w_ref, o_ref):
    o_ref[...] = w_ref[...].astype(BF16)


def _inproj_weight_tiles(w_in_t):
    n_tiles = IN_MAIN // PROJ_TN
    return pl.pallas_call(
        _cast_rows_kernel,
        grid=(DEPTH, n_tiles),
        in_specs=[pl.BlockSpec((None, PROJ_TN, D), lambda l, t: (l, t, 0))],
        out_specs=pl.BlockSpec((None, None, PROJ_TN, D), lambda l, t: (l, t, 0, 0)),
        out_shape=jax.ShapeDtypeStruct((DEPTH, n_tiles, PROJ_TN, D), BF16),
        compiler_params=_cparams(("parallel", "parallel"), 32),
        name="inproj_weight_tiles",
    )(w_in_t)


def _inproj_kernel(x_ref, sh_ref, sc_ref, w_ref, wkr_ref, pa_ref, pz_ref, pb_ref, kr_ref, h_ref):
    _modulate_tile(x_ref, sh_ref, sc_ref, h_ref, pl.program_id(0), TMP)
    h = h_ref[...]
    kr_ref[...] = _nt_dot(h, wkr_ref[...])
    for t in range(IN_MAIN // PROJ_TN):
        y = _nt_dot(h, w_ref[t])
        c0 = t * PROJ_TN
        if c0 < OFF_HF:
            pa_ref[:, c0:c0 + PROJ_TN] = y.astype(BF16)
        elif c0 < OFF_HG:
            pz_ref[:, c0 - OFF_HF:c0 - OFF_HF + PROJ_TN] = y
        else:
            pb_ref[:, c0 - OFF_HG:c0 - OFF_HG + PROJ_TN] = y.astype(BF16)


def _inproj(x, mods, w_tiles, w_kr, layer):
    def mod_spec(k):
        return pl.BlockSpec((None, SUBLANES, D), lambda i: (layer, 0, k))

    def out_spec(w):
        return pl.BlockSpec((TMP, w), lambda i: (i, 0))

    n_tiles = IN_MAIN // PROJ_TN
    return pl.pallas_call(
        _inproj_kernel,
        grid=(M // TMP,),
        in_specs=[
            pl.BlockSpec((TMP, D), lambda i: (i, 0)),
            mod_spec(3), mod_spec(4),
            pl.BlockSpec((None, n_tiles, PROJ_TN, D), lambda i: (layer, 0, 0, 0), pipeline_mode=pl.Buffered(1)),
            pl.BlockSpec((None, LANES, D), lambda i: (layer, 0, 0)),
        ],
        out_specs=[out_spec(PA_W), out_spec(PZ_W), out_spec(PB_W), out_spec(LANES)],
        out_shape=[jax.ShapeDtypeStruct((M, PA_W), BF16), jax.ShapeDtypeStruct((M, PZ_W), F32),
                   jax.ShapeDtypeStruct((M, PB_W), BF16), jax.ShapeDtypeStruct((M, LANES), F32)],
        scratch_shapes=[pltpu.VMEM((TMP, D), BF16)],
        compiler_params=_cparams(("parallel",), 56),
        name="inproj",
    )(x, mods, mods, w_tiles, w_kr)


def _rope_tables(rot_dim):
    rows = SEQ // GRID_W
    row = jnp.repeat(jnp.arange(rows, dtype=F32), GRID_W)
    colp = jnp.tile(jnp.arange(GRID_W, dtype=F32), rows)
    axis_dim = rot_dim // 2
    inv_freq = ROPE_THETA ** (-jnp.arange(0, axis_dim, 2, dtype=F32) / axis_dim)
    ang_r = row[:, None] * inv_freq
    ang_c = colp[:, None] * inv_freq
    ang = jnp.concatenate([ang_r, ang_r, ang_c, ang_c], axis=-1)
    cos, sin = jnp.cos(ang), jnp.sin(ang)
    quarter = rot_dim // 4
    lane = jnp.arange(rot_dim)
    first = (lane % (2 * quarter)) < quarter
    sin_up = jnp.where(first, -sin, 0.0)
    sin_dn = jnp.where(first, 0.0, sin)
    pad = LANES - rot_dim
    if pad:
        cos = jnp.pad(cos, ((0, 0), (0, pad)), constant_values=1.0)
        sin_up = jnp.pad(sin_up, ((0, 0), (0, pad)))
        sin_dn = jnp.pad(sin_dn, ((0, 0), (0, pad)))
    return cos, sin_up, sin_dn


def _rope(x, cos, sin_up, sin_dn, quarter):
    up = pltpu.roll(x, LANES - quarter, 1)
    dn = pltpu.roll(x, quarter, 1)
    return x * cos + up * sin_up + dn * sin_dn


def _head_norm(x, gain):
    return x * lax.rsqrt(jnp.mean(x * x, axis=-1, keepdims=True) + EPS) * gain


def _tile_is_ctx(t):
    return t >= LAT // TQ


def _rope_idx(t):
    return jnp.where(_tile_is_ctx(t), 0, t % QT_SEQ)


def _gqa_prep_kernel(p_ref, qg_ref, kg_ref, cos_ref, su_ref, sd_ref, q_ref, k_ref, v_ref):
    is_ctx = _tile_is_ctx(pl.program_id(0))
    cos, su, sd = cos_ref[...], su_ref[...], sd_ref[...]
    scale = HEAD_DIM ** -0.5 * LOG2E
    for kk in range(GQA_KV):
        v_ref[:, kk * V_AUG:kk * V_AUG + HEAD_DIM] = p_ref[:, OFF_GV + kk * HEAD_DIM:OFF_GV + (kk + 1) * HEAD_DIM]
        v_ref[:, kk * V_AUG + HEAD_DIM:(kk + 1) * V_AUG] = jnp.ones((TQ, V_AUG - HEAD_DIM), BF16)
    for hh in range(GQA_HEADS + GQA_KV):
        xh = p_ref[:, hh * HEAD_DIM:(hh + 1) * HEAD_DIM].astype(F32)
        is_q = hh < GQA_HEADS
        n = _head_norm(xh, qg_ref[...] if is_q else kg_ref[...])
        y = jnp.where(is_ctx, n, _rope(n, cos, su, sd, HEAD_DIM // 4))
        if is_q:
            q_ref[:, hh * HEAD_DIM:(hh + 1) * HEAD_DIM] = (y * scale).astype(BF16)
        else:
            kk = hh - GQA_HEADS
            k_ref[:, kk * HEAD_DIM:(kk + 1) * HEAD_DIM] = y.astype(BF16)


def _gqa_prep(pa, q_gain, k_gain, tabs):
    width = OFF_HQ
    tab_spec = pl.BlockSpec((TQ, LANES), lambda t: (_rope_idx(t), 0))
    vec_spec = pl.BlockSpec((1, HEAD_DIM), lambda t: (0, 0))
    return pl.pallas_call(
        _gqa_prep_kernel,
        grid=(M // TQ,),
        in_specs=[pl.BlockSpec((TQ, width), lambda t: (t, 0)), vec_spec, vec_spec,
                  tab_spec, tab_spec, tab_spec],
        out_specs=[pl.BlockSpec((TQ, GQA_HEADS * HEAD_DIM), lambda t: (t, 0)),
                   pl.BlockSpec((TQ, GQA_KV * HEAD_DIM), lambda t: (t, 0)),
                   pl.BlockSpec((TQ, GQA_KV * V_AUG), lambda t: (t, 0))],
        out_shape=[jax.ShapeDtypeStruct((M, GQA_HEADS * HEAD_DIM), BF16),
                   jax.ShapeDtypeStruct((M, GQA_KV * HEAD_DIM), BF16),
                   jax.ShapeDtypeStruct((M, GQA_KV * V_AUG), BF16)],
        compiler_params=_cparams(("parallel",), 32),
        name="gqa_prep",
    )(pa, q_gain.reshape(1, HEAD_DIM), k_gain.reshape(1, HEAD_DIM), *tabs)


def _attn_kernel(q_ref, kl_ref, kc_ref, vl_ref, vc_ref, o_ref, *, n_heads, head_cols, with_ctx):
    def finish(pv, os_):
        e_w = os_.stop - os_.start
        o_ref[:, os_] = (pv[:, :e_w] / pv[:, e_w:2 * e_w]).astype(BF16)

    def latent_queries():
        for hh in range(n_heads):
            qs, ks, vs, os_ = head_cols(hh)
            q = q_ref[:, qs]
            s = jnp.concatenate([_nt_dot(q, kl_ref[:, ks]), _nt_dot(q, kc_ref[:, ks])], axis=1)
            eb = jnp.exp2(s - jnp.max(s, axis=-1, keepdims=True)).astype(BF16)
            finish(jnp.dot(eb[:, :SEQ], vl_ref[:, vs], preferred_element_type=F32)
                   + jnp.dot(eb[:, SEQ:], vc_ref[:, vs], preferred_element_type=F32), os_)

    def context_queries():
        for hh in range(n_heads):
            qs, ks, vs, os_ = head_cols(hh)
            s = _nt_dot(q_ref[:, qs], kc_ref[:, ks])
            eb = jnp.exp2(s - jnp.max(s, axis=-1, keepdims=True)).astype(BF16)
            finish(jnp.dot(eb, vc_ref[:, vs], preferred_element_type=F32), os_)

    if with_ctx:
        pl.when(pl.program_id(1) < QT_SEQ)(latent_queries)
        pl.when(pl.program_id(1) == QT_SEQ)(context_queries)
    else:
        latent_queries()


def _attention(name, q, k, v, v_col, qw, kw, vw, ow, n_heads, head_cols, with_ctx):
    def q_row(b, i):
        return jnp.where(i < QT_SEQ, b * QT_SEQ + i, LAT // TQ + b)

    def ctx_row(b, i):
        return LAT // CTX + b

    return pl.pallas_call(
        functools.partial(_attn_kernel, n_heads=n_heads, head_cols=head_cols, with_ctx=with_ctx),
        grid=(BATCH, QT_SEQ + (1 if with_ctx else 0)),
        in_specs=[
            pl.BlockSpec((TQ, qw), lambda b, i: (q_row(b, i), 0)),
            pl.BlockSpec((SEQ, kw), lambda b, i: (b, 0)),
            pl.BlockSpec((CTX, kw), lambda b, i: (ctx_row(b, i), 0)),
            pl.BlockSpec((SEQ, vw), lambda b, i: (b, v_col)),
            pl.BlockSpec((CTX, vw), lambda b, i: (ctx_row(b, i), v_col)),
        ],
        out_specs=pl.BlockSpec((TQ, ow), lambda b, i: (q_row(b, i), 0)),
        out_shape=jax.ShapeDtypeStruct((M if with_ctx else LAT, ow), BF16),
        compiler_params=_cparams(("parallel", "arbitrary"), 48),
        name=name,
    )(q, k, k, v, v)


def _gqa_head_cols(hh):
    kk = hh // GQA_GROUP
    kv = slice(kk * HEAD_DIM, (kk + 1) * HEAD_DIM)
    hs = slice(hh * HEAD_DIM, (hh + 1) * HEAD_DIM)
    return hs, kv, slice(kk * V_AUG, (kk + 1) * V_AUG), hs


def _mla_head_cols(hh):
    qs = slice(hh * MLA_PAD, (hh + 1) * MLA_PAD)
    return qs, qs, slice(hh * V_AUG, (hh + 1) * V_AUG), slice(hh * MLA_V, (hh + 1) * MLA_V)


def _gqa_attn(q, k, v_aug, with_ctx):
    qw, kw = GQA_HEADS * HEAD_DIM, GQA_KV * HEAD_DIM
    return _attention("gqa_attn", q, k, v_aug, 0, qw, kw, GQA_KV * V_AUG, qw, GQA_HEADS, _gqa_head_cols, with_ctx)


def _mla_attn(q, k, v_aug, with_ctx):
    qw = MLA_HEADS * MLA_PAD
    return _attention("mla_attn", q, k, v_aug, 0, qw, qw, MLA_HEADS * V_AUG, MLA_HEADS * MLA_V, MLA_HEADS,
                      _mla_head_cols, with_ctx)


def _mla_prep_kernel(cq_ref, ckv_ref, kr_ref, qg_ref, kvg_ref, wuq_ref, wukv_ref,
                     cos_ref, su_ref, sd_ref, q_ref, k_ref, v_ref):
    is_ctx = _tile_is_ctx(pl.program_id(0))
    cos, su, sd = cos_ref[...], su_ref[...], sd_ref[...]
    quarter = MLA_ROPE // 4
    scale = MLA_QK ** -0.5 * LOG2E

    cq = _head_norm(cq_ref[...].astype(F32), qg_ref[...]).astype(BF16)
    qf = jnp.dot(cq, wuq_ref[...], preferred_element_type=F32)
    ckv = _head_norm(ckv_ref[...].astype(F32), kvg_ref[...]).astype(BF16)
    kvf = jnp.dot(ckv, wukv_ref[...], preferred_element_type=F32)
    kr = kr_ref[...]
    kr = jnp.where(is_ctx, kr, _rope(kr, cos, su, sd, quarter)).astype(BF16)
    for hh in range(MLA_HEADS):
        base = hh * MLA_PAD
        q_ref[:, base:base + MLA_NOPE] = (qf[:, base:base + MLA_NOPE] * scale).astype(BF16)
        qr = qf[:, base + MLA_NOPE:base + MLA_PAD]
        qr = jnp.where(is_ctx, qr, _rope(qr, cos, su, sd, quarter))
        q_ref[:, base + MLA_NOPE:base + MLA_PAD] = (qr * scale).astype(BF16)
        k_ref[:, base:base + MLA_NOPE] = kvf[:, base:base + MLA_NOPE].astype(BF16)
        k_ref[:, base + MLA_NOPE:base + MLA_PAD] = kr
        v_ref[:, hh * V_AUG:hh * V_AUG + MLA_V] = kvf[:, base + MLA_NOPE:base + MLA_PAD].astype(BF16)
        v_ref[:, hh * V_AUG + MLA_V:(hh + 1) * V_AUG] = jnp.ones((TQ, V_AUG - MLA_V), BF16)


def _mla_prep(pb, pkr, q_gain, kv_gain, w_uq_pad, w_ukv, tabs):
    tab_spec = pl.BlockSpec((TQ, LANES), lambda t: (_rope_idx(t), 0))
    qk_w = MLA_HEADS * MLA_PAD
    return pl.pallas_call(
        _mla_prep_kernel,
        grid=(M // TQ,),
        in_specs=[
            pl.BlockSpec((TQ, MLA_Q_RANK), lambda t: (t, (OFF_CQ - OFF_HG) // MLA_Q_RANK)),
            pl.BlockSpec((TQ, MLA_KV_RANK), lambda t: (t, (OFF_CKV - OFF_HG) // MLA_KV_RANK)),
            pl.BlockSpec((TQ, LANES), lambda t: (t, 0)),
            pl.BlockSpec((1, MLA_Q_RANK), lambda t: (0, 0)),
            pl.BlockSpec((1, MLA_KV_RANK), lambda t: (0, 0)),
            pl.BlockSpec((MLA_Q_RANK, qk_w), lambda t: (0, 0)),
            pl.BlockSpec((MLA_KV_RANK, qk_w), lambda t: (0, 0)),
            tab_spec, tab_spec, tab_spec,
        ],
        out_specs=[pl.BlockSpec((TQ, qk_w), lambda t: (t, 0)),
                   pl.BlockSpec((TQ, qk_w), lambda t: (t, 0)),
                   pl.BlockSpec((TQ, MLA_HEADS * V_AUG), lambda t: (t, 0))],
        out_shape=[jax.ShapeDtypeStruct((M, qk_w), BF16),
                   jax.ShapeDtypeStruct((M, qk_w), BF16),
                   jax.ShapeDtypeStruct((M, MLA_HEADS * V_AUG), BF16)],
        compiler_params=_cparams(("parallel",), 32),
        name="mla_prep",
    )(pb, pb, pkr, q_gain.reshape(1, -1), kv_gain.reshape(1, -1), w_uq_pad, w_ukv, *tabs)


TILE_CHUNKS = TQ // HCHUNK
DIAG = 8
LEVELS = (8, 16, 32)


def _split3(x):
    hi = x.astype(BF16)
    r1 = x - hi.astype(F32)
    mid = r1.astype(BF16)
    lo = (r1 - mid.astype(F32)).astype(BF16)
    return hi, mid, lo


def _hgrn_kernel(q_ref, v_ref, z_ref, lg_ref, o_ref, *scratch, reverse, layer):
    cum_refs = scratch[0:HG_HEADS]
    k_refs = scratch[HG_HEADS:2 * HG_HEADS]
    s_refs = scratch[2 * HG_HEADS:]

    lg = [lg_ref[l:l + 1, :] for l in range(DEPTH)]
    mx = functools.reduce(jnp.maximum, lg)
    ex = [jnp.exp(r - mx) for r in lg]
    den = functools.reduce(lambda a_, b_: a_ + b_, ex)
    lb_all = jnp.zeros((1, HG_W), F32)
    for l in range(1, layer + 1):
        lb_all = lb_all + ex[l] / den
    log_lb_all = jnp.log(lb_all)
    log_1m_all = jnp.log1p(-lb_all)

    ti = lax.broadcasted_iota(jnp.int32, (HCHUNK, HCHUNK), 0)
    si = lax.broadcasted_iota(jnp.int32, (HCHUNK, HCHUNK), 1)
    causal = (si >= ti) if reverse else (si <= ti)
    tri = causal.astype(BF16)
    diag_mask = (causal & ((ti // DIAG) == (si // DIAG))).astype(F32)
    ri = lax.broadcasted_iota(jnp.int32, (DIAG * HG_DK, LANES), 0)
    ci = lax.broadcasted_iota(jnp.int32, (DIAG * HG_DK, LANES), 1)
    sel = ((ri // HG_DK) == (ci % DIAG)).astype(BF16)
    edge = 0 if reverse else HCHUNK - 1

    @pl.when(pl.program_id(1) == 0)
    def _():
        for s_ref in s_refs:
            s_ref[...] = jnp.zeros_like(s_ref)

    heads = range(HG_HEADS)

    def lanes(hh):
        return slice(hh * HG_DK, (hh + 1) * HG_DK)

    def chunk_all_heads(r0):
        rows = pl.ds(r0, HCHUNK)
        q = [q_ref[rows, lanes(hh)].astype(F32) * (HG_DK ** -0.5) for hh in heads]
        v = [v_ref[rows, lanes(hh)] for hh in heads]

        z = z_ref[rows, :]
        u = jnp.exp(-jnp.abs(z))
        w = 1.0 + u
        b_ = log_1m_all + (jnp.minimum(z, 0.0) - jnp.log(w))
        log_f = jnp.maximum(log_lb_all, b_) + jnp.log(1.0 + jnp.exp(-jnp.abs(log_lb_all - b_)))
        k_all = (1.0 - lb_all) * (jnp.where(z >= 0.0, u, 1.0) / w)
        k = [k_all[:, lanes(hh)] for hh in heads]

        cum3 = jnp.dot(tri, jnp.concatenate(_split3(log_f), axis=1), preferred_element_type=F32)
        cum_all = (cum3[:, :HG_W] + cum3[:, HG_W:2 * HG_W] + cum3[:, 2 * HG_W:]) * LOG2E
        cum = [cum_all[:, lanes(hh)] for hh in heads]
        for hh in heads:
            cum_refs[hh][...] = cum[hh]
            k_refs[hh][...] = k[hh]
        tot = [cum_refs[hh][edge:edge + 1, :] for hh in heads]

        pair_lhs = []
        for hh in heads:
            cols = []
            for s in range(DIAG):
                cs = jnp.concatenate(
                    [jnp.broadcast_to(cum_refs[hh][blk * DIAG + s:blk * DIAG + s + 1, :], (DIAG, HG_DK))
                     for blk in range(HCHUNK // DIAG)], axis=0)
                ks = jnp.concatenate(
                    [jnp.broadcast_to(k_refs[hh][blk * DIAG + s:blk * DIAG + s + 1, :], (DIAG, HG_DK))
                     for blk in range(HCHUNK // DIAG)], axis=0)
                cols.append((q[hh] * jnp.exp2(jnp.minimum(cum[hh] - cs, 0.0)) * ks).astype(BF16))
            pair_lhs.append(jnp.concatenate(cols, axis=1))

        span_lhs, span_rhs = [], []
        for hh in heads:
            lhs, rhs = [], []
            for hs in LEVELS:
                for blk in range(HCHUNK // (2 * hs)):
                    lo, mid, hi = blk * 2 * hs, blk * 2 * hs + hs, (blk + 1) * 2 * hs
                    k_rows, q_rows = ((mid, hi), (lo, mid)) if reverse else ((lo, mid), (mid, hi))
                    ref_row = mid if reverse else mid - 1
                    rr = cum_refs[hh][ref_row:ref_row + 1, :]
                    qs = q[hh][q_rows[0]:q_rows[1], :] * jnp.exp2(cum[hh][q_rows[0]:q_rows[1], :] - rr)
                    ks = k[hh][k_rows[0]:k_rows[1], :] * jnp.exp2(rr - cum[hh][k_rows[0]:k_rows[1], :])
                    zq = [jnp.zeros((q_rows[0], HG_DK), F32), qs, jnp.zeros((HCHUNK - q_rows[1], HG_DK), F32)]
                    zk = [jnp.zeros((k_rows[0], HG_DK), F32), ks, jnp.zeros((HCHUNK - k_rows[1], HG_DK), F32)]
                    lhs.append(jnp.concatenate([p_ for p_ in zq if p_.shape[0]], axis=0))
                    rhs.append(jnp.concatenate([p_ for p_ in zk if p_.shape[0]], axis=0))
            span_lhs.append(jnp.concatenate(lhs, axis=1).astype(BF16))
            span_rhs.append(jnp.concatenate(rhs, axis=1).astype(BF16))

        state = [s_refs[hh][...] for hh in heads]
        qe = [(q[hh] * jnp.exp2(cum[hh])).astype(BF16) for hh in heads]
        ke = [(k[hh] * jnp.exp2(tot[hh] - cum[hh])).astype(BF16) for hh in heads]

        pair = [jnp.dot(pair_lhs[hh], sel, preferred_element_type=F32) for hh in heads]
        a_off = [_nt_dot(span_lhs[hh], span_rhs[hh]) for hh in heads]
        inter = [_nt_dot(qe[hh], state[hh].astype(BF16)) for hh in heads]
        upd = [lax.dot_general(v[hh], ke[hh], (((0,), (0,)), ((), ())), preferred_element_type=F32)
               for hh in heads]
        a = [(a_off[hh] + pair[hh][:, :HCHUNK] * diag_mask).astype(BF16) for hh in heads]
        o = [jnp.dot(a[hh], v[hh], preferred_element_type=F32) for hh in heads]
        for hh in heads:
            o_ref[rows, lanes(hh)] = o[hh] + inter[hh]
            s_refs[hh][...] = state[hh] * jnp.exp2(tot[hh]) + upd[hh]

    def chunk(jstep, carry):
        c = TILE_CHUNKS - 1 - jstep if reverse else jstep
        chunk_all_heads(pl.multiple_of(c * HCHUNK, HCHUNK))
        return carry

    lax.fori_loop(0, TILE_CHUNKS, chunk, 0, unroll=2)


def _hgrn(pa, pz, lb_logits_dir, layer, reverse):
    def tile_row(b, j):
        lat = b * QT_SEQ + (QT_SEQ - j if reverse else j - 1)
        return jnp.where(j == 0, LAT // TQ + b, lat)

    def spec(col):
        return pl.BlockSpec((TQ, HG_W), lambda b, j: (tile_row(b, j), col))

    return pl.pallas_call(
        functools.partial(_hgrn_kernel, reverse=reverse, layer=layer),
        grid=(BATCH, QT_SEQ + 1),
        in_specs=[spec(OFF_HQ // HG_W), spec(OFF_HI // HG_W), spec(1 if reverse else 0),
                  pl.BlockSpec((DEPTH, HG_W), lambda b, j: (0, 0))],
        out_specs=spec(0),
        out_shape=jax.ShapeDtypeStruct((M, HG_W), F32),
        scratch_shapes=([pltpu.VMEM((HCHUNK, HG_DK), F32)] * (2 * HG_HEADS)
                        + [pltpu.VMEM((HG_DK, HG_DK), F32)] * HG_HEADS),
        compiler_params=_cparams(("parallel", "arbitrary"), 32),
        name="hgrn_bwd" if reverse else "hgrn_fwd",
    )(pa, pa, pz, lb_logits_dir)


def _outproj_kernel(x_ref, gt_ref, oa_ref, of_ref, ob_ref, hg_ref, om_ref, ng_ref, w_ref, o_ref, lhs_ref):
    a_w = GQA_HEADS * HEAD_DIM
    lhs_ref[:, 0:a_w] = oa_ref[...]
    lhs_ref[:, a_w + HG_W:] = om_ref[...]
    gain = ng_ref[...]
    for hh in range(HG_HEADS):
        sl = slice(hh * HG_DK, (hh + 1) * HG_DK)
        g = hg_ref[:, sl].astype(F32)
        y = _head_norm(of_ref[:, sl] + ob_ref[:, sl], gain) * (g * _sigmoid(g))
        lhs_ref[:, a_w + hh * HG_DK:a_w + (hh + 1) * HG_DK] = y.astype(BF16)

    lhs = lhs_ref[...]
    gate = _mod_vec(gt_ref, pl.program_id(0), TMP)
    for n in range(D // PROJ_TN):
        cols = slice(n * PROJ_TN, (n + 1) * PROJ_TN)
        y = jnp.dot(lhs, w_ref[n], preferred_element_type=F32)
        o_ref[:, cols] = x_ref[:, cols] + gate[:, cols] * y


def _outproj(x, mods, o_a, o_f, o_b, pb, o_m, norm_gain, w_tiles, layer, rows):
    def row_spec(w, col=0):
        return pl.BlockSpec((TMP, w), lambda i: (i, col))

    return pl.pallas_call(
        _outproj_kernel,
        grid=(rows // TMP,),
        in_specs=[
            row_spec(D),
            pl.BlockSpec((None, SUBLANES, D), lambda i: (layer, 0, 5)),
            row_spec(GQA_HEADS * HEAD_DIM), row_spec(HG_W), row_spec(HG_W),
            row_spec(HG_W, 0),
            row_spec(MLA_HEADS * MLA_V),
            pl.BlockSpec((1, HG_DK), lambda i: (0, 0)),
            pl.BlockSpec((None, D // PROJ_TN, D, PROJ_TN), lambda i: (layer, 0, 0, 0),
                         pipeline_mode=pl.Buffered(1)),
        ],
        out_specs=row_spec(D),
        out_shape=jax.ShapeDtypeStruct((rows, D), F32),
        scratch_shapes=[pltpu.VMEM((TMP, D), BF16)],
        compiler_params=_cparams(("parallel",), 56),
        name="outproj",
    )(x, mods, o_a, o_f, o_b, pb, o_m, norm_gain.reshape(1, HG_DK), w_tiles)


def _pad_w_uq(w_uq):
    w = w_uq.reshape(DEPTH, MLA_Q_RANK, MLA_HEADS, MLA_QK)
    w = jnp.pad(w, ((0, 0), (0, 0), (0, 0), (0, MLA_PAD - MLA_QK)))
    return w.reshape(DEPTH, MLA_Q_RANK, MLA_HEADS * MLA_PAD).astype(BF16)


def kernel(x, c, ctx, c_ctx, w_mod, b_mod, w_ffn1_in, w_ffn1_out, w_in, w_uq, w_ukv, w_out,
           gqa_q_gain, gqa_k_gain, mla_q_gain, mla_kv_gain, hgrn_lb_logits, hgrn_norm_gain,
           w_ffn2_in, w_ffn2_out, final_gain):
    cvec8 = jnp.concatenate([c, c_ctx[None, :], jnp.zeros((SUBLANES - BATCH - 1, D), F32)], axis=0)
    mods = _modulation(cvec8, w_mod, b_mod)
    tabs_h = _rope_tables(HEAD_DIM)
    tabs_r = _rope_tables(MLA_ROPE)
    f1_in, f1_out = _ffn_weights(w_ffn1_in, w_ffn1_out)
    f2_in, f2_out = _ffn_weights(w_ffn2_in, w_ffn2_out)
    w_in_rows = jnp.transpose(w_in, (0, 2, 1))
    w_in_t = _inproj_weight_tiles(w_in_rows)
    w_out_t = _proj_tiles(w_out, D)
    w_kr = jnp.pad(w_in_rows[:, OFF_KR:, :], ((0, 0), (0, LANES - MLA_ROPE), (0, 0))).astype(BF16)
    w_uq_pad = _pad_w_uq(w_uq)
    w_ukv_b = w_ukv.astype(BF16)

    xa = None
    for l in range(DEPTH):
        last = l == DEPTH - 1
        if l == 0:
            xa = _ffn(x.reshape(LAT, D), mods, f1_in, f1_out, l, 0, ctx=ctx.reshape(CTXR, D))
        else:
            xa = _ffn(xa, mods, f1_in, f1_out, l, 0)
        pa, pz, pb, pkr = _inproj(xa, mods, w_in_t, w_kr, l)
        q_a, k_a, v_a = _gqa_prep(pa, gqa_q_gain[l], gqa_k_gain[l], tabs_h)
        o_a = _gqa_attn(q_a, k_a, v_a, not last)
        q_m, k_m, v_m = _mla_prep(pb, pkr, mla_q_gain[l], mla_kv_gain[l], w_uq_pad[l], w_ukv_b[l], tabs_r)
        o_m = _mla_attn(q_m, k_m, v_m, not last)
        o_f = _hgrn(pa, pz, hgrn_lb_logits[0], l, False)
        o_b = _hgrn(pa, pz, hgrn_lb_logits[1], l, True)
        xa = _outproj(xa, mods, o_a, o_f, o_b, pb, o_m, hgrn_norm_gain[l], w_out_t, l, LAT if last else M)
        xa = _ffn(xa, mods, f2_in, f2_out, l, 6, final_gain=final_gain if last else None)
    return xa
```

```python
import functools

import jax
import jax.numpy as jnp
from jax import lax
from jax.experimental import pallas as pl
from jax.experimental.pallas import tpu as pltpu

F32 = jnp.float32
BF16 = jnp.bfloat16

D = 2048
BATCH = 4
SEQ = 2048
DEPTH = 2
GRID_W = 64
CTX = 256
EPS = 1e-6
ROPE_THETA = 10000.0

HEAD_DIM = 128
GQA_HEADS = 8
GQA_KV = 2
GQA_GROUP = GQA_HEADS // GQA_KV
HG_HEADS = 4
HG_DK = 128
HG_W = HG_HEADS * HG_DK
MLA_HEADS = 4
MLA_Q_RANK = 512
MLA_KV_RANK = 256
MLA_NOPE = 128
MLA_ROPE = 64
MLA_V = 128
MLA_QK = MLA_NOPE + MLA_ROPE
MLA_PAD = 256
FFN_H = 5504
N_MOD = 9

LAT = BATCH * SEQ
CTXR = BATCH * CTX
M = LAT + CTXR
OFF_GQ, OFF_GK, OFF_GV = 0, 1024, 1280
OFF_HQ, OFF_HI, OFF_HF, OFF_HB, OFF_HG = 1536, 2048, 2560, 3072, 3584
OFF_CQ, OFF_CKV, OFF_KR = 4096, 4608, 4864
IN_MAIN = OFF_KR
PA_W, PZ_W, PB_W = OFF_HF, OFF_HG - OFF_HF, OFF_KR - OFF_HG

LANES = 128
SUBLANES = 8
MXU_W = 256

TM = 1024
TMP = 512
ROWC = 64
TQ = 256
QT_SEQ = SEQ // TQ
HCHUNK = 64
FFN_TH = 512
FFN_HP = -(-FFN_H // FFN_TH) * FFN_TH
FFN_STEPS = FFN_HP // FFN_TH
MOD_TN = 1024
PROJ_TN = MXU_W
PREP_ROWS = 128
V_AUG = MXU_W
LOG2E = 1.4426950408889634


def _cparams(sem, vmem_mb):
    return pltpu.CompilerParams(dimension_semantics=sem, vmem_limit_bytes=vmem_mb * 1024 * 1024)


def _sigmoid(x):
    return 1.0 / (1.0 + jnp.exp(-x))


def _nt_dot(a, b):
    return lax.dot_general(a, b, (((1,), (1,)), ((), ())), preferred_element_type=F32)


def _mod_kernel(c_ref, w_ref, b_ref, o_ref):
    c = c_ref[...]
    a = (c * _sigmoid(c)).astype(BF16)
    o_ref[...] = jnp.dot(a, w_ref[...].astype(BF16), preferred_element_type=F32) + b_ref[...]


def _modulation(cvec8, w_mod, b_mod):
    n = N_MOD * D
    return pl.pallas_call(
        _mod_kernel,
        grid=(DEPTH, n // MOD_TN),
        in_specs=[
            pl.BlockSpec((SUBLANES, D), lambda l, j: (0, 0)),
            pl.BlockSpec((None, D, MOD_TN), lambda l, j: (l, 0, j)),
            pl.BlockSpec((None, 1, MOD_TN), lambda l, j: (l, 0, j)),
        ],
        out_specs=pl.BlockSpec((None, SUBLANES, MOD_TN), lambda l, j: (l, 0, j)),
        out_shape=jax.ShapeDtypeStruct((DEPTH, SUBLANES, n), F32),
        compiler_params=_cparams(("parallel", "parallel"), 40),
        name="modulation",
    )(cvec8, w_mod, b_mod.reshape(DEPTH, 1, n))


def _mod_vec(ref, tile, tm):
    row = jnp.where(tile < LAT // tm, tile // (SEQ // tm), BATCH)
    return ref[pl.ds(row, 1), :]


def _modulate_tile(x_ref, shift_ref, scale_ref, h_ref, tile, tm):
    sh = _mod_vec(shift_ref, tile, tm)
    sc = 1.0 + _mod_vec(scale_ref, tile, tm)

    def body(c, carry):
        r0 = pl.multiple_of(c * ROWC, ROWC)
        xc = x_ref[pl.ds(r0, ROWC), :]
        ms = jnp.mean(xc * xc, axis=-1, keepdims=True)
        h_ref[pl.ds(r0, ROWC), :] = (xc * lax.rsqrt(ms + EPS) * sc + sh).astype(BF16)
        return carry

    lax.fori_loop(0, tm // ROWC, body, 0)


def _ffn_kernel(*refs, two_inputs, final):
    refs = list(refs)
    x_ref = refs.pop(0)
    c_ref = refs.pop(0) if two_inputs else None
    sh_ref, sc_ref, gt_ref = refs.pop(0), refs.pop(0), refs.pop(0)
    fg_ref = refs.pop(0) if final else None
    wi_ref, wo_ref, o_ref, h_ref = refs
    i = pl.program_id(0)
    j = pl.program_id(1)

    def per_source(fn):
        if two_inputs:
            pl.when(i < LAT // TM)(lambda: fn(x_ref))
            pl.when(i >= LAT // TM)(lambda: fn(c_ref))
        else:
            fn(x_ref)

    @pl.when(j == 0)
    def _():
        per_source(lambda src: _modulate_tile(src, sh_ref, sc_ref, h_ref, i, TM))
        o_ref[...] = jnp.zeros_like(o_ref)

    pass_rows = TM // 2 if two_inputs else TM
    for rc in range(TM // pass_rows):
        rows = slice(rc * pass_rows, (rc + 1) * pass_rows)
        r = jnp.dot(h_ref[rows, :], wi_ref[...], preferred_element_type=F32)
        g = r[:, :FFN_TH]
        u = r[:, FFN_TH:]
        a = (g * _sigmoid(g) * u).astype(BF16)
        o_ref[rows, :] += jnp.dot(a, wo_ref[...], preferred_element_type=F32)

    @pl.when(j == FFN_STEPS - 1)
    def _():
        gate = 0.5 * _mod_vec(gt_ref, i, TM)

        def epilogue(src):
            def body(c, carry):
                r0 = pl.multiple_of(c * ROWC, ROWC)
                y = src[pl.ds(r0, ROWC), :] + gate * o_ref[pl.ds(r0, ROWC), :]
                if final:
                    y = y * lax.rsqrt(jnp.mean(y * y, axis=-1, keepdims=True) + EPS) * fg_ref[...]
                o_ref[pl.ds(r0, ROWC), :] = y
                return carry

            lax.fori_loop(0, TM // ROWC, body, 0)

        per_source(epilogue)


def _ffn(x, mods, wi, wo, layer, mod0, ctx=None, final_gain=None):
    two_inputs, final = ctx is not None, final_gain is not None
    rows = M if two_inputs else x.shape[0]
    lat_tiles = LAT // TM

    def mod_spec(k):
        return pl.BlockSpec((None, SUBLANES, D), lambda i, j: (layer, 0, mod0 + k))

    in_specs = [pl.BlockSpec((TM, D), lambda i, j: (jnp.minimum(i, lat_tiles - 1) if two_inputs else i, 0),
                             pipeline_mode=pl.Buffered(1))]
    args = [x]
    if two_inputs:
        in_specs.append(pl.BlockSpec((TM, D), lambda i, j: (0, 0), pipeline_mode=pl.Buffered(1)))
        args.append(ctx)
    in_specs += [mod_spec(0), mod_spec(1), mod_spec(2)]
    args += [mods, mods, mods]
    if final:
        in_specs.append(pl.BlockSpec((1, D), lambda i, j: (0, 0)))
        args.append(final_gain.reshape(1, D))
        out_spec = pl.BlockSpec((None, TM, D), lambda i, j: (i // (SEQ // TM), i % (SEQ // TM), 0))
        out_shape = jax.ShapeDtypeStruct((BATCH, SEQ, D), F32)
    else:
        out_spec = pl.BlockSpec((TM, D), lambda i, j: (i, 0))
        out_shape = jax.ShapeDtypeStruct((rows, D), F32)
    in_specs += [pl.BlockSpec((None, None, D, 2 * FFN_TH), lambda i, j: (layer, j, 0, 0)),
                 pl.BlockSpec((None, None, FFN_TH, D), lambda i, j: (layer, j, 0, 0))]
    args += [wi, wo]
    return pl.pallas_call(
        functools.partial(_ffn_kernel, two_inputs=two_inputs, final=final),
        grid=(rows // TM, FFN_STEPS),
        in_specs=in_specs,
        out_specs=out_spec,
        out_shape=out_shape,
        scratch_shapes=[pltpu.VMEM((TM, D), BF16)],
        compiler_params=_cparams(("parallel", "arbitrary"), 58),
        name="ffn",
    )(*args)


def _ffn_win_tiles_kernel(w_ref, o_ref):
    for j in range(FFN_STEPS):
        valid = min(FFN_TH, FFN_H - j * FFN_TH)
        for part, base in ((0, 0), (1, FFN_H)):
            c0 = part * FFN_TH
            o_ref[j, :, c0:c0 + valid] = w_ref[:, base + j * FFN_TH:base + j * FFN_TH + valid].astype(BF16)
            if valid < FFN_TH:
                o_ref[j, :, c0 + valid:c0 + FFN_TH] = jnp.zeros((PREP_ROWS, FFN_TH - valid), BF16)


def _ffn_wout_tiles_kernel(w_ref, *rest):
    tails, o_ref = rest[:-1], rest[-1]
    last = pl.program_id(1) == FFN_STEPS - 1

    @pl.when(jnp.logical_not(last))
    def _():
        o_ref[...] = w_ref[...].astype(BF16)

    @pl.when(last)
    def _():
        for r, t_ref in enumerate(tails):
            o_ref[r * PREP_ROWS:(r + 1) * PREP_ROWS, :] = t_ref[...].astype(BF16)
        n = len(tails) * PREP_ROWS
        o_ref[n:, :] = jnp.zeros((FFN_TH - n, D), BF16)


def _ffn_weights(w_in, w_out):
    wi = pl.pallas_call(
        _ffn_win_tiles_kernel,
        grid=(DEPTH, D // PREP_ROWS),
        in_specs=[pl.BlockSpec((None, PREP_ROWS, 2 * FFN_H), lambda l, r: (l, r, 0))],
        out_specs=pl.BlockSpec((None, FFN_STEPS, PREP_ROWS, 2 * FFN_TH), lambda l, r: (l, 0, r, 0)),
        out_shape=jax.ShapeDtypeStruct((DEPTH, FFN_STEPS, D, 2 * FFN_TH), BF16),
        compiler_params=_cparams(("parallel", "parallel"), 32),
        name="ffn_win_tiles",
    )(w_in)
    full_tiles = FFN_H // FFN_TH
    n_tail = (FFN_H - full_tiles * FFN_TH) // PREP_ROWS
    slab0 = full_tiles * FFN_TH // PREP_ROWS
    tail_specs = [pl.BlockSpec((None, PREP_ROWS, D), functools.partial(lambda l, j, r: (l, slab0 + r, 0), r=r))
                  for r in range(n_tail)]
    wo = pl.pallas_call(
        _ffn_wout_tiles_kernel,
        grid=(DEPTH, FFN_STEPS),
        in_specs=[pl.BlockSpec((None, FFN_TH, D), lambda l, j: (l, jnp.minimum(j, full_tiles - 1), 0))] + tail_specs,
        out_specs=pl.BlockSpec((None, None, FFN_TH, D), lambda l, j: (l, j, 0, 0)),
        out_shape=jax.ShapeDtypeStruct((DEPTH, FFN_STEPS, FFN_TH, D), BF16),
        compiler_params=_cparams(("parallel", "arbitrary"), 32),
        name="ffn_wout_tiles",
    )(w_out, *([w_out] * n_tail))
    return wi, wo


def _proj_tiles_kernel(w_ref, o_ref):
    for t in range(o_ref.shape[0]):
        o_ref[t] = w_ref[:, t * PROJ_TN:(t + 1) * PROJ_TN].astype(BF16)


def _proj_tiles(w, n):
    k = w.shape[1]
    return pl.pallas_call(
        _proj_tiles_kernel,
        grid=(DEPTH, k // PREP_ROWS),
        in_specs=[pl.BlockSpec((None, PREP_ROWS, n), lambda l, r: (l, r, 0))],
        out_specs=pl.BlockSpec((None, n // PROJ_TN, PREP_ROWS, PROJ_TN), lambda l, r: (l, 0, r, 0)),
        out_shape=jax.ShapeDtypeStruct((DEPTH, n // PROJ_TN, k, PROJ_TN), BF16),
        compiler_params=_cparams(("parallel", "parallel"), 32),
        name="proj_tiles",
    )(w)


def _cast_rows_kernel(w_ref, o_ref):
    o_ref[...] = w_ref[...].astype(BF16)


def _inproj_weight_tiles(w_in_t):
    n_tiles = IN_MAIN // PROJ_TN
    return pl.pallas_call(
        _cast_rows_kernel,
        grid=(DEPTH, n_tiles),
        in_specs=[pl.BlockSpec((None, PROJ_TN, D), lambda l, t: (l, t, 0))],
        out_specs=pl.BlockSpec((None, None, PROJ_TN, D), lambda l, t: (l, t, 0, 0)),
        out_shape=jax.ShapeDtypeStruct((DEPTH, n_tiles, PROJ_TN, D), BF16),
        compiler_params=_cparams(("parallel", "parallel"), 32),
        name="inproj_weight_tiles",
    )(w_in_t)


def _inproj_kernel(x_ref, sh_ref, sc_ref, w_ref, wkr_ref, pa_ref, pz_ref, pb_ref, kr_ref, h_ref):
    _modulate_tile(x_ref, sh_ref, sc_ref, h_ref, pl.program_id(0), TMP)
    h = h_ref[...]
    kr_ref[...] = _nt_dot(h, wkr_ref[...])
    for t in range(IN_MAIN // PROJ_TN):
        y = _nt_dot(h, w_ref[t])
        c0 = t * PROJ_TN
        if c0 < OFF_HF:
            pa_ref[:, c0:c0 + PROJ_TN] = y.astype(BF16)
        elif c0 < OFF_HG:
            pz_ref[:, c0 - OFF_HF:c0 - OFF_HF + PROJ_TN] = y
        else:
            pb_ref[:, c0 - OFF_HG:c0 - OFF_HG + PROJ_TN] = y.astype(BF16)


def _inproj(x, mods, w_tiles, w_kr, layer):
    def mod_spec(k):
        return pl.BlockSpec((None, SUBLANES, D), lambda i: (layer, 0, k))

    def out_spec(w):
        return pl.BlockSpec((TMP, w), lambda i: (i, 0))

    n_tiles = IN_MAIN // PROJ_TN
    return pl.pallas_call(
        _inproj_kernel,
        grid=(M // TMP,),
        in_specs=[
            pl.BlockSpec((TMP, D), lambda i: (i, 0)),
            mod_spec(3), mod_spec(4),
            pl.BlockSpec((None, n_tiles, PROJ_TN, D), lambda i: (layer, 0, 0, 0), pipeline_mode=pl.Buffered(1)),
            pl.BlockSpec((None, LANES, D), lambda i: (layer, 0, 0)),
        ],
        out_specs=[out_spec(PA_W), out_spec(PZ_W), out_spec(PB_W), out_spec(LANES)],
        out_shape=[jax.ShapeDtypeStruct((M, PA_W), BF16), jax.ShapeDtypeStruct((M, PZ_W), F32),
                   jax.ShapeDtypeStruct((M, PB_W), BF16), jax.ShapeDtypeStruct((M, LANES), F32)],
        scratch_shapes=[pltpu.VMEM((TMP, D), BF16)],
        compiler_params=_cparams(("parallel",), 56),
        name="inproj",
    )(x, mods, mods, w_tiles, w_kr)


def _rope_tables(rot_dim):
    rows = SEQ // GRID_W
    row = jnp.repeat(jnp.arange(rows, dtype=F32), GRID_W)
    colp = jnp.tile(jnp.arange(GRID_W, dtype=F32), rows)
    axis_dim = rot_dim // 2
    inv_freq = ROPE_THETA ** (-jnp.arange(0, axis_dim, 2, dtype=F32) / axis_dim)
    ang_r = row[:, None] * inv_freq
    ang_c = colp[:, None] * inv_freq
    ang = jnp.concatenate([ang_r, ang_r, ang_c, ang_c], axis=-1)
    cos, sin = jnp.cos(ang), jnp.sin(ang)
    quarter = rot_dim // 4
    lane = jnp.arange(rot_dim)
    first = (lane % (2 * quarter)) < quarter
    sin_up = jnp.where(first, -sin, 0.0)
    sin_dn = jnp.where(first, 0.0, sin)
    pad = LANES - rot_dim
    if pad:
        cos = jnp.pad(cos, ((0, 0), (0, pad)), constant_values=1.0)
        sin_up = jnp.pad(sin_up, ((0, 0), (0, pad)))
        sin_dn = jnp.pad(sin_dn, ((0, 0), (0, pad)))
    return cos, sin_up, sin_dn


def _rope(x, cos, sin_up, sin_dn, quarter):
    up = pltpu.roll(x, LANES - quarter, 1)
    dn = pltpu.roll(x, quarter, 1)
    return x * cos + up * sin_up + dn * sin_dn


def _head_norm(x, gain):
    return x * lax.rsqrt(jnp.mean(x * x, axis=-1, keepdims=True) + EPS) * gain


def _lane_shift_matrix(quarter):
    j = lax.broadcasted_iota(jnp.int32, (LANES, 2 * LANES), 0)
    i = lax.broadcasted_iota(jnp.int32, (LANES, 2 * LANES), 1)
    source = jnp.where(i < LANES, i + quarter, i - LANES - quarter)
    return jnp.where(j == source, 1.0, 0.0).astype(BF16)


def _rope_mxu(x, shifts, cos, sin_up, sin_dn):
    r = jnp.dot(x.astype(BF16), shifts, preferred_element_type=F32)
    return x * cos + r[:, :LANES] * sin_up + r[:, LANES:] * sin_dn


def _head_norm_mxu(x, gain):
    mean_w = jnp.full((LANES, LANES), 1.0 / LANES, BF16)
    ms = jnp.dot((x * x).astype(BF16), mean_w, preferred_element_type=F32)
    return x * lax.rsqrt(ms + EPS) * gain


def _tile_is_ctx(t):
    return t >= LAT // TQ


def _rope_idx(t):
    return jnp.where(_tile_is_ctx(t), 0, t % QT_SEQ)


def _gqa_prep_kernel(p_ref, qg_ref, kg_ref, cos_ref, su_ref, sd_ref, q_ref, k_ref, v_ref):
    is_ctx = _tile_is_ctx(pl.program_id(0))
    cos, su, sd = cos_ref[...], su_ref[...], sd_ref[...]
    scale = HEAD_DIM ** -0.5 * LOG2E
    for kk in range(GQA_KV):
        v_ref[:, kk * V_AUG:kk * V_AUG + HEAD_DIM] = p_ref[:, OFF_GV + kk * HEAD_DIM:OFF_GV + (kk + 1) * HEAD_DIM]
        v_ref[:, kk * V_AUG + HEAD_DIM:(kk + 1) * V_AUG] = jnp.ones((TQ, V_AUG - HEAD_DIM), BF16)
    shifts = _lane_shift_matrix(HEAD_DIM // 4)
    for hh in range(GQA_HEADS + GQA_KV):
        xh = p_ref[:, hh * HEAD_DIM:(hh + 1) * HEAD_DIM].astype(F32)
        is_q = hh < GQA_HEADS
        n = _head_norm_mxu(xh, qg_ref[...] if is_q else kg_ref[...])
        y = jnp.where(is_ctx, n, _rope_mxu(n, shifts, cos, su, sd))
        if is_q:
            q_ref[:, hh * HEAD_DIM:(hh + 1) * HEAD_DIM] = (y * scale).astype(BF16)
        else:
            kk = hh - GQA_HEADS
            k_ref[:, kk * HEAD_DIM:(kk + 1) * HEAD_DIM] = y.astype(BF16)


def _gqa_prep(pa, q_gain, k_gain, tabs):
    width = OFF_HQ
    tab_spec = pl.BlockSpec((TQ, LANES), lambda t: (_rope_idx(t), 0))
    vec_spec = pl.BlockSpec((1, HEAD_DIM), lambda t: (0, 0))
    return pl.pallas_call(
        _gqa_prep_kernel,
        grid=(M // TQ,),
        in_specs=[pl.BlockSpec((TQ, width), lambda t: (t, 0)), vec_spec, vec_spec,
                  tab_spec, tab_spec, tab_spec],
        out_specs=[pl.BlockSpec((TQ, GQA_HEADS * HEAD_DIM), lambda t: (t, 0)),
                   pl.BlockSpec((TQ, GQA_KV * HEAD_DIM), lambda t: (t, 0)),
                   pl.BlockSpec((TQ, GQA_KV * V_AUG), lambda t: (t, 0))],
        out_shape=[jax.ShapeDtypeStruct((M, GQA_HEADS * HEAD_DIM), BF16),
                   jax.ShapeDtypeStruct((M, GQA_KV * HEAD_DIM), BF16),
                   jax.ShapeDtypeStruct((M, GQA_KV * V_AUG), BF16)],
        compiler_params=_cparams(("parallel",), 32),
        name="gqa_prep",
    )(pa, q_gain.reshape(1, HEAD_DIM), k_gain.reshape(1, HEAD_DIM), *tabs)


def _attn_kernel(q_ref, kl_ref, kc_ref, vl_ref, vc_ref, o_ref, *, n_heads, head_cols, with_ctx):
    def finish(pv, os_):
        e_w = os_.stop - os_.start
        o_ref[:, os_] = (pv[:, :e_w] / pv[:, e_w:2 * e_w]).astype(BF16)

    def latent_queries():
        for hh in range(n_heads):
            qs, ks, vs, os_ = head_cols(hh)
            q = q_ref[:, qs]
            s = jnp.concatenate([_nt_dot(q, kl_ref[:, ks]), _nt_dot(q, kc_ref[:, ks])], axis=1)
            eb = jnp.exp2(s - jnp.max(s, axis=-1, keepdims=True)).astype(BF16)
            finish(jnp.dot(eb[:, :SEQ], vl_ref[:, vs], preferred_element_type=F32)
                   + jnp.dot(eb[:, SEQ:], vc_ref[:, vs], preferred_element_type=F32), os_)

    def context_queries():
        for hh in range(n_heads):
            qs, ks, vs, os_ = head_cols(hh)
            s = _nt_dot(q_ref[:, qs], kc_ref[:, ks])
            eb = jnp.exp2(s - jnp.max(s, axis=-1, keepdims=True)).astype(BF16)
            finish(jnp.dot(eb, vc_ref[:, vs], preferred_element_type=F32), os_)

    if with_ctx:
        pl.when(pl.program_id(1) < QT_SEQ)(latent_queries)
        pl.when(pl.program_id(1) == QT_SEQ)(context_queries)
    else:
        latent_queries()


def _attention(name, q, k, v, v_col, qw, kw, vw, ow, n_heads, head_cols, with_ctx):
    def q_row(b, i):
        return jnp.where(i < QT_SEQ, b * QT_SEQ + i, LAT // TQ + b)

    def ctx_row(b, i):
        return LAT // CTX + b

    return pl.pallas_call(
        functools.partial(_attn_kernel, n_heads=n_heads, head_cols=head_cols, with_ctx=with_ctx),
        grid=(BATCH, QT_SEQ + (1 if with_ctx else 0)),
        in_specs=[
            pl.BlockSpec((TQ, qw), lambda b, i: (q_row(b, i), 0)),
            pl.BlockSpec((SEQ, kw), lambda b, i: (b, 0)),
            pl.BlockSpec((CTX, kw), lambda b, i: (ctx_row(b, i), 0)),
            pl.BlockSpec((SEQ, vw), lambda b, i: (b, v_col)),
            pl.BlockSpec((CTX, vw), lambda b, i: (ctx_row(b, i), v_col)),
        ],
        out_specs=pl.BlockSpec((TQ, ow), lambda b, i: (q_row(b, i), 0)),
        out_shape=jax.ShapeDtypeStruct((M if with_ctx else LAT, ow), BF16),
        compiler_params=_cparams(("parallel", "arbitrary"), 48),
        name=name,
    )(q, k, k, v, v)


def _gqa_head_cols(hh):
    kk = hh // GQA_GROUP
    kv = slice(kk * HEAD_DIM, (kk + 1) * HEAD_DIM)
    hs = slice(hh * HEAD_DIM, (hh + 1) * HEAD_DIM)
    return hs, kv, slice(kk * V_AUG, (kk + 1) * V_AUG), hs


def _mla_head_cols(hh):
    qs = slice(hh * MLA_PAD, (hh + 1) * MLA_PAD)
    return qs, qs, slice(hh * V_AUG, (hh + 1) * V_AUG), slice(hh * MLA_V, (hh + 1) * MLA_V)


def _gqa_attn(q, k, v_aug, with_ctx):
    qw, kw = GQA_HEADS * HEAD_DIM, GQA_KV * HEAD_DIM
    return _attention("gqa_attn", q, k, v_aug, 0, qw, kw, GQA_KV * V_AUG, qw, GQA_HEADS, _gqa_head_cols, with_ctx)


def _mla_attn(q, k, v_aug, with_ctx):
    qw = MLA_HEADS * MLA_PAD
    return _attention("mla_attn", q, k, v_aug, 0, qw, qw, MLA_HEADS * V_AUG, MLA_HEADS * MLA_V, MLA_HEADS,
                      _mla_head_cols, with_ctx)


def _mla_prep_kernel(cq_ref, ckv_ref, kr_ref, qg_ref, kvg_ref, wuq_ref, wukv_ref,
                     cos_ref, su_ref, sd_ref, q_ref, k_ref, v_ref):
    is_ctx = _tile_is_ctx(pl.program_id(0))
    cos, su, sd = cos_ref[...], su_ref[...], sd_ref[...]
    quarter = MLA_ROPE // 4
    scale = MLA_QK ** -0.5 * LOG2E

    cq = _head_norm(cq_ref[...].astype(F32), qg_ref[...]).astype(BF16)
    qf = jnp.dot(cq, wuq_ref[...], preferred_element_type=F32)
    ckv = _head_norm(ckv_ref[...].astype(F32), kvg_ref[...]).astype(BF16)
    kvf = jnp.dot(ckv, wukv_ref[...], preferred_element_type=F32)
    kr = kr_ref[...]
    kr = jnp.where(is_ctx, kr, _rope(kr, cos, su, sd, quarter)).astype(BF16)
    for hh in range(MLA_HEADS):
        base = hh * MLA_PAD
        q_ref[:, base:base + MLA_NOPE] = (qf[:, base:base + MLA_NOPE] * scale).astype(BF16)
        qr = qf[:, base + MLA_NOPE:base + MLA_PAD]
        qr = jnp.where(is_ctx, qr, _rope(qr, cos, su, sd, quarter))
        q_ref[:, base + MLA_NOPE:base + MLA_PAD] = (qr * scale).astype(BF16)
        k_ref[:, base:base + MLA_NOPE] = kvf[:, base:base + MLA_NOPE].astype(BF16)
        k_ref[:, base + MLA_NOPE:base + MLA_PAD] = kr
        v_ref[:, hh * V_AUG:hh * V_AUG + MLA_V] = kvf[:, base + MLA_NOPE:base + MLA_PAD].astype(BF16)
        v_ref[:, hh * V_AUG + MLA_V:(hh + 1) * V_AUG] = jnp.ones((TQ, V_AUG - MLA_V), BF16)


def _mla_prep(pb, pkr, q_gain, kv_gain, w_uq_pad, w_ukv, tabs):
    tab_spec = pl.BlockSpec((TQ, LANES), lambda t: (_rope_idx(t), 0))
    qk_w = MLA_HEADS * MLA_PAD
    return pl.pallas_call(
        _mla_prep_kernel,
        grid=(M // TQ,),
        in_specs=[
            pl.BlockSpec((TQ, MLA_Q_RANK), lambda t: (t, (OFF_CQ - OFF_HG) // MLA_Q_RANK)),
            pl.BlockSpec((TQ, MLA_KV_RANK), lambda t: (t, (OFF_CKV - OFF_HG) // MLA_KV_RANK)),
            pl.BlockSpec((TQ, LANES), lambda t: (t, 0)),
            pl.BlockSpec((1, MLA_Q_RANK), lambda t: (0, 0)),
            pl.BlockSpec((1, MLA_KV_RANK), lambda t: (0, 0)),
            pl.BlockSpec((MLA_Q_RANK, qk_w), lambda t: (0, 0)),
            pl.BlockSpec((MLA_KV_RANK, qk_w), lambda t: (0, 0)),
            tab_spec, tab_spec, tab_spec,
        ],
        out_specs=[pl.BlockSpec((TQ, qk_w), lambda t: (t, 0)),
                   pl.BlockSpec((TQ, qk_w), lambda t: (t, 0)),
                   pl.BlockSpec((TQ, MLA_HEADS * V_AUG), lambda t: (t, 0))],
        out_shape=[jax.ShapeDtypeStruct((M, qk_w), BF16),
                   jax.ShapeDtypeStruct((M, qk_w), BF16),
                   jax.ShapeDtypeStruct((M, MLA_HEADS * V_AUG), BF16)],
        compiler_params=_cparams(("parallel",), 32),
        name="mla_prep",
    )(pb, pb, pkr, q_gain.reshape(1, -1), kv_gain.reshape(1, -1), w_uq_pad, w_ukv, *tabs)


TILE_CHUNKS = TQ // HCHUNK
DIAG = 8
LEVELS = (8, 16, 32)


def _split3(x):
    hi = x.astype(BF16)
    r1 = x - hi.astype(F32)
    mid = r1.astype(BF16)
    lo = (r1 - mid.astype(F32)).astype(BF16)
    return hi, mid, lo


def _hgrn_kernel(q_ref, v_ref, z_ref, lg_ref, o_ref, *scratch, reverse, layer):
    cum_refs = scratch[0:HG_HEADS]
    k_refs = scratch[HG_HEADS:2 * HG_HEADS]
    s_refs = scratch[2 * HG_HEADS:]

    lg = [lg_ref[l:l + 1, :] for l in range(DEPTH)]
    mx = functools.reduce(jnp.maximum, lg)
    ex = [jnp.exp(r - mx) for r in lg]
    den = functools.reduce(lambda a_, b_: a_ + b_, ex)
    lb_all = jnp.zeros((1, HG_W), F32)
    for l in range(1, layer + 1):
        lb_all = lb_all + ex[l] / den
    log_lb_all = jnp.log(lb_all)
    log_1m_all = jnp.log1p(-lb_all)

    ti = lax.broadcasted_iota(jnp.int32, (TQ, TQ), 0)
    si = lax.broadcasted_iota(jnp.int32, (TQ, TQ), 1)
    causal = (si >= ti) if reverse else (si <= ti)
    tri = (causal & ((ti // HCHUNK) == (si // HCHUNK))).astype(BF16)
    tc = lax.broadcasted_iota(jnp.int32, (HCHUNK, HCHUNK), 0)
    sc = lax.broadcasted_iota(jnp.int32, (HCHUNK, HCHUNK), 1)
    diag_mask = (((sc >= tc) if reverse else (sc <= tc)) & ((tc // DIAG) == (sc // DIAG))).astype(F32)
    ri = lax.broadcasted_iota(jnp.int32, (DIAG * HG_DK, LANES), 0)
    ci = lax.broadcasted_iota(jnp.int32, (DIAG * HG_DK, LANES), 1)
    sel = ((ri // HG_DK) == (ci % DIAG)).astype(BF16)
    edge = 0 if reverse else HCHUNK - 1

    @pl.when(pl.program_id(1) == 0)
    def _():
        for s_ref in s_refs:
            s_ref[...] = jnp.zeros_like(s_ref)

    heads = range(HG_HEADS)

    def lanes(hh):
        return slice(hh * HG_DK, (hh + 1) * HG_DK)

    chunks = range(TILE_CHUNKS)

    def crow(c):
        return slice(c * HCHUNK, (c + 1) * HCHUNK)

    q = [q_ref[:, lanes(hh)].astype(F32) * (HG_DK ** -0.5) for hh in heads]
    v = [v_ref[:, lanes(hh)] for hh in heads]

    z = z_ref[...]
    u = jnp.exp(-jnp.abs(z))
    w = 1.0 + u
    b_ = log_1m_all + (jnp.minimum(z, 0.0) - jnp.log(w))
    log_f = jnp.maximum(log_lb_all, b_) + jnp.log(1.0 + jnp.exp(-jnp.abs(log_lb_all - b_)))
    k_all = (1.0 - lb_all) * (jnp.where(z >= 0.0, u, 1.0) / w)
    k = [k_all[:, lanes(hh)] for hh in heads]

    cum3 = jnp.dot(tri, jnp.concatenate(_split3(log_f), axis=1), preferred_element_type=F32)
    cum_all = (cum3[:, :HG_W] + cum3[:, HG_W:2 * HG_W] + cum3[:, 2 * HG_W:]) * LOG2E
    cum = [cum_all[:, lanes(hh)] for hh in heads]
    for hh in heads:
        cum_refs[hh][...] = cum[hh]
        k_refs[hh][...] = k[hh]
    tot = [[cum_refs[hh][c * HCHUNK + edge:c * HCHUNK + edge + 1, :] for c in chunks] for hh in heads]

    def bcast_rows(ref, s):
        return jnp.concatenate(
            [jnp.broadcast_to(ref[blk * DIAG + s:blk * DIAG + s + 1, :], (DIAG, HG_DK))
             for blk in range(TQ // DIAG)], axis=0)

    pair_lhs = []
    for hh in heads:
        cols = [(q[hh] * jnp.exp2(jnp.minimum(cum[hh] - bcast_rows(cum_refs[hh], s), 0.0))
                 * bcast_rows(k_refs[hh], s)).astype(BF16) for s in range(DIAG)]
        pair_lhs.append(jnp.concatenate(cols, axis=1))

    def span_operands(hh, c):
        base = c * HCHUNK
        lhs, rhs = [], []
        for hs in LEVELS:
            for blk in range(HCHUNK // (2 * hs)):
                lo, mid, hi = blk * 2 * hs, blk * 2 * hs + hs, (blk + 1) * 2 * hs
                k_rows, q_rows = ((mid, hi), (lo, mid)) if reverse else ((lo, mid), (mid, hi))
                ref_row = base + (mid if reverse else mid - 1)
                rr = cum_refs[hh][ref_row:ref_row + 1, :]
                qr = slice(base + q_rows[0], base + q_rows[1])
                kr = slice(base + k_rows[0], base + k_rows[1])
                qs = q[hh][qr, :] * jnp.exp2(cum[hh][qr, :] - rr)
                ks = k[hh][kr, :] * jnp.exp2(rr - cum[hh][kr, :])
                zq = [jnp.zeros((q_rows[0], HG_DK), F32), qs, jnp.zeros((HCHUNK - q_rows[1], HG_DK), F32)]
                zk = [jnp.zeros((k_rows[0], HG_DK), F32), ks, jnp.zeros((HCHUNK - k_rows[1], HG_DK), F32)]
                lhs.append(jnp.concatenate([p_ for p_ in zq if p_.shape[0]], axis=0))
                rhs.append(jnp.concatenate([p_ for p_ in zk if p_.shape[0]], axis=0))
        return jnp.concatenate(lhs, axis=1).astype(BF16), jnp.concatenate(rhs, axis=1).astype(BF16)

    spans = [[span_operands(hh, c) for c in chunks] for hh in heads]
    qe = [(q[hh] * jnp.exp2(cum[hh])).astype(BF16) for hh in heads]
    ke = [[(k[hh][crow(c), :] * jnp.exp2(tot[hh][c] - cum[hh][crow(c), :])).astype(BF16) for c in chunks]
          for hh in heads]

    pair = [jnp.dot(pair_lhs[hh], sel, preferred_element_type=F32) for hh in heads]
    a_off = [[_nt_dot(*spans[hh][c]) for c in chunks] for hh in heads]
    upd = [[lax.dot_general(v[hh][crow(c), :], ke[hh][c], (((0,), (0,)), ((), ())),
                            preferred_element_type=F32) for c in chunks] for hh in heads]
    a = [[(a_off[hh][c] + pair[hh][crow(c), :HCHUNK] * diag_mask).astype(BF16) for c in chunks] for hh in heads]
    o_intra = [[jnp.dot(a[hh][c], v[hh][crow(c), :], preferred_element_type=F32) for c in chunks] for hh in heads]

    state = [s_refs[hh][...] for hh in heads]
    for c in (reversed(chunks) if reverse else chunks):
        inter = [_nt_dot(qe[hh][crow(c), :], state[hh].astype(BF16)) for hh in heads]
        for hh in heads:
            o_ref[crow(c), lanes(hh)] = o_intra[hh][c] + inter[hh]
            state[hh] = state[hh] * jnp.exp2(tot[hh][c]) + upd[hh][c]
    for hh in heads:
        s_refs[hh][...] = state[hh]


def _hgrn(pa, pz, lb_logits_dir, layer, reverse):
    def tile_row(b, j):
        lat = b * QT_SEQ + (QT_SEQ - j if reverse else j - 1)
        return jnp.where(j == 0, LAT // TQ + b, lat)

    def spec(col):
        return pl.BlockSpec((TQ, HG_W), lambda b, j: (tile_row(b, j), col))

    return pl.pallas_call(
        functools.partial(_hgrn_kernel, reverse=reverse, layer=layer),
        grid=(BATCH, QT_SEQ + 1),
        in_specs=[spec(OFF_HQ // HG_W), spec(OFF_HI // HG_W), spec(1 if reverse else 0),
                  pl.BlockSpec((DEPTH, HG_W), lambda b, j: (0, 0))],
        out_specs=spec(0),
        out_shape=jax.ShapeDtypeStruct((M, HG_W), F32),
        scratch_shapes=([pltpu.VMEM((TQ, HG_DK), F32)] * (2 * HG_HEADS)
                        + [pltpu.VMEM((HG_DK, HG_DK), F32)] * HG_HEADS),
        compiler_params=_cparams(("parallel", "arbitrary"), 32),
        name="hgrn_bwd" if reverse else "hgrn_fwd",
    )(pa, pa, pz, lb_logits_dir)


def _outproj_kernel(x_ref, gt_ref, oa_ref, of_ref, ob_ref, hg_ref, om_ref, ng_ref, w_ref, o_ref, lhs_ref):
    a_w = GQA_HEADS * HEAD_DIM
    lhs_ref[:, 0:a_w] = oa_ref[...]
    lhs_ref[:, a_w + HG_W:] = om_ref[...]
    gain = ng_ref[...]
    for hh in range(HG_HEADS):
        sl = slice(hh * HG_DK, (hh + 1) * HG_DK)
        g = hg_ref[:, sl].astype(F32)
        y = _head_norm(of_ref[:, sl] + ob_ref[:, sl], gain) * (g * _sigmoid(g))
        lhs_ref[:, a_w + hh * HG_DK:a_w + (hh + 1) * HG_DK] = y.astype(BF16)

    lhs = lhs_ref[...]
    gate = _mod_vec(gt_ref, pl.program_id(0), TMP)
    for n in range(D // PROJ_TN):
        cols = slice(n * PROJ_TN, (n + 1) * PROJ_TN)
        y = jnp.dot(lhs, w_ref[n], preferred_element_type=F32)
        o_ref[:, cols] = x_ref[:, cols] + gate[:, cols] * y


def _outproj(x, mods, o_a, o_f, o_b, pb, o_m, norm_gain, w_tiles, layer, rows):
    def row_spec(w, col=0):
        return pl.BlockSpec((TMP, w), lambda i: (i, col))

    return pl.pallas_call(
        _outproj_kernel,
        grid=(rows // TMP,),
        in_specs=[
            row_spec(D),
            pl.BlockSpec((None, SUBLANES, D), lambda i: (layer, 0, 5)),
            row_spec(GQA_HEADS * HEAD_DIM), row_spec(HG_W), row_spec(HG_W),
            row_spec(HG_W, 0),
            row_spec(MLA_HEADS * MLA_V),
            pl.BlockSpec((1, HG_DK), lambda i: (0, 0)),
            pl.BlockSpec((None, D // PROJ_TN, D, PROJ_TN), lambda i: (layer, 0, 0, 0),
                         pipeline_mode=pl.Buffered(1)),
        ],
        out_specs=row_spec(D),
        out_shape=jax.ShapeDtypeStruct((rows, D), F32),
        scratch_shapes=[pltpu.VMEM((TMP, D), BF16)],
        compiler_params=_cparams(("parallel",), 56),
        name="outproj",
    )(x, mods, o_a, o_f, o_b, pb, o_m, norm_gain.reshape(1, HG_DK), w_tiles)


def _pad_w_uq(w_uq):
    w = w_uq.reshape(DEPTH, MLA_Q_RANK, MLA_HEADS, MLA_QK)
    w = jnp.pad(w, ((0, 0), (0, 0), (0, 0), (0, MLA_PAD - MLA_QK)))
    return w.reshape(DEPTH, MLA_Q_RANK, MLA_HEADS * MLA_PAD).astype(BF16)


def kernel(x, c, ctx, c_ctx, w_mod, b_mod, w_ffn1_in, w_ffn1_out, w_in, w_uq, w_ukv, w_out,
           gqa_q_gain, gqa_k_gain, mla_q_gain, mla_kv_gain, hgrn_lb_logits, hgrn_norm_gain,
           w_ffn2_in, w_ffn2_out, final_gain):
    cvec8 = jnp.concatenate([c, c_ctx[None, :], jnp.zeros((SUBLANES - BATCH - 1, D), F32)], axis=0)
    mods = _modulation(cvec8, w_mod, b_mod)
    tabs_h = _rope_tables(HEAD_DIM)
    tabs_r = _rope_tables(MLA_ROPE)
    f1_in, f1_out = _ffn_weights(w_ffn1_in, w_ffn1_out)
    f2_in, f2_out = _ffn_weights(w_ffn2_in, w_ffn2_out)
    w_in_rows = jnp.transpose(w_in, (0, 2, 1))
    w_in_t = _inproj_weight_tiles(w_in_rows)
    w_out_t = _proj_tiles(w_out, D)
    w_kr = jnp.pad(w_in_rows[:, OFF_KR:, :], ((0, 0), (0, LANES - MLA_ROPE), (0, 0))).astype(BF16)
    w_uq_pad = _pad_w_uq(w_uq)
    w_ukv_b = w_ukv.astype(BF16)

    xa = None
    for l in range(DEPTH):
        last = l == DEPTH - 1
        if l == 0:
            xa = _ffn(x.reshape(LAT, D), mods, f1_in, f1_out, l, 0, ctx=ctx.reshape(CTXR, D))
        else:
            xa = _ffn(xa, mods, f1_in, f1_out, l, 0)
        pa, pz, pb, pkr = _inproj(xa, mods, w_in_t, w_kr, l)
        q_a, k_a, v_a = _gqa_prep(pa, gqa_q_gain[l], gqa_k_gain[l], tabs_h)
        o_a = _gqa_attn(q_a, k_a, v_a, not last)
        q_m, k_m, v_m = _mla_prep(pb, pkr, mla_q_gain[l], mla_kv_gain[l], w_uq_pad[l], w_ukv_b[l], tabs_r)
        o_m = _mla_attn(q_m, k_m, v_m, not last)
        o_f = _hgrn(pa, pz, hgrn_lb_logits[0], l, False)
        o_b = _hgrn(pa, pz, hgrn_lb_logits[1], l, True)
        xa = _outproj(xa, mods, o_a, o_f, o_b, pb, o_m, hgrn_norm_gain[l], w_out_t, l, LAT if last else M)
        xa = _ffn(xa, mods, f2_in, f2_out, l, 6, final_gain=final_gain if last else None)
    return xa
```

```python
import functools

import jax
import jax.numpy as jnp
from jax import lax
from jax.experimental import pallas as pl
from jax.experimental.pallas import tpu as pltpu

F32 = jnp.float32
BF16 = jnp.bfloat16

D = 2048
BATCH = 4
SEQ = 2048
DEPTH = 2
GRID_W = 64
CTX = 256
EPS = 1e-6
ROPE_THETA = 10000.0

HEAD_DIM = 128
GQA_HEADS = 8
GQA_KV = 2
GQA_GROUP = GQA_HEADS // GQA_KV
HG_HEADS = 4
HG_DK = 128
HG_W = HG_HEADS * HG_DK
MLA_HEADS = 4
MLA_Q_RANK = 512
MLA_KV_RANK = 256
MLA_NOPE = 128
MLA_ROPE = 64
MLA_V = 128
MLA_QK = MLA_NOPE + MLA_ROPE
MLA_PAD = 256
FFN_H = 5504
N_MOD = 9

LAT = BATCH * SEQ
CTXR = BATCH * CTX
M = LAT + CTXR
OFF_GQ, OFF_GK, OFF_GV = 0, 1024, 1280
OFF_HQ, OFF_HI, OFF_HF, OFF_HB, OFF_HG = 1536, 2048, 2560, 3072, 3584
OFF_CQ, OFF_CKV, OFF_KR = 4096, 4608, 4864
IN_MAIN = OFF_KR
PA_W, PZ_W, PB_W = OFF_HF, OFF_HG - OFF_HF, OFF_KR - OFF_HG

LANES = 128
SUBLANES = 8
MXU_W = 256

TM = 1024
TMP = 512
ROWC = 64
TQ = 256
TP = 512
QT_SEQ = SEQ // TQ
HCHUNK = 64
FFN_TH = 512
FFN_HP = -(-FFN_H // FFN_TH) * FFN_TH
FFN_STEPS = FFN_HP // FFN_TH
MOD_TN = 1024
PROJ_TN = MXU_W
PREP_ROWS = 128
WO_SLAB = 128
V_AUG = MXU_W
LOG2E = 1.4426950408889634


def _cparams(sem, vmem_mb):
    return pltpu.CompilerParams(dimension_semantics=sem, vmem_limit_bytes=vmem_mb * 1024 * 1024)


def _sigmoid(x):
    return 1.0 / (1.0 + jnp.exp(-x))


def _nt_dot(a, b):
    return lax.dot_general(a, b, (((1,), (1,)), ((), ())), preferred_element_type=F32)


def _mod_kernel(c_ref, w_ref, b_ref, o_ref):
    c = c_ref[...]
    a = (c * _sigmoid(c)).astype(BF16)
    o_ref[...] = jnp.dot(a, w_ref[...].astype(BF16), preferred_element_type=F32) + b_ref[...]


def _modulation(cvec8, w_mod, b_mod):
    n = N_MOD * D
    return pl.pallas_call(
        _mod_kernel,
        grid=(DEPTH, n // MOD_TN),
        in_specs=[
            pl.BlockSpec((SUBLANES, D), lambda l, j: (0, 0)),
            pl.BlockSpec((None, D, MOD_TN), lambda l, j: (l, 0, j)),
            pl.BlockSpec((None, 1, MOD_TN), lambda l, j: (l, 0, j)),
        ],
        out_specs=pl.BlockSpec((None, SUBLANES, MOD_TN), lambda l, j: (l, 0, j)),
        out_shape=jax.ShapeDtypeStruct((DEPTH, SUBLANES, n), F32),
        compiler_params=_cparams(("parallel", "parallel"), 40),
        name="modulation",
    )(cvec8, w_mod, b_mod.reshape(DEPTH, 1, n))


def _mod_vec(ref, tile, tm):
    row = jnp.where(tile < LAT // tm, tile // (SEQ // tm), BATCH)
    return ref[pl.ds(row, 1), :]


def _modulate_tile(x_ref, shift_ref, scale_ref, h_ref, tile, tm):
    sh = _mod_vec(shift_ref, tile, tm)
    sc = 1.0 + _mod_vec(scale_ref, tile, tm)

    def body(c, carry):
        r0 = pl.multiple_of(c * ROWC, ROWC)
        xc = x_ref[pl.ds(r0, ROWC), :]
        ms = jnp.mean(xc * xc, axis=-1, keepdims=True)
        h_ref[pl.ds(r0, ROWC), :] = (xc * lax.rsqrt(ms + EPS) * sc + sh).astype(BF16)
        return carry

    lax.fori_loop(0, tm // ROWC, body, 0, unroll=2)


def _ffn_kernel(*refs, two_inputs, final):
    refs = list(refs)
    x_ref = refs.pop(0)
    c_ref = refs.pop(0) if two_inputs else None
    sh_ref, sc_ref, gt_ref = refs.pop(0), refs.pop(0), refs.pop(0)
    fg_ref = refs.pop(0) if final else None
    wi_ref = refs.pop(0)
    wo_refs = [refs.pop(0) for _ in range(FFN_TH // WO_SLAB)]
    o_ref, h_ref = refs
    i = pl.program_id(0)
    j = pl.program_id(1)

    def per_source(fn):
        if two_inputs:
            pl.when(i < LAT // TM)(lambda: fn(x_ref))
            pl.when(i >= LAT // TM)(lambda: fn(c_ref))
        else:
            fn(x_ref)

    @pl.when(j == 0)
    def _():
        per_source(lambda src: _modulate_tile(src, sh_ref, sc_ref, h_ref, i, TM))
        o_ref[...] = jnp.zeros_like(o_ref)

    pass_rows = TM // 2 if two_inputs else TM
    wo = jnp.concatenate([r_[...].astype(BF16) for r_ in wo_refs], axis=0)
    for rc in range(TM // pass_rows):
        rows = slice(rc * pass_rows, (rc + 1) * pass_rows)
        r = jnp.dot(h_ref[rows, :], wi_ref[...], preferred_element_type=F32)
        g = r[:, :FFN_TH]
        u = r[:, FFN_TH:]
        a = (g * _sigmoid(g) * u).astype(BF16)
        o_ref[rows, :] += jnp.dot(a, wo, preferred_element_type=F32)

    @pl.when(j == FFN_STEPS - 1)
    def _():
        gate = 0.5 * _mod_vec(gt_ref, i, TM)

        def epilogue(src):
            def body(c, carry):
                r0 = pl.multiple_of(c * ROWC, ROWC)
                y = src[pl.ds(r0, ROWC), :] + gate * o_ref[pl.ds(r0, ROWC), :]
                if final:
                    y = y * lax.rsqrt(jnp.mean(y * y, axis=-1, keepdims=True) + EPS) * fg_ref[...]
                o_ref[pl.ds(r0, ROWC), :] = y
                return carry

            lax.fori_loop(0, TM // ROWC, body, 0)

        per_source(epilogue)


def _ffn(x, mods, wi, wo, layer, mod0, ctx=None, final_gain=None):
    two_inputs, final = ctx is not None, final_gain is not None
    rows = M if two_inputs else x.shape[0]
    lat_tiles = LAT // TM

    def mod_spec(k):
        return pl.BlockSpec((None, SUBLANES, D), lambda i, j: (layer, 0, mod0 + k))

    in_specs = [pl.BlockSpec((TM, D), lambda i, j: (jnp.minimum(i, lat_tiles - 1) if two_inputs else i, 0),
                             pipeline_mode=pl.Buffered(1))]
    args = [x]
    if two_inputs:
        in_specs.append(pl.BlockSpec((TM, D), lambda i, j: (0, 0), pipeline_mode=pl.Buffered(1)))
        args.append(ctx)
    in_specs += [mod_spec(0), mod_spec(1), mod_spec(2)]
    args += [mods, mods, mods]
    if final:
        in_specs.append(pl.BlockSpec((1, D), lambda i, j: (0, 0)))
        args.append(final_gain.reshape(1, D))
        out_spec = pl.BlockSpec((None, TM, D), lambda i, j: (i // (SEQ // TM), i % (SEQ // TM), 0))
        out_shape = jax.ShapeDtypeStruct((BATCH, SEQ, D), F32)
    else:
        out_spec = pl.BlockSpec((TM, D), lambda i, j: (i, 0))
        out_shape = jax.ShapeDtypeStruct((rows, D), F32)
    in_specs.append(pl.BlockSpec((None, None, D, 2 * FFN_TH), lambda i, j: (layer, j, 0, 0)))
    args.append(wi)
    slabs, last_slab = FFN_TH // WO_SLAB, FFN_H // WO_SLAB - 1
    for r in range(slabs):
        in_specs.append(pl.BlockSpec(
            (None, WO_SLAB, D),
            functools.partial(lambda i, j, r: (layer, jnp.minimum(j * slabs + r, last_slab), 0), r=r)))
        args.append(wo)
    return pl.pallas_call(
        functools.partial(_ffn_kernel, two_inputs=two_inputs, final=final),
        grid=(rows // TM, FFN_STEPS),
        in_specs=in_specs,
        out_specs=out_spec,
        out_shape=out_shape,
        scratch_shapes=[pltpu.VMEM((TM, D), BF16)],
        compiler_params=_cparams(("parallel", "arbitrary"), 58),
        name="ffn",
    )(*args)


def _ffn_win_tiles_kernel(w_ref, o_ref):
    for j in range(FFN_STEPS):
        valid = min(FFN_TH, FFN_H - j * FFN_TH)
        for part, base in ((0, 0), (1, FFN_H)):
            c0 = part * FFN_TH
            o_ref[j, :, c0:c0 + valid] = w_ref[:, base + j * FFN_TH:base + j * FFN_TH + valid].astype(BF16)
            if valid < FFN_TH:
                o_ref[j, :, c0 + valid:c0 + FFN_TH] = jnp.zeros((PREP_ROWS, FFN_TH - valid), BF16)


def _ffn_win_tiles(w_in):
    return pl.pallas_call(
        _ffn_win_tiles_kernel,
        grid=(DEPTH, D // PREP_ROWS),
        in_specs=[pl.BlockSpec((None, PREP_ROWS, 2 * FFN_H), lambda l, r: (l, r, 0))],
        out_specs=pl.BlockSpec((None, FFN_STEPS, PREP_ROWS, 2 * FFN_TH), lambda l, r: (l, 0, r, 0)),
        out_shape=jax.ShapeDtypeStruct((DEPTH, FFN_STEPS, D, 2 * FFN_TH), BF16),
        compiler_params=_cparams(("parallel", "parallel"), 32),
        name="ffn_win_tiles",
    )(w_in)


def _proj_tiles_kernel(w_ref, o_ref):
    for t in range(o_ref.shape[0]):
        o_ref[t] = w_ref[:, t * PROJ_TN:(t + 1) * PROJ_TN].astype(BF16)


def _proj_tiles(w, n):
    k = w.shape[1]
    return pl.pallas_call(
        _proj_tiles_kernel,
        grid=(DEPTH, k // PREP_ROWS),
        in_specs=[pl.BlockSpec((None, PREP_ROWS, n), lambda l, r: (l, r, 0))],
        out_specs=pl.BlockSpec((None, n // PROJ_TN, PREP_ROWS, PROJ_TN), lambda l, r: (l, 0, r, 0)),
        out_shape=jax.ShapeDtypeStruct((DEPTH, n // PROJ_TN, k, PROJ_TN), BF16),
        compiler_params=_cparams(("parallel", "parallel"), 32),
        name="proj_tiles",
    )(w)


def _cast_rows_kernel(w_ref, o_ref):
    o_ref[...] = w_ref[...].astype(BF16)


def _inproj_weight_tiles(w_in_t):
    n_tiles = IN_MAIN // PROJ_TN
    return pl.pallas_call(
        _cast_rows_kernel,
        grid=(DEPTH, n_tiles),
        in_specs=[pl.BlockSpec((None, PROJ_TN, D), lambda l, t: (l, t, 0))],
        out_specs=pl.BlockSpec((None, None, PROJ_TN, D), lambda l, t: (l, t, 0, 0)),
        out_shape=jax.ShapeDtypeStruct((DEPTH, n_tiles, PROJ_TN, D), BF16),
        compiler_params=_cparams(("parallel", "parallel"), 32),
        name="inproj_weight_tiles",
    )(w_in_t)


def _inproj_kernel(x_ref, sh_ref, sc_ref, w_ref, wkr_ref, pa_ref, pz_ref, pb_ref, kr_ref, h_ref):
    _modulate_tile(x_ref, sh_ref, sc_ref, h_ref, pl.program_id(0), TMP)
    h = h_ref[...]
    kr_ref[...] = _nt_dot(h, wkr_ref[...])
    for t in range(IN_MAIN // PROJ_TN):
        y = _nt_dot(h, w_ref[t])
        c0 = t * PROJ_TN
        if c0 < OFF_HF:
            pa_ref[:, c0:c0 + PROJ_TN] = y.astype(BF16)
        elif c0 < OFF_HG:
            pz_ref[:, c0 - OFF_HF:c0 - OFF_HF + PROJ_TN] = y
        else:
            pb_ref[:, c0 - OFF_HG:c0 - OFF_HG + PROJ_TN] = y.astype(BF16)


def _inproj(x, mods, w_tiles, w_kr, layer):
    def mod_spec(k):
        return pl.BlockSpec((None, SUBLANES, D), lambda i: (layer, 0, k))

    def out_spec(w):
        return pl.BlockSpec((TMP, w), lambda i: (i, 0))

    n_tiles = IN_MAIN // PROJ_TN
    return pl.pallas_call(
        _inproj_kernel,
        grid=(M // TMP,),
        in_specs=[
            pl.BlockSpec((TMP, D), lambda i: (i, 0)),
            mod_spec(3), mod_spec(4),
            pl.BlockSpec((None, n_tiles, PROJ_TN, D), lambda i: (layer, 0, 0, 0), pipeline_mode=pl.Buffered(1)),
            pl.BlockSpec((None, LANES, D), lambda i: (layer, 0, 0)),
        ],
        out_specs=[out_spec(PA_W), out_spec(PZ_W), out_spec(PB_W), out_spec(LANES)],
        out_shape=[jax.ShapeDtypeStruct((M, PA_W), BF16), jax.ShapeDtypeStruct((M, PZ_W), F32),
                   jax.ShapeDtypeStruct((M, PB_W), BF16), jax.ShapeDtypeStruct((M, LANES), F32)],
        scratch_shapes=[pltpu.VMEM((TMP, D), BF16)],
        compiler_params=_cparams(("parallel",), 56),
        name="inproj",
    )(x, mods, mods, w_tiles, w_kr)


def _rope_tables(rot_dim):
    rows = SEQ // GRID_W
    row = jnp.repeat(jnp.arange(rows, dtype=F32), GRID_W)
    colp = jnp.tile(jnp.arange(GRID_W, dtype=F32), rows)
    axis_dim = rot_dim // 2
    inv_freq = ROPE_THETA ** (-jnp.arange(0, axis_dim, 2, dtype=F32) / axis_dim)
    ang_r = row[:, None] * inv_freq
    ang_c = colp[:, None] * inv_freq
    ang = jnp.concatenate([ang_r, ang_r, ang_c, ang_c], axis=-1)
    cos, sin = jnp.cos(ang), jnp.sin(ang)
    quarter = rot_dim // 4
    lane = jnp.arange(rot_dim)
    first = (lane % (2 * quarter)) < quarter
    sin_up = jnp.where(first, -sin, 0.0)
    sin_dn = jnp.where(first, 0.0, sin)
    pad = LANES - rot_dim
    if pad:
        cos = jnp.pad(cos, ((0, 0), (0, pad)), constant_values=1.0)
        sin_up = jnp.pad(sin_up, ((0, 0), (0, pad)))
        sin_dn = jnp.pad(sin_dn, ((0, 0), (0, pad)))
    return cos, sin_up, sin_dn


def _head_norm(x, gain):
    return x * lax.rsqrt(jnp.mean(x * x, axis=-1, keepdims=True) + EPS) * gain


def _lane_shift_matrix(quarter):
    j = lax.broadcasted_iota(jnp.int32, (LANES, 2 * LANES), 0)
    i = lax.broadcasted_iota(jnp.int32, (LANES, 2 * LANES), 1)
    source = jnp.where(i < LANES, i + quarter, i - LANES - quarter)
    return jnp.where(j == source, 1.0, 0.0).astype(BF16)


def _rope_mxu(x, shifts, cos, sin_up, sin_dn):
    r = jnp.dot(x.astype(BF16), shifts, preferred_element_type=F32)
    return x * cos + r[:, :LANES] * sin_up + r[:, LANES:] * sin_dn


def _head_norm_mxu(x, gain):
    mean_w = jnp.full((LANES, LANES), 1.0 / LANES, BF16)
    ms = jnp.dot((x * x).astype(BF16), mean_w, preferred_element_type=F32)
    return x * lax.rsqrt(ms + EPS) * gain


def _tile_is_ctx(t):
    return t >= LAT // TP


def _rope_idx(t):
    return jnp.where(_tile_is_ctx(t), 0, t % (SEQ // TP))


def _gqa_prep_kernel(p_ref, qg_ref, kg_ref, cos_ref, su_ref, sd_ref, q_ref, k_ref, v_ref):
    is_ctx = _tile_is_ctx(pl.program_id(0))
    cos, su, sd = cos_ref[...], su_ref[...], sd_ref[...]
    scale = HEAD_DIM ** -0.5 * LOG2E
    for kk in range(GQA_KV):
        v_ref[:, kk * V_AUG:kk * V_AUG + HEAD_DIM] = p_ref[:, OFF_GV + kk * HEAD_DIM:OFF_GV + (kk + 1) * HEAD_DIM]
        v_ref[:, kk * V_AUG + HEAD_DIM:(kk + 1) * V_AUG] = jnp.ones((TP, V_AUG - HEAD_DIM), BF16)
    shifts = _lane_shift_matrix(HEAD_DIM // 4)
    for hh in range(GQA_HEADS + GQA_KV):
        xh = p_ref[:, hh * HEAD_DIM:(hh + 1) * HEAD_DIM].astype(F32)
        is_q = hh < GQA_HEADS
        n = _head_norm_mxu(xh, qg_ref[...] if is_q else kg_ref[...])
        y = jnp.where(is_ctx, n, _rope_mxu(n, shifts, cos, su, sd))
        if is_q:
            q_ref[:, hh * HEAD_DIM:(hh + 1) * HEAD_DIM] = (y * scale).astype(BF16)
        else:
            kk = hh - GQA_HEADS
            k_ref[:, kk * HEAD_DIM:(kk + 1) * HEAD_DIM] = y.astype(BF16)


def _gqa_prep(pa, q_gain, k_gain, tabs):
    width = OFF_HQ
    tab_spec = pl.BlockSpec((TP, LANES), lambda t: (_rope_idx(t), 0))
    vec_spec = pl.BlockSpec((1, HEAD_DIM), lambda t: (0, 0))
    return pl.pallas_call(
        _gqa_prep_kernel,
        grid=(M // TP,),
        in_specs=[pl.BlockSpec((TP, width), lambda t: (t, 0)), vec_spec, vec_spec,
                  tab_spec, tab_spec, tab_spec],
        out_specs=[pl.BlockSpec((TP, GQA_HEADS * HEAD_DIM), lambda t: (t, 0)),
                   pl.BlockSpec((TP, GQA_KV * HEAD_DIM), lambda t: (t, 0)),
                   pl.BlockSpec((TP, GQA_KV * V_AUG), lambda t: (t, 0))],
        out_shape=[jax.ShapeDtypeStruct((M, GQA_HEADS * HEAD_DIM), BF16),
                   jax.ShapeDtypeStruct((M, GQA_KV * HEAD_DIM), BF16),
                   jax.ShapeDtypeStruct((M, GQA_KV * V_AUG), BF16)],
        compiler_params=_cparams(("parallel",), 32),
        name="gqa_prep",
    )(pa, q_gain.reshape(1, HEAD_DIM), k_gain.reshape(1, HEAD_DIM), *tabs)


def _attn_kernel(q_ref, kl_ref, kc_ref, vl_ref, vc_ref, o_ref, *, n_heads, head_cols, with_ctx):
    def finish(pv, os_):
        e_w = os_.stop - os_.start
        o_ref[:, os_] = (pv[:, :e_w] / pv[:, e_w:2 * e_w]).astype(BF16)

    def latent_queries():
        for hh in range(n_heads):
            qs, ks, vs, os_ = head_cols(hh)
            q = q_ref[:, qs]
            s = jnp.concatenate([_nt_dot(q, kl_ref[:, ks]), _nt_dot(q, kc_ref[:, ks])], axis=1)
            eb = jnp.exp2(s - jnp.max(s, axis=-1, keepdims=True)).astype(BF16)
            finish(jnp.dot(eb[:, :SEQ], vl_ref[:, vs], preferred_element_type=F32)
                   + jnp.dot(eb[:, SEQ:], vc_ref[:, vs], preferred_element_type=F32), os_)

    def context_queries():
        for hh in range(n_heads):
            qs, ks, vs, os_ = head_cols(hh)
            s = _nt_dot(q_ref[:, qs], kc_ref[:, ks])
            eb = jnp.exp2(s - jnp.max(s, axis=-1, keepdims=True)).astype(BF16)
            finish(jnp.dot(eb, vc_ref[:, vs], preferred_element_type=F32), os_)

    if with_ctx:
        pl.when(pl.program_id(1) < QT_SEQ)(latent_queries)
        pl.when(pl.program_id(1) == QT_SEQ)(context_queries)
    else:
        latent_queries()


def _attention(name, q, k, v, v_col, qw, kw, vw, ow, n_heads, head_cols, with_ctx):
    def q_row(b, i):
        return jnp.where(i < QT_SEQ, b * QT_SEQ + i, LAT // TQ + b)

    def ctx_row(b, i):
        return LAT // CTX + b

    return pl.pallas_call(
        functools.partial(_attn_kernel, n_heads=n_heads, head_cols=head_cols, with_ctx=with_ctx),
        grid=(BATCH, QT_SEQ + (1 if with_ctx else 0)),
        in_specs=[
            pl.BlockSpec((TQ, qw), lambda b, i: (q_row(b, i), 0)),
            pl.BlockSpec((SEQ, kw), lambda b, i: (b, 0)),
            pl.BlockSpec((CTX, kw), lambda b, i: (ctx_row(b, i), 0)),
            pl.BlockSpec((SEQ, vw), lambda b, i: (b, v_col)),
            pl.BlockSpec((CTX, vw), lambda b, i: (ctx_row(b, i), v_col)),
        ],
        out_specs=pl.BlockSpec((TQ, ow), lambda b, i: (q_row(b, i), 0)),
        out_shape=jax.ShapeDtypeStruct((M if with_ctx else LAT, ow), BF16),
        compiler_params=_cparams(("parallel", "arbitrary"), 48),
        name=name,
    )(q, k, k, v, v)


def _gqa_head_cols(hh):
    kk = hh // GQA_GROUP
    kv = slice(kk * HEAD_DIM, (kk + 1) * HEAD_DIM)
    hs = slice(hh * HEAD_DIM, (hh + 1) * HEAD_DIM)
    return hs, kv, slice(kk * V_AUG, (kk + 1) * V_AUG), hs


def _mla_head_cols(hh):
    qs = slice(hh * MLA_PAD, (hh + 1) * MLA_PAD)
    return qs, qs, slice(hh * V_AUG, (hh + 1) * V_AUG), slice(hh * MLA_V, (hh + 1) * MLA_V)


def _gqa_attn(q, k, v_aug, with_ctx):
    qw, kw = GQA_HEADS * HEAD_DIM, GQA_KV * HEAD_DIM
    return _attention("gqa_attn", q, k, v_aug, 0, qw, kw, GQA_KV * V_AUG, qw, GQA_HEADS, _gqa_head_cols, with_ctx)


def _mla_attn(q, k, v_aug, with_ctx):
    qw = MLA_HEADS * MLA_PAD
    return _attention("mla_attn", q, k, v_aug, 0, qw, qw, MLA_HEADS * V_AUG, MLA_HEADS * MLA_V, MLA_HEADS,
                      _mla_head_cols, with_ctx)


def _mla_prep_kernel(cq_ref, ckv_ref, kr_ref, qg_ref, kvg_ref, wuq_ref, wukv_ref,
                     cos_ref, su_ref, sd_ref, q_ref, k_ref, v_ref):
    is_ctx = _tile_is_ctx(pl.program_id(0))
    cos, su, sd = cos_ref[...], su_ref[...], sd_ref[...]
    quarter = MLA_ROPE // 4
    scale = MLA_QK ** -0.5 * LOG2E

    cq = _head_norm(cq_ref[...].astype(F32), qg_ref[...]).astype(BF16)
    qf = jnp.dot(cq, wuq_ref[...], preferred_element_type=F32)
    ckv = _head_norm(ckv_ref[...].astype(F32), kvg_ref[...]).astype(BF16)
    kvf = jnp.dot(ckv, wukv_ref[...], preferred_element_type=F32)
    kr = kr_ref[...]
    shifts = _lane_shift_matrix(quarter)
    kr = jnp.where(is_ctx, kr, _rope_mxu(kr, shifts, cos, su, sd)).astype(BF16)
    for hh in range(MLA_HEADS):
        base = hh * MLA_PAD
        q_ref[:, base:base + MLA_NOPE] = (qf[:, base:base + MLA_NOPE] * scale).astype(BF16)
        qr = qf[:, base + MLA_NOPE:base + MLA_PAD]
        qr = jnp.where(is_ctx, qr, _rope_mxu(qr, shifts, cos, su, sd))
        q_ref[:, base + MLA_NOPE:base + MLA_PAD] = (qr * scale).astype(BF16)
        k_ref[:, base:base + MLA_NOPE] = kvf[:, base:base + MLA_NOPE].astype(BF16)
        k_ref[:, base + MLA_NOPE:base + MLA_PAD] = kr
        v_ref[:, hh * V_AUG:hh * V_AUG + MLA_V] = kvf[:, base + MLA_NOPE:base + MLA_PAD].astype(BF16)
        v_ref[:, hh * V_AUG + MLA_V:(hh + 1) * V_AUG] = jnp.ones((TP, V_AUG - MLA_V), BF16)


def _mla_prep(pb, pkr, q_gain, kv_gain, w_uq_pad, w_ukv, tabs):
    tab_spec = pl.BlockSpec((TP, LANES), lambda t: (_rope_idx(t), 0))
    qk_w = MLA_HEADS * MLA_PAD
    return pl.pallas_call(
        _mla_prep_kernel,
        grid=(M // TP,),
        in_specs=[
            pl.BlockSpec((TP, MLA_Q_RANK), lambda t: (t, (OFF_CQ - OFF_HG) // MLA_Q_RANK)),
            pl.BlockSpec((TP, MLA_KV_RANK), lambda t: (t, (OFF_CKV - OFF_HG) // MLA_KV_RANK)),
            pl.BlockSpec((TP, LANES), lambda t: (t, 0)),
            pl.BlockSpec((1, MLA_Q_RANK), lambda t: (0, 0)),
            pl.BlockSpec((1, MLA_KV_RANK), lambda t: (0, 0)),
            pl.BlockSpec((MLA_Q_RANK, qk_w), lambda t: (0, 0)),
            pl.BlockSpec((MLA_KV_RANK, qk_w), lambda t: (0, 0)),
            tab_spec, tab_spec, tab_spec,
        ],
        out_specs=[pl.BlockSpec((TP, qk_w), lambda t: (t, 0)),
                   pl.BlockSpec((TP, qk_w), lambda t: (t, 0)),
                   pl.BlockSpec((TP, MLA_HEADS * V_AUG), lambda t: (t, 0))],
        out_shape=[jax.ShapeDtypeStruct((M, qk_w), BF16),
                   jax.ShapeDtypeStruct((M, qk_w), BF16),
                   jax.ShapeDtypeStruct((M, MLA_HEADS * V_AUG), BF16)],
        compiler_params=_cparams(("parallel",), 32),
        name="mla_prep",
    )(pb, pb, pkr, q_gain.reshape(1, -1), kv_gain.reshape(1, -1), w_uq_pad, w_ukv, *tabs)


TILE_CHUNKS = TQ // HCHUNK
DIAG = 8
LEVELS = (8, 16, 32)


def _split3(x):
    hi = x.astype(BF16)
    r1 = x - hi.astype(F32)
    mid = r1.astype(BF16)
    lo = (r1 - mid.astype(F32)).astype(BF16)
    return hi, mid, lo


def _hgrn_kernel(q_ref, v_ref, z_ref, lg_ref, o_ref, *scratch, reverse, layer):
    cum_refs = scratch[0:HG_HEADS]
    k_refs = scratch[HG_HEADS:2 * HG_HEADS]
    s_refs = scratch[2 * HG_HEADS:]

    lg = [lg_ref[l:l + 1, :] for l in range(DEPTH)]
    mx = functools.reduce(jnp.maximum, lg)
    ex = [jnp.exp(r - mx) for r in lg]
    den = functools.reduce(lambda a_, b_: a_ + b_, ex)
    lb_all = jnp.zeros((1, HG_W), F32)
    for l in range(1, layer + 1):
        lb_all = lb_all + ex[l] / den
    log_lb_all = jnp.log(lb_all)
    log_1m_all = jnp.log1p(-lb_all)

    ti = lax.broadcasted_iota(jnp.int32, (TQ, TQ), 0)
    si = lax.broadcasted_iota(jnp.int32, (TQ, TQ), 1)
    causal = (si >= ti) if reverse else (si <= ti)
    tri = (causal & ((ti // HCHUNK) == (si // HCHUNK))).astype(BF16)
    tc = lax.broadcasted_iota(jnp.int32, (HCHUNK, HCHUNK), 0)
    sc = lax.broadcasted_iota(jnp.int32, (HCHUNK, HCHUNK), 1)
    diag_mask = (((sc >= tc) if reverse else (sc <= tc)) & ((tc // DIAG) == (sc // DIAG))).astype(F32)
    ri = lax.broadcasted_iota(jnp.int32, (DIAG * HG_DK, LANES), 0)
    ci = lax.broadcasted_iota(jnp.int32, (DIAG * HG_DK, LANES), 1)
    sel = ((ri // HG_DK) == (ci % DIAG)).astype(BF16)
    edge = 0 if reverse else HCHUNK - 1

    @pl.when(pl.program_id(1) == 0)
    def _():
        for s_ref in s_refs:
            s_ref[...] = jnp.zeros_like(s_ref)

    heads = range(HG_HEADS)

    def lanes(hh):
        return slice(hh * HG_DK, (hh + 1) * HG_DK)

    chunks = range(TILE_CHUNKS)

    def crow(c):
        return slice(c * HCHUNK, (c + 1) * HCHUNK)

    q = [q_ref[:, lanes(hh)].astype(F32) * (HG_DK ** -0.5) for hh in heads]
    v = [v_ref[:, lanes(hh)] for hh in heads]

    z = z_ref[...]
    u = jnp.exp(-jnp.abs(z))
    w = 1.0 + u
    b_ = log_1m_all + (jnp.minimum(z, 0.0) - jnp.log(w))
    log_f = jnp.maximum(log_lb_all, b_) + jnp.log(1.0 + jnp.exp(-jnp.abs(log_lb_all - b_)))
    k_all = (1.0 - lb_all) * (jnp.where(z >= 0.0, u, 1.0) / w)
    k = [k_all[:, lanes(hh)] for hh in heads]

    cum3 = jnp.dot(tri, jnp.concatenate(_split3(log_f), axis=1), preferred_element_type=F32)
    cum_all = (cum3[:, :HG_W] + cum3[:, HG_W:2 * HG_W] + cum3[:, 2 * HG_W:]) * LOG2E
    cum = [cum_all[:, lanes(hh)] for hh in heads]
    for hh in heads:
        cum_refs[hh][...] = cum[hh]
        k_refs[hh][...] = k[hh]
    tot = [[cum_refs[hh][c * HCHUNK + edge:c * HCHUNK + edge + 1, :] for c in chunks] for hh in heads]

    def bcast_rows(ref, s):
        return jnp.concatenate(
            [jnp.broadcast_to(ref[blk * DIAG + s:blk * DIAG + s + 1, :], (DIAG, HG_DK))
             for blk in range(TQ // DIAG)], axis=0)

    pair_lhs = []
    for hh in heads:
        cols = [(q[hh] * jnp.exp2(jnp.minimum(cum[hh] - bcast_rows(cum_refs[hh], s), 0.0))
                 * bcast_rows(k_refs[hh], s)).astype(BF16) for s in range(DIAG)]
        pair_lhs.append(jnp.concatenate(cols, axis=1))

    def span_operands(hh, c):
        base = c * HCHUNK
        lhs, rhs = [], []
        for hs in LEVELS:
            for blk in range(HCHUNK // (2 * hs)):
                lo, mid, hi = blk * 2 * hs, blk * 2 * hs + hs, (blk + 1) * 2 * hs
                k_rows, q_rows = ((mid, hi), (lo, mid)) if reverse else ((lo, mid), (mid, hi))
                ref_row = base + (mid if reverse else mid - 1)
                rr = cum_refs[hh][ref_row:ref_row + 1, :]
                qr = slice(base + q_rows[0], base + q_rows[1])
                kr = slice(base + k_rows[0], base + k_rows[1])
                qs = q[hh][qr, :] * jnp.exp2(cum[hh][qr, :] - rr)
                ks = k[hh][kr, :] * jnp.exp2(rr - cum[hh][kr, :])
                zq = [jnp.zeros((q_rows[0], HG_DK), F32), qs, jnp.zeros((HCHUNK - q_rows[1], HG_DK), F32)]
                zk = [jnp.zeros((k_rows[0], HG_DK), F32), ks, jnp.zeros((HCHUNK - k_rows[1], HG_DK), F32)]
                lhs.append(jnp.concatenate([p_ for p_ in zq if p_.shape[0]], axis=0))
                rhs.append(jnp.concatenate([p_ for p_ in zk if p_.shape[0]], axis=0))
        return jnp.concatenate(lhs, axis=1).astype(BF16), jnp.concatenate(rhs, axis=1).astype(BF16)

    spans = [[span_operands(hh, c) for c in chunks] for hh in heads]
    qe = [(q[hh] * jnp.exp2(cum[hh])).astype(BF16) for hh in heads]
    ke = [[(k[hh][crow(c), :] * jnp.exp2(tot[hh][c] - cum[hh][crow(c), :])).astype(BF16) for c in chunks]
          for hh in heads]

    pair = [jnp.dot(pair_lhs[hh], sel, preferred_element_type=F32) for hh in heads]
    a_off = [[_nt_dot(*spans[hh][c]) for c in chunks] for hh in heads]
    upd = [[lax.dot_general(v[hh][crow(c), :], ke[hh][c], (((0,), (0,)), ((), ())),
                            preferred_element_type=F32) for c in chunks] for hh in heads]
    a = [[(a_off[hh][c] + pair[hh][crow(c), :HCHUNK] * diag_mask).astype(BF16) for c in chunks] for hh in heads]
    o_intra = [[jnp.dot(a[hh][c], v[hh][crow(c), :], preferred_element_type=F32) for c in chunks] for hh in heads]

    state = [s_refs[hh][...] for hh in heads]
    for c in (reversed(chunks) if reverse else chunks):
        inter = [_nt_dot(qe[hh][crow(c), :], state[hh].astype(BF16)) for hh in heads]
        for hh in heads:
            o_ref[crow(c), lanes(hh)] = o_intra[hh][c] + inter[hh]
            state[hh] = state[hh] * jnp.exp2(tot[hh][c]) + upd[hh][c]
    for hh in heads:
        s_refs[hh][...] = state[hh]


def _hgrn(pa, pz, lb_logits_dir, layer, reverse):
    def tile_row(b, j):
        lat = b * QT_SEQ + (QT_SEQ - j if reverse else j - 1)
        return jnp.where(j == 0, LAT // TQ + b, lat)

    def spec(col):
        return pl.BlockSpec((TQ, HG_W), lambda b, j: (tile_row(b, j), col))

    return pl.pallas_call(
        functools.partial(_hgrn_kernel, reverse=reverse, layer=layer),
        grid=(BATCH, QT_SEQ + 1),
        in_specs=[spec(OFF_HQ // HG_W), spec(OFF_HI // HG_W), spec(1 if reverse else 0),
                  pl.BlockSpec((DEPTH, HG_W), lambda b, j: (0, 0))],
        out_specs=spec(0),
        out_shape=jax.ShapeDtypeStruct((M, HG_W), F32),
        scratch_shapes=([pltpu.VMEM((TQ, HG_DK), F32)] * (2 * HG_HEADS)
                        + [pltpu.VMEM((HG_DK, HG_DK), F32)] * HG_HEADS),
        compiler_params=_cparams(("parallel", "arbitrary"), 32),
        name="hgrn_bwd" if reverse else "hgrn_fwd",
    )(pa, pa, pz, lb_logits_dir)


def _outproj_kernel(x_ref, gt_ref, oa_ref, of_ref, ob_ref, hg_ref, om_ref, ng_ref, w_ref, o_ref, lhs_ref):
    a_w = GQA_HEADS * HEAD_DIM
    lhs_ref[:, 0:a_w] = oa_ref[...]
    lhs_ref[:, a_w + HG_W:] = om_ref[...]
    gain = ng_ref[...]
    for hh in range(HG_HEADS):
        sl = slice(hh * HG_DK, (hh + 1) * HG_DK)
        g = hg_ref[:, sl].astype(F32)
        y = _head_norm(of_ref[:, sl] + ob_ref[:, sl], gain) * (g * _sigmoid(g))
        lhs_ref[:, a_w + hh * HG_DK:a_w + (hh + 1) * HG_DK] = y.astype(BF16)

    lhs = lhs_ref[...]
    gate = _mod_vec(gt_ref, pl.program_id(0), TMP)
    for n in range(D // PROJ_TN):
        cols = slice(n * PROJ_TN, (n + 1) * PROJ_TN)
        y = jnp.dot(lhs, w_ref[n], preferred_element_type=F32)
        o_ref[:, cols] = x_ref[:, cols] + gate[:, cols] * y


def _outproj(x, mods, o_a, o_f, o_b, pb, o_m, norm_gain, w_tiles, layer, rows):
    def row_spec(w, col=0):
        return pl.BlockSpec((TMP, w), lambda i: (i, col))

    return pl.pallas_call(
        _outproj_kernel,
        grid=(rows // TMP,),
        in_specs=[
            row_spec(D),
            pl.BlockSpec((None, SUBLANES, D), lambda i: (layer, 0, 5)),
            row_spec(GQA_HEADS * HEAD_DIM), row_spec(HG_W), row_spec(HG_W),
            row_spec(HG_W, 0),
            row_spec(MLA_HEADS * MLA_V),
            pl.BlockSpec((1, HG_DK), lambda i: (0, 0)),
            pl.BlockSpec((None, D // PROJ_TN, D, PROJ_TN), lambda i: (layer, 0, 0, 0),
                         pipeline_mode=pl.Buffered(1)),
        ],
        out_specs=row_spec(D),
        out_shape=jax.ShapeDtypeStruct((rows, D), F32),
        scratch_shapes=[pltpu.VMEM((TMP, D), BF16)],
        compiler_params=_cparams(("parallel",), 56),
        name="outproj",
    )(x, mods, o_a, o_f, o_b, pb, o_m, norm_gain.reshape(1, HG_DK), w_tiles)


def _pad_w_uq(w_uq):
    w = w_uq.reshape(DEPTH, MLA_Q_RANK, MLA_HEADS, MLA_QK)
    w = jnp.pad(w, ((0, 0), (0, 0), (0, 0), (0, MLA_PAD - MLA_QK)))
    return w.reshape(DEPTH, MLA_Q_RANK, MLA_HEADS * MLA_PAD).astype(BF16)


def kernel(x, c, ctx, c_ctx, w_mod, b_mod, w_ffn1_in, w_ffn1_out, w_in, w_uq, w_ukv, w_out,
           gqa_q_gain, gqa_k_gain, mla_q_gain, mla_kv_gain, hgrn_lb_logits, hgrn_norm_gain,
           w_ffn2_in, w_ffn2_out, final_gain):
    cvec8 = jnp.concatenate([c, c_ctx[None, :], jnp.zeros((SUBLANES - BATCH - 1, D), F32)], axis=0)
    mods = _modulation(cvec8, w_mod, b_mod)
    tabs_h = _rope_tables(HEAD_DIM)
    tabs_r = _rope_tables(MLA_ROPE)
    f1_in, f1_out = _ffn_win_tiles(w_ffn1_in), w_ffn1_out
    f2_in, f2_out = _ffn_win_tiles(w_ffn2_in), w_ffn2_out
    w_in_rows = jnp.transpose(w_in, (0, 2, 1))
    w_in_t = _inproj_weight_tiles(w_in_rows)
    w_out_t = _proj_tiles(w_out, D)
    w_kr = jnp.pad(w_in_rows[:, OFF_KR:, :], ((0, 0), (0, LANES - MLA_ROPE), (0, 0))).astype(BF16)
    w_uq_pad = _pad_w_uq(w_uq)
    w_ukv_b = w_ukv.astype(BF16)

    xa = None
    for l in range(DEPTH):
        last = l == DEPTH - 1
        if l == 0:
            xa = _ffn(x.reshape(LAT, D), mods, f1_in, f1_out, l, 0, ctx=ctx.reshape(CTXR, D))
        else:
            xa = _ffn(xa, mods, f1_in, f1_out, l, 0)
        pa, pz, pb, pkr = _inproj(xa, mods, w_in_t, w_kr, l)
        q_a, k_a, v_a = _gqa_prep(pa, gqa_q_gain[l], gqa_k_gain[l], tabs_h)
        o_a = _gqa_attn(q_a, k_a, v_a, not last)
        q_m, k_m, v_m = _mla_prep(pb, pkr, mla_q_gain[l], mla_kv_gain[l], w_uq_pad[l], w_ukv_b[l], tabs_r)
        o_m = _mla_attn(q_m, k_m, v_m, not last)
        o_f = _hgrn(pa, pz, hgrn_lb_logits[0], l, False)
        o_b = _hgrn(pa, pz, hgrn_lb_logits[1], l, True)
        xa = _outproj(xa, mods, o_a, o_f, o_b, pb, o_m, hgrn_norm_gain[l], w_out_t, l, LAT if last else M)
        xa = _ffn(xa, mods, f2_in, f2_out, l, 6, final_gain=final_gain if last else None)
    return xa
```

```python
import functools

import jax
import jax.numpy as jnp
from jax import lax
from jax.experimental import pallas as pl
from jax.experimental.pallas import tpu as pltpu

F32 = jnp.float32
BF16 = jnp.bfloat16

D = 2048
BATCH = 4
SEQ = 2048
DEPTH = 2
GRID_W = 64
CTX = 256
EPS = 1e-6
ROPE_THETA = 10000.0

HEAD_DIM = 128
GQA_HEADS = 8
GQA_KV = 2
GQA_GROUP = GQA_HEADS // GQA_KV
HG_HEADS = 4
HG_DK = 128
HG_W = HG_HEADS * HG_DK
MLA_HEADS = 4
MLA_Q_RANK = 512
MLA_KV_RANK = 256
MLA_NOPE = 128
MLA_ROPE = 64
MLA_V = 128
MLA_QK = MLA_NOPE + MLA_ROPE
MLA_PAD = 256
FFN_H = 5504
N_MOD = 9

LAT = BATCH * SEQ
CTXR = BATCH * CTX
M = LAT + CTXR
OFF_GQ, OFF_GK, OFF_GV = 0, 1024, 1280
OFF_HQ, OFF_HI, OFF_HF, OFF_HB, OFF_HG = 1536, 2048, 2560, 3072, 3584
OFF_CQ, OFF_CKV, OFF_KR = 4096, 4608, 4864
IN_MAIN = OFF_KR
PA_W, PZ_W, PB_W = OFF_HF, OFF_HG - OFF_HF, OFF_KR - OFF_HG

LANES = 128
SUBLANES = 8
MXU_W = 256

TM = 1024
TMP = 512
ROWC = 64
TQ = 256
TP = 512
QT_SEQ = SEQ // TQ
HCHUNK = 64
FFN_TH = 512
FFN_HP = -(-FFN_H // FFN_TH) * FFN_TH
FFN_STEPS = FFN_HP // FFN_TH
MOD_TN = 1024
PROJ_TN = MXU_W
PREP_ROWS = 128
WO_SLAB = 128
CAST_ROWS = 64
V_AUG = MXU_W
LOG2E = 1.4426950408889634


def _cparams(sem, vmem_mb):
    return pltpu.CompilerParams(dimension_semantics=sem, vmem_limit_bytes=vmem_mb * 1024 * 1024)


def _sigmoid(x):
    return 1.0 / (1.0 + jnp.exp(-x))


def _nt_dot(a, b):
    return lax.dot_general(a, b, (((1,), (1,)), ((), ())), preferred_element_type=F32)


def _mod_kernel(c_ref, w_ref, b_ref, o_ref):
    c = c_ref[...]
    a = (c * _sigmoid(c)).astype(BF16)
    o_ref[...] = jnp.dot(a, w_ref[...].astype(BF16), preferred_element_type=F32) + b_ref[...]


def _modulation(cvec8, w_mod, b_mod):
    n = N_MOD * D
    return pl.pallas_call(
        _mod_kernel,
        grid=(DEPTH, n // MOD_TN),
        in_specs=[
            pl.BlockSpec((SUBLANES, D), lambda l, j: (0, 0)),
            pl.BlockSpec((None, D, MOD_TN), lambda l, j: (l, 0, j)),
            pl.BlockSpec((None, 1, MOD_TN), lambda l, j: (l, 0, j)),
        ],
        out_specs=pl.BlockSpec((None, SUBLANES, MOD_TN), lambda l, j: (l, 0, j)),
        out_shape=jax.ShapeDtypeStruct((DEPTH, SUBLANES, n), F32),
        compiler_params=_cparams(("parallel", "parallel"), 40),
        name="modulation",
    )(cvec8, w_mod, b_mod.reshape(DEPTH, 1, n))


def _mod_vec(ref, tile, tm):
    row = jnp.where(tile < LAT // tm, tile // (SEQ // tm), BATCH)
    return ref[pl.ds(row, 1), :]


def _modulate_tile(x_ref, shift_ref, scale_ref, h_ref, tile, tm):
    sh = _mod_vec(shift_ref, tile, tm)
    sc = 1.0 + _mod_vec(scale_ref, tile, tm)

    def body(c, carry):
        r0 = pl.multiple_of(c * ROWC, ROWC)
        xc = x_ref[pl.ds(r0, ROWC), :]
        ms = jnp.mean(xc * xc, axis=-1, keepdims=True)
        h_ref[pl.ds(r0, ROWC), :] = (xc * lax.rsqrt(ms + EPS) * sc + sh).astype(BF16)
        return carry

    lax.fori_loop(0, tm // ROWC, body, 0, unroll=2)


def _ffn_kernel(*refs, two_inputs, final):
    refs = list(refs)
    x_ref = refs.pop(0)
    c_ref = refs.pop(0) if two_inputs else None
    sh_ref, sc_ref, gt_ref = refs.pop(0), refs.pop(0), refs.pop(0)
    fg_ref = refs.pop(0) if final else None
    wi_ref = refs.pop(0)
    wo_refs = [refs.pop(0) for _ in range(FFN_TH // WO_SLAB)]
    o_ref, h_ref = refs
    i = pl.program_id(0)
    j = pl.program_id(1)

    def per_source(fn):
        if two_inputs:
            pl.when(i < LAT // TM)(lambda: fn(x_ref))
            pl.when(i >= LAT // TM)(lambda: fn(c_ref))
        else:
            fn(x_ref)

    @pl.when(j == 0)
    def _():
        per_source(lambda src: _modulate_tile(src, sh_ref, sc_ref, h_ref, i, TM))
        o_ref[...] = jnp.zeros_like(o_ref)

    pass_rows = TM // 2 if two_inputs else TM
    wo = jnp.concatenate([r_[...].astype(BF16) for r_ in wo_refs], axis=0)
    for rc in range(TM // pass_rows):
        rows = slice(rc * pass_rows, (rc + 1) * pass_rows)
        r = jnp.dot(h_ref[rows, :], wi_ref[...], preferred_element_type=F32)
        g = r[:, :FFN_TH]
        u = r[:, FFN_TH:]
        a = (g * _sigmoid(g) * u).astype(BF16)
        o_ref[rows, :] += jnp.dot(a, wo, preferred_element_type=F32)

    @pl.when(j == FFN_STEPS - 1)
    def _():
        gate = 0.5 * _mod_vec(gt_ref, i, TM)

        def epilogue(src):
            def body(c, carry):
                r0 = pl.multiple_of(c * ROWC, ROWC)
                y = src[pl.ds(r0, ROWC), :] + gate * o_ref[pl.ds(r0, ROWC), :]
                if final:
                    y = y * lax.rsqrt(jnp.mean(y * y, axis=-1, keepdims=True) + EPS) * fg_ref[...]
                o_ref[pl.ds(r0, ROWC), :] = y
                return carry

            lax.fori_loop(0, TM // ROWC, body, 0)

        per_source(epilogue)


def _ffn(x, mods, wi, wo, layer, mod0, ctx=None, final_gain=None):
    two_inputs, final = ctx is not None, final_gain is not None
    rows = M if two_inputs else x.shape[0]
    lat_tiles = LAT // TM

    def mod_spec(k):
        return pl.BlockSpec((None, SUBLANES, D), lambda i, j: (layer, 0, mod0 + k))

    in_specs = [pl.BlockSpec((TM, D), lambda i, j: (jnp.minimum(i, lat_tiles - 1) if two_inputs else i, 0),
                             pipeline_mode=pl.Buffered(1))]
    args = [x]
    if two_inputs:
        in_specs.append(pl.BlockSpec((TM, D), lambda i, j: (0, 0), pipeline_mode=pl.Buffered(1)))
        args.append(ctx)
    in_specs += [mod_spec(0), mod_spec(1), mod_spec(2)]
    args += [mods, mods, mods]
    if final:
        in_specs.append(pl.BlockSpec((1, D), lambda i, j: (0, 0)))
        args.append(final_gain.reshape(1, D))
        out_spec = pl.BlockSpec((None, TM, D), lambda i, j: (i // (SEQ // TM), i % (SEQ // TM), 0))
        out_shape = jax.ShapeDtypeStruct((BATCH, SEQ, D), F32)
    else:
        out_spec = pl.BlockSpec((TM, D), lambda i, j: (i, 0))
        out_shape = jax.ShapeDtypeStruct((rows, D), F32)
    in_specs.append(pl.BlockSpec((None, None, D, 2 * FFN_TH), lambda i, j: (0, j, 0, 0)))
    args.append(wi)
    slabs, last_slab = FFN_TH // WO_SLAB, FFN_H // WO_SLAB - 1
    for r in range(slabs):
        in_specs.append(pl.BlockSpec(
            (None, WO_SLAB, D),
            functools.partial(lambda i, j, r: (layer, jnp.minimum(j * slabs + r, last_slab), 0), r=r)))
        args.append(wo)
    return pl.pallas_call(
        functools.partial(_ffn_kernel, two_inputs=two_inputs, final=final),
        grid=(rows // TM, FFN_STEPS),
        in_specs=in_specs,
        out_specs=out_spec,
        out_shape=out_shape,
        scratch_shapes=[pltpu.VMEM((TM, D), BF16)],
        compiler_params=_cparams(("parallel", "arbitrary"), 58),
        name="ffn",
    )(*args)


def _ffn_win_tiles_body(w_ref, o_ref, rows):
    for j in range(FFN_STEPS):
        valid = min(FFN_TH, FFN_H - j * FFN_TH)
        for part, base in ((0, 0), (1, FFN_H)):
            c0 = part * FFN_TH
            o_ref[j, :, c0:c0 + valid] = w_ref[:, base + j * FFN_TH:base + j * FFN_TH + valid].astype(BF16)
            if valid < FFN_TH:
                o_ref[j, :, c0 + valid:c0 + FFN_TH] = jnp.zeros((rows, FFN_TH - valid), BF16)


def _ffn_win_tiles(w_in, layer):
    return pl.pallas_call(
        functools.partial(_ffn_win_tiles_body, rows=PREP_ROWS),
        grid=(D // PREP_ROWS,),
        in_specs=[pl.BlockSpec((None, PREP_ROWS, 2 * FFN_H), lambda r: (layer, r, 0))],
        out_specs=pl.BlockSpec((None, FFN_STEPS, PREP_ROWS, 2 * FFN_TH), lambda r: (0, 0, r, 0)),
        out_shape=jax.ShapeDtypeStruct((1, FFN_STEPS, D, 2 * FFN_TH), BF16),
        compiler_params=_cparams(("parallel",), 32),
        name="ffn_win_tiles",
    )(w_in)


def _proj_tiles_kernel(w_ref, o_ref):
    for t in range(o_ref.shape[0]):
        o_ref[t] = w_ref[:, t * PROJ_TN:(t + 1) * PROJ_TN].astype(BF16)


def _proj_tiles(w, n):
    k = w.shape[1]
    return pl.pallas_call(
        _proj_tiles_kernel,
        grid=(DEPTH, k // PREP_ROWS),
        in_specs=[pl.BlockSpec((None, PREP_ROWS, n), lambda l, r: (l, r, 0))],
        out_specs=pl.BlockSpec((None, n // PROJ_TN, PREP_ROWS, PROJ_TN), lambda l, r: (l, 0, r, 0)),
        out_shape=jax.ShapeDtypeStruct((DEPTH, n // PROJ_TN, k, PROJ_TN), BF16),
        compiler_params=_cparams(("parallel", "parallel"), 32),
        name="proj_tiles",
    )(w)


def _cast_rows_kernel(w_ref, o_ref):
    o_ref[...] = w_ref[...].astype(BF16)


def _inproj_weight_tiles(w_in_t):
    n_tiles = IN_MAIN // PROJ_TN
    return pl.pallas_call(
        _cast_rows_kernel,
        grid=(DEPTH, n_tiles),
        in_specs=[pl.BlockSpec((None, PROJ_TN, D), lambda l, t: (l, t, 0))],
        out_specs=pl.BlockSpec((None, None, PROJ_TN, D), lambda l, t: (l, t, 0, 0)),
        out_shape=jax.ShapeDtypeStruct((DEPTH, n_tiles, PROJ_TN, D), BF16),
        compiler_params=_cparams(("parallel", "parallel"), 32),
        name="inproj_weight_tiles",
    )(w_in_t)


def _inproj_kernel(x_ref, sh_ref, sc_ref, w_ref, wkr_ref, pa_ref, pz_ref, pb_ref, kr_ref, h_ref):
    _modulate_tile(x_ref, sh_ref, sc_ref, h_ref, pl.program_id(0), TMP)
    h = h_ref[...]
    kr_ref[...] = _nt_dot(h, wkr_ref[...])
    for t in range(IN_MAIN // PROJ_TN):
        y = _nt_dot(h, w_ref[t])
        c0 = t * PROJ_TN
        if c0 < OFF_HF:
            pa_ref[:, c0:c0 + PROJ_TN] = y.astype(BF16)
        elif c0 < OFF_HG:
            pz_ref[:, c0 - OFF_HF:c0 - OFF_HF + PROJ_TN] = y
        else:
            pb_ref[:, c0 - OFF_HG:c0 - OFF_HG + PROJ_TN] = y.astype(BF16)


def _inproj(x, mods, w_tiles, w_kr, layer):
    def mod_spec(k):
        return pl.BlockSpec((None, SUBLANES, D), lambda i: (layer, 0, k))

    def out_spec(w):
        return pl.BlockSpec((TMP, w), lambda i: (i, 0))

    n_tiles = IN_MAIN // PROJ_TN
    return pl.pallas_call(
        _inproj_kernel,
        grid=(M // TMP,),
        in_specs=[
            pl.BlockSpec((TMP, D), lambda i: (i, 0)),
            mod_spec(3), mod_spec(4),
            pl.BlockSpec((None, n_tiles, PROJ_TN, D), lambda i: (layer, 0, 0, 0), pipeline_mode=pl.Buffered(1)),
            pl.BlockSpec((None, LANES, D), lambda i: (layer, 0, 0)),
        ],
        out_specs=[out_spec(PA_W), out_spec(PZ_W), out_spec(PB_W), out_spec(LANES)],
        out_shape=[jax.ShapeDtypeStruct((M, PA_W), BF16), jax.ShapeDtypeStruct((M, PZ_W), F32),
                   jax.ShapeDtypeStruct((M, PB_W), BF16), jax.ShapeDtypeStruct((M, LANES), F32)],
        scratch_shapes=[pltpu.VMEM((TMP, D), BF16)],
        compiler_params=_cparams(("parallel",), 56),
        name="inproj",
    )(x, mods, mods, w_tiles, w_kr)


def _rope_tables(rot_dim):
    rows = SEQ // GRID_W
    row = jnp.repeat(jnp.arange(rows, dtype=F32), GRID_W)
    colp = jnp.tile(jnp.arange(GRID_W, dtype=F32), rows)
    axis_dim = rot_dim // 2
    inv_freq = ROPE_THETA ** (-jnp.arange(0, axis_dim, 2, dtype=F32) / axis_dim)
    ang_r = row[:, None] * inv_freq
    ang_c = colp[:, None] * inv_freq
    ang = jnp.concatenate([ang_r, ang_r, ang_c, ang_c], axis=-1)
    cos, sin = jnp.cos(ang), jnp.sin(ang)
    quarter = rot_dim // 4
    lane = jnp.arange(rot_dim)
    first = (lane % (2 * quarter)) < quarter
    sin_up = jnp.where(first, -sin, 0.0)
    sin_dn = jnp.where(first, 0.0, sin)
    pad = LANES - rot_dim
    if pad:
        cos = jnp.pad(cos, ((0, 0), (0, pad)), constant_values=1.0)
        sin_up = jnp.pad(sin_up, ((0, 0), (0, pad)))
        sin_dn = jnp.pad(sin_dn, ((0, 0), (0, pad)))
    return cos, sin_up, sin_dn


def _head_norm(x, gain):
    return x * lax.rsqrt(jnp.mean(x * x, axis=-1, keepdims=True) + EPS) * gain


def _lane_shift_matrix(quarter):
    j = lax.broadcasted_iota(jnp.int32, (LANES, 2 * LANES), 0)
    i = lax.broadcasted_iota(jnp.int32, (LANES, 2 * LANES), 1)
    source = jnp.where(i < LANES, i + quarter, i - LANES - quarter)
    return jnp.where(j == source, 1.0, 0.0).astype(BF16)


def _rope_mxu(x, shifts, cos, sin_up, sin_dn):
    r = jnp.dot(x.astype(BF16), shifts, preferred_element_type=F32)
    return x * cos + r[:, :LANES] * sin_up + r[:, LANES:] * sin_dn


def _head_norm_mxu(x, gain):
    mean_w = jnp.full((LANES, LANES), 1.0 / LANES, BF16)
    ms = jnp.dot((x * x).astype(BF16), mean_w, preferred_element_type=F32)
    return x * lax.rsqrt(ms + EPS) * gain


def _tile_is_ctx(t):
    return t >= LAT // TP


def _rope_idx(t):
    return jnp.where(_tile_is_ctx(t), 0, t % (SEQ // TP))


def _gqa_prep_kernel(p_ref, qg_ref, kg_ref, cos_ref, su_ref, sd_ref, q_ref, k_ref, v_ref):
    is_ctx = _tile_is_ctx(pl.program_id(0))
    cos, su, sd = cos_ref[...], su_ref[...], sd_ref[...]
    scale = HEAD_DIM ** -0.5 * LOG2E
    for kk in range(GQA_KV):
        v_ref[:, kk * V_AUG:kk * V_AUG + HEAD_DIM] = p_ref[:, OFF_GV + kk * HEAD_DIM:OFF_GV + (kk + 1) * HEAD_DIM]
        v_ref[:, kk * V_AUG + HEAD_DIM:(kk + 1) * V_AUG] = jnp.ones((TP, V_AUG - HEAD_DIM), BF16)
    shifts = _lane_shift_matrix(HEAD_DIM // 4)
    for hh in range(GQA_HEADS + GQA_KV):
        xh = p_ref[:, hh * HEAD_DIM:(hh + 1) * HEAD_DIM].astype(F32)
        is_q = hh < GQA_HEADS
        n = _head_norm_mxu(xh, qg_ref[...] if is_q else kg_ref[...])
        y = jnp.where(is_ctx, n, _rope_mxu(n, shifts, cos, su, sd))
        if is_q:
            q_ref[:, hh * HEAD_DIM:(hh + 1) * HEAD_DIM] = (y * scale).astype(BF16)
        else:
            kk = hh - GQA_HEADS
            k_ref[:, kk * HEAD_DIM:(kk + 1) * HEAD_DIM] = y.astype(BF16)


def _gqa_prep(pa, q_gain, k_gain, tabs):
    width = OFF_HQ
    tab_spec = pl.BlockSpec((TP, LANES), lambda t: (_rope_idx(t), 0))
    vec_spec = pl.BlockSpec((1, HEAD_DIM), lambda t: (0, 0))
    return pl.pallas_call(
        _gqa_prep_kernel,
        grid=(M // TP,),
        in_specs=[pl.BlockSpec((TP, width), lambda t: (t, 0)), vec_spec, vec_spec,
                  tab_spec, tab_spec, tab_spec],
        out_specs=[pl.BlockSpec((TP, GQA_HEADS * HEAD_DIM), lambda t: (t, 0)),
                   pl.BlockSpec((TP, GQA_KV * HEAD_DIM), lambda t: (t, 0)),
                   pl.BlockSpec((TP, GQA_KV * V_AUG), lambda t: (t, 0))],
        out_shape=[jax.ShapeDtypeStruct((M, GQA_HEADS * HEAD_DIM), BF16),
                   jax.ShapeDtypeStruct((M, GQA_KV * HEAD_DIM), BF16),
                   jax.ShapeDtypeStruct((M, GQA_KV * V_AUG), BF16)],
        compiler_params=_cparams(("parallel",), 32),
        name="gqa_prep",
    )(pa, q_gain.reshape(1, HEAD_DIM), k_gain.reshape(1, HEAD_DIM), *tabs)


def _attn_kernel(q_ref, kl_ref, kc_ref, vl_ref, vc_ref, *rest, n_heads, head_cols, with_ctx):
    o_ref = rest[-1] if len(rest) == 1 else rest[1]
    if len(rest) == 3:
        step = pl.program_id(0) * pl.num_programs(1) + pl.program_id(1)
        pl.when(step < D // CAST_ROWS)(lambda: _ffn_win_tiles_body(rest[0], rest[2], CAST_ROWS))

    def finish(pv, os_):
        e_w = os_.stop - os_.start
        o_ref[:, os_] = (pv[:, :e_w] / pv[:, e_w:2 * e_w]).astype(BF16)

    def latent_queries():
        for hh in range(n_heads):
            qs, ks, vs, os_ = head_cols(hh)
            q = q_ref[:, qs]
            s = jnp.concatenate([_nt_dot(q, kl_ref[:, ks]), _nt_dot(q, kc_ref[:, ks])], axis=1)
            eb = jnp.exp2(s - jnp.max(s, axis=-1, keepdims=True)).astype(BF16)
            finish(jnp.dot(eb[:, :SEQ], vl_ref[:, vs], preferred_element_type=F32)
                   + jnp.dot(eb[:, SEQ:], vc_ref[:, vs], preferred_element_type=F32), os_)

    def context_queries():
        for hh in range(n_heads):
            qs, ks, vs, os_ = head_cols(hh)
            s = _nt_dot(q_ref[:, qs], kc_ref[:, ks])
            eb = jnp.exp2(s - jnp.max(s, axis=-1, keepdims=True)).astype(BF16)
            finish(jnp.dot(eb, vc_ref[:, vs], preferred_element_type=F32), os_)

    if with_ctx:
        pl.when(pl.program_id(1) < QT_SEQ)(latent_queries)
        pl.when(pl.program_id(1) == QT_SEQ)(context_queries)
    else:
        latent_queries()


def _attention(name, q, k, v, v_col, qw, kw, vw, ow, n_heads, head_cols, with_ctx, cast=None):
    nq = QT_SEQ + (1 if with_ctx else 0)

    def q_row(b, i):
        return jnp.where(i < QT_SEQ, b * QT_SEQ + i, LAT // TQ + b)

    def ctx_row(b, i):
        return LAT // CTX + b

    in_specs = [
        pl.BlockSpec((TQ, qw), lambda b, i: (q_row(b, i), 0)),
        pl.BlockSpec((SEQ, kw), lambda b, i: (b, 0)),
        pl.BlockSpec((CTX, kw), lambda b, i: (ctx_row(b, i), 0)),
        pl.BlockSpec((SEQ, vw), lambda b, i: (b, v_col)),
        pl.BlockSpec((CTX, vw), lambda b, i: (ctx_row(b, i), v_col)),
    ]
    args = [q, k, k, v, v]
    out_specs = pl.BlockSpec((TQ, ow), lambda b, i: (q_row(b, i), 0))
    out_shape = jax.ShapeDtypeStruct((M if with_ctx else LAT, ow), BF16)
    if cast is not None:
        w_in, layer = cast

        def panel(b, i):
            return jnp.minimum(b * nq + i, D // CAST_ROWS - 1)

        in_specs.append(pl.BlockSpec((None, CAST_ROWS, 2 * FFN_H), lambda b, i: (layer, panel(b, i), 0)))
        args.append(w_in)
        out_specs = [out_specs,
                     pl.BlockSpec((None, FFN_STEPS, CAST_ROWS, 2 * FFN_TH), lambda b, i: (0, 0, panel(b, i), 0))]
        out_shape = [out_shape, jax.ShapeDtypeStruct((1, FFN_STEPS, D, 2 * FFN_TH), BF16)]
    return pl.pallas_call(
        functools.partial(_attn_kernel, n_heads=n_heads, head_cols=head_cols, with_ctx=with_ctx),
        grid=(BATCH, nq),
        in_specs=in_specs,
        out_specs=out_specs,
        out_shape=out_shape,
        compiler_params=_cparams(("arbitrary", "arbitrary"), 48),
        name=name,
    )(*args)


def _gqa_head_cols(hh):
    kk = hh // GQA_GROUP
    kv = slice(kk * HEAD_DIM, (kk + 1) * HEAD_DIM)
    hs = slice(hh * HEAD_DIM, (hh + 1) * HEAD_DIM)
    return hs, kv, slice(kk * V_AUG, (kk + 1) * V_AUG), hs


def _mla_head_cols(hh):
    qs = slice(hh * MLA_PAD, (hh + 1) * MLA_PAD)
    return qs, qs, slice(hh * V_AUG, (hh + 1) * V_AUG), slice(hh * MLA_V, (hh + 1) * MLA_V)


def _gqa_attn(q, k, v_aug, with_ctx, cast=None):
    qw, kw = GQA_HEADS * HEAD_DIM, GQA_KV * HEAD_DIM
    return _attention("gqa_attn", q, k, v_aug, 0, qw, kw, GQA_KV * V_AUG, qw, GQA_HEADS, _gqa_head_cols, with_ctx,
                      cast)


def _mla_attn(q, k, v_aug, with_ctx, cast=None):
    qw = MLA_HEADS * MLA_PAD
    return _attention("mla_attn", q, k, v_aug, 0, qw, qw, MLA_HEADS * V_AUG, MLA_HEADS * MLA_V, MLA_HEADS,
                      _mla_head_cols, with_ctx, cast)


def _mla_prep_kernel(cq_ref, ckv_ref, kr_ref, qg_ref, kvg_ref, wuq_ref, wukv_ref,
                     cos_ref, su_ref, sd_ref, q_ref, k_ref, v_ref):
    is_ctx = _tile_is_ctx(pl.program_id(0))
    cos, su, sd = cos_ref[...], su_ref[...], sd_ref[...]
    quarter = MLA_ROPE // 4
    scale = MLA_QK ** -0.5 * LOG2E

    cq = _head_norm(cq_ref[...].astype(F32), qg_ref[...]).astype(BF16)
    qf = jnp.dot(cq, wuq_ref[...], preferred_element_type=F32)
    ckv = _head_norm(ckv_ref[...].astype(F32), kvg_ref[...]).astype(BF16)
    kvf = jnp.dot(ckv, wukv_ref[...], preferred_element_type=F32)
    kr = kr_ref[...]
    shifts = _lane_shift_matrix(quarter)
    kr = jnp.where(is_ctx, kr, _rope_mxu(kr, shifts, cos, su, sd)).astype(BF16)
    for hh in range(MLA_HEADS):
        base = hh * MLA_PAD
        q_ref[:, base:base + MLA_NOPE] = (qf[:, base:base + MLA_NOPE] * scale).astype(BF16)
        qr = qf[:, base + MLA_NOPE:base + MLA_PAD]
        qr = jnp.where(is_ctx, qr, _rope_mxu(qr, shifts, cos, su, sd))
        q_ref[:, base + MLA_NOPE:base + MLA_PAD] = (qr * scale).astype(BF16)
        k_ref[:, base:base + MLA_NOPE] = kvf[:, base:base + MLA_NOPE].astype(BF16)
        k_ref[:, base + MLA_NOPE:base + MLA_PAD] = kr
        v_ref[:, hh * V_AUG:hh * V_AUG + MLA_V] = kvf[:, base + MLA_NOPE:base + MLA_PAD].astype(BF16)
        v_ref[:, hh * V_AUG + MLA_V:(hh + 1) * V_AUG] = jnp.ones((TP, V_AUG - MLA_V), BF16)


def _mla_prep(pb, pkr, q_gain, kv_gain, w_uq_pad, w_ukv, tabs):
    tab_spec = pl.BlockSpec((TP, LANES), lambda t: (_rope_idx(t), 0))
    qk_w = MLA_HEADS * MLA_PAD
    return pl.pallas_call(
        _mla_prep_kernel,
        grid=(M // TP,),
        in_specs=[
            pl.BlockSpec((TP, MLA_Q_RANK), lambda t: (t, (OFF_CQ - OFF_HG) // MLA_Q_RANK)),
            pl.BlockSpec((TP, MLA_KV_RANK), lambda t: (t, (OFF_CKV - OFF_HG) // MLA_KV_RANK)),
            pl.BlockSpec((TP, LANES), lambda t: (t, 0)),
            pl.BlockSpec((1, MLA_Q_RANK), lambda t: (0, 0)),
            pl.BlockSpec((1, MLA_KV_RANK), lambda t: (0, 0)),
            pl.BlockSpec((MLA_Q_RANK, qk_w), lambda t: (0, 0)),
            pl.BlockSpec((MLA_KV_RANK, qk_w), lambda t: (0, 0)),
            tab_spec, tab_spec, tab_spec,
        ],
        out_specs=[pl.BlockSpec((TP, qk_w), lambda t: (t, 0)),
                   pl.BlockSpec((TP, qk_w), lambda t: (t, 0)),
                   pl.BlockSpec((TP, MLA_HEADS * V_AUG), lambda t: (t, 0))],
        out_shape=[jax.ShapeDtypeStruct((M, qk_w), BF16),
                   jax.ShapeDtypeStruct((M, qk_w), BF16),
                   jax.ShapeDtypeStruct((M, MLA_HEADS * V_AUG), BF16)],
        compiler_params=_cparams(("parallel",), 32),
        name="mla_prep",
    )(pb, pb, pkr, q_gain.reshape(1, -1), kv_gain.reshape(1, -1), w_uq_pad, w_ukv, *tabs)


TILE_CHUNKS = TQ // HCHUNK
DIAG = 8
LEVELS = (8, 16, 32)


def _split3(x):
    hi = x.astype(BF16)
    r1 = x - hi.astype(F32)
    mid = r1.astype(BF16)
    lo = (r1 - mid.astype(F32)).astype(BF16)
    return hi, mid, lo


def _hgrn_kernel(q_ref, v_ref, z_ref, lg_ref, o_ref, *scratch, reverse, layer):
    cum_refs = scratch[0:HG_HEADS]
    k_refs = scratch[HG_HEADS:2 * HG_HEADS]
    s_refs = scratch[2 * HG_HEADS:]

    lg = [lg_ref[l:l + 1, :] for l in range(DEPTH)]
    mx = functools.reduce(jnp.maximum, lg)
    ex = [jnp.exp(r - mx) for r in lg]
    den = functools.reduce(lambda a_, b_: a_ + b_, ex)
    lb_all = jnp.zeros((1, HG_W), F32)
    for l in range(1, layer + 1):
        lb_all = lb_all + ex[l] / den
    log_lb_all = jnp.log(lb_all)
    log_1m_all = jnp.log1p(-lb_all)

    ti = lax.broadcasted_iota(jnp.int32, (TQ, TQ), 0)
    si = lax.broadcasted_iota(jnp.int32, (TQ, TQ), 1)
    causal = (si >= ti) if reverse else (si <= ti)
    tri = (causal & ((ti // HCHUNK) == (si // HCHUNK))).astype(BF16)
    tc = lax.broadcasted_iota(jnp.int32, (HCHUNK, HCHUNK), 0)
    sc = lax.broadcasted_iota(jnp.int32, (HCHUNK, HCHUNK), 1)
    diag_mask = (((sc >= tc) if reverse else (sc <= tc)) & ((tc // DIAG) == (sc // DIAG))).astype(F32)
    ri = lax.broadcasted_iota(jnp.int32, (DIAG * HG_DK, LANES), 0)
    ci = lax.broadcasted_iota(jnp.int32, (DIAG * HG_DK, LANES), 1)
    sel = ((ri // HG_DK) == (ci % DIAG)).astype(BF16)
    edge = 0 if reverse else HCHUNK - 1

    @pl.when(pl.program_id(1) == 0)
    def _():
        for s_ref in s_refs:
            s_ref[...] = jnp.zeros_like(s_ref)

    heads = range(HG_HEADS)

    def lanes(hh):
        return slice(hh * HG_DK, (hh + 1) * HG_DK)

    chunks = range(TILE_CHUNKS)

    def crow(c):
        return slice(c * HCHUNK, (c + 1) * HCHUNK)

    q = [q_ref[:, lanes(hh)].astype(F32) * (HG_DK ** -0.5) for hh in heads]
    v = [v_ref[:, lanes(hh)] for hh in heads]

    z = z_ref[...]
    u = jnp.exp(-jnp.abs(z))
    w = 1.0 + u
    b_ = log_1m_all + (jnp.minimum(z, 0.0) - jnp.log(w))
    log_f = jnp.maximum(log_lb_all, b_) + jnp.log(1.0 + jnp.exp(-jnp.abs(log_lb_all - b_)))
    k_all = (1.0 - lb_all) * (jnp.where(z >= 0.0, u, 1.0) / w)
    k = [k_all[:, lanes(hh)] for hh in heads]

    cum3 = jnp.dot(tri, jnp.concatenate(_split3(log_f), axis=1), preferred_element_type=F32)
    cum_all = (cum3[:, :HG_W] + cum3[:, HG_W:2 * HG_W] + cum3[:, 2 * HG_W:]) * LOG2E
    cum = [cum_all[:, lanes(hh)] for hh in heads]
    for hh in heads:
        cum_refs[hh][...] = cum[hh]
        k_refs[hh][...] = k[hh]
    tot = [[cum_refs[hh][c * HCHUNK + edge:c * HCHUNK + edge + 1, :] for c in chunks] for hh in heads]

    def bcast_rows(ref, s):
        return jnp.concatenate(
            [jnp.broadcast_to(ref[blk * DIAG + s:blk * DIAG + s + 1, :], (DIAG, HG_DK))
             for blk in range(TQ // DIAG)], axis=0)

    pair_lhs = []
    for hh in heads:
        cols = [(q[hh] * jnp.exp2(jnp.minimum(cum[hh] - bcast_rows(cum_refs[hh], s), 0.0))
                 * bcast_rows(k_refs[hh], s)).astype(BF16) for s in range(DIAG)]
        pair_lhs.append(jnp.concatenate(cols, axis=1))

    def span_operands(hh, c):
        base = c * HCHUNK
        lhs, rhs = [], []
        for hs in LEVELS:
            for blk in range(HCHUNK // (2 * hs)):
                lo, mid, hi = blk * 2 * hs, blk * 2 * hs + hs, (blk + 1) * 2 * hs
                k_rows, q_rows = ((mid, hi), (lo, mid)) if reverse else ((lo, mid), (mid, hi))
                ref_row = base + (mid if reverse else mid - 1)
                rr = cum_refs[hh][ref_row:ref_row + 1, :]
                qr = slice(base + q_rows[0], base + q_rows[1])
                kr = slice(base + k_rows[0], base + k_rows[1])
                qs = q[hh][qr, :] * jnp.exp2(cum[hh][qr, :] - rr)
                ks = k[hh][kr, :] * jnp.exp2(rr - cum[hh][kr, :])
                zq = [jnp.zeros((q_rows[0], HG_DK), F32), qs, jnp.zeros((HCHUNK - q_rows[1], HG_DK), F32)]
                zk = [jnp.zeros((k_rows[0], HG_DK), F32), ks, jnp.zeros((HCHUNK - k_rows[1], HG_DK), F32)]
                lhs.append(jnp.concatenate([p_ for p_ in zq if p_.shape[0]], axis=0))
                rhs.append(jnp.concatenate([p_ for p_ in zk if p_.shape[0]], axis=0))
        return jnp.concatenate(lhs, axis=1).astype(BF16), jnp.concatenate(rhs, axis=1).astype(BF16)

    spans = [[span_operands(hh, c) for c in chunks] for hh in heads]
    qe = [(q[hh] * jnp.exp2(cum[hh])).astype(BF16) for hh in heads]
    ke = [[(k[hh][crow(c), :] * jnp.exp2(tot[hh][c] - cum[hh][crow(c), :])).astype(BF16) for c in chunks]
          for hh in heads]

    pair = [jnp.dot(pair_lhs[hh], sel, preferred_element_type=F32) for hh in heads]
    a_off = [[_nt_dot(*spans[hh][c]) for c in chunks] for hh in heads]
    upd = [[lax.dot_general(v[hh][crow(c), :], ke[hh][c], (((0,), (0,)), ((), ())),
                            preferred_element_type=F32) for c in chunks] for hh in heads]
    a = [[(a_off[hh][c] + pair[hh][crow(c), :HCHUNK] * diag_mask).astype(BF16) for c in chunks] for hh in heads]
    o_intra = [[jnp.dot(a[hh][c], v[hh][crow(c), :], preferred_element_type=F32) for c in chunks] for hh in heads]

    state = [s_refs[hh][...] for hh in heads]
    for c in (reversed(chunks) if reverse else chunks):
        inter = [_nt_dot(qe[hh][crow(c), :], state[hh].astype(BF16)) for hh in heads]
        for hh in heads:
            o_ref[crow(c), lanes(hh)] = o_intra[hh][c] + inter[hh]
            state[hh] = state[hh] * jnp.exp2(tot[hh][c]) + upd[hh][c]
    for hh in heads:
        s_refs[hh][...] = state[hh]


def _hgrn(pa, pz, lb_logits_dir, layer, reverse):
    def tile_row(b, j):
        lat = b * QT_SEQ + (QT_SEQ - j if reverse else j - 1)
        return jnp.where(j == 0, LAT // TQ + b, lat)

    def spec(col):
        return pl.BlockSpec((TQ, HG_W), lambda b, j: (tile_row(b, j), col))

    return pl.pallas_call(
        functools.partial(_hgrn_kernel, reverse=reverse, layer=layer),
        grid=(BATCH, QT_SEQ + 1),
        in_specs=[spec(OFF_HQ // HG_W), spec(OFF_HI // HG_W), spec(1 if reverse else 0),
                  pl.BlockSpec((DEPTH, HG_W), lambda b, j: (0, 0))],
        out_specs=spec(0),
        out_shape=jax.ShapeDtypeStruct((M, HG_W), F32),
        scratch_shapes=([pltpu.VMEM((TQ, HG_DK), F32)] * (2 * HG_HEADS)
                        + [pltpu.VMEM((HG_DK, HG_DK), F32)] * HG_HEADS),
        compiler_params=_cparams(("parallel", "arbitrary"), 32),
        name="hgrn_bwd" if reverse else "hgrn_fwd",
    )(pa, pa, pz, lb_logits_dir)


def _outproj_kernel(x_ref, gt_ref, oa_ref, of_ref, ob_ref, hg_ref, om_ref, ng_ref, w_ref, o_ref, lhs_ref):
    a_w = GQA_HEADS * HEAD_DIM
    lhs_ref[:, 0:a_w] = oa_ref[...]
    lhs_ref[:, a_w + HG_W:] = om_ref[...]
    gain = ng_ref[...]
    for hh in range(HG_HEADS):
        sl = slice(hh * HG_DK, (hh + 1) * HG_DK)
        g = hg_ref[:, sl].astype(F32)
        y = _head_norm(of_ref[:, sl] + ob_ref[:, sl], gain) * (g * _sigmoid(g))
        lhs_ref[:, a_w + hh * HG_DK:a_w + (hh + 1) * HG_DK] = y.astype(BF16)

    lhs = lhs_ref[...]
    gate = _mod_vec(gt_ref, pl.program_id(0), TMP)
    for n in range(D // PROJ_TN):
        cols = slice(n * PROJ_TN, (n + 1) * PROJ_TN)
        y = jnp.dot(lhs, w_ref[n], preferred_element_type=F32)
        o_ref[:, cols] = x_ref[:, cols] + gate[:, cols] * y


def _outproj(x, mods, o_a, o_f, o_b, pb, o_m, norm_gain, w_tiles, layer, rows):
    def row_spec(w, col=0):
        return pl.BlockSpec((TMP, w), lambda i: (i, col))

    return pl.pallas_call(
        _outproj_kernel,
        grid=(rows // TMP,),
        in_specs=[
            row_spec(D),
            pl.BlockSpec((None, SUBLANES, D), lambda i: (layer, 0, 5)),
            row_spec(GQA_HEADS * HEAD_DIM), row_spec(HG_W), row_spec(HG_W),
            row_spec(HG_W, 0),
            row_spec(MLA_HEADS * MLA_V),
            pl.BlockSpec((1, HG_DK), lambda i: (0, 0)),
            pl.BlockSpec((None, D // PROJ_TN, D, PROJ_TN), lambda i: (layer, 0, 0, 0),
                         pipeline_mode=pl.Buffered(1)),
        ],
        out_specs=row_spec(D),
        out_shape=jax.ShapeDtypeStruct((rows, D), F32),
        scratch_shapes=[pltpu.VMEM((TMP, D), BF16)],
        compiler_params=_cparams(("parallel",), 56),
        name="outproj",
    )(x, mods, o_a, o_f, o_b, pb, o_m, norm_gain.reshape(1, HG_DK), w_tiles)


def _pad_w_uq(w_uq):
    w = w_uq.reshape(DEPTH, MLA_Q_RANK, MLA_HEADS, MLA_QK)
    w = jnp.pad(w, ((0, 0), (0, 0), (0, 0), (0, MLA_PAD - MLA_QK)))
    return w.reshape(DEPTH, MLA_Q_RANK, MLA_HEADS * MLA_PAD).astype(BF16)


def kernel(x, c, ctx, c_ctx, w_mod, b_mod, w_ffn1_in, w_ffn1_out, w_in, w_uq, w_ukv, w_out,
           gqa_q_gain, gqa_k_gain, mla_q_gain, mla_kv_gain, hgrn_lb_logits, hgrn_norm_gain,
           w_ffn2_in, w_ffn2_out, final_gain):
    cvec8 = jnp.concatenate([c, c_ctx[None, :], jnp.zeros((SUBLANES - BATCH - 1, D), F32)], axis=0)
    mods = _modulation(cvec8, w_mod, b_mod)
    tabs_h = _rope_tables(HEAD_DIM)
    tabs_r = _rope_tables(MLA_ROPE)
    ffn_in = _ffn_win_tiles(w_ffn1_in, 0)
    w_in_rows = jnp.transpose(w_in, (0, 2, 1))
    w_in_t = _inproj_weight_tiles(w_in_rows)
    w_out_t = _proj_tiles(w_out, D)
    w_kr = jnp.pad(w_in_rows[:, OFF_KR:, :], ((0, 0), (0, LANES - MLA_ROPE), (0, 0))).astype(BF16)
    w_uq_pad = _pad_w_uq(w_uq)
    w_ukv_b = w_ukv.astype(BF16)

    xa = None
    for l in range(DEPTH):
        last = l == DEPTH - 1
        if l == 0:
            xa = _ffn(x.reshape(LAT, D), mods, ffn_in, w_ffn1_out, l, 0, ctx=ctx.reshape(CTXR, D))
        else:
            xa = _ffn(xa, mods, ffn_in, w_ffn1_out, l, 0)
        pa, pz, pb, pkr = _inproj(xa, mods, w_in_t, w_kr, l)
        q_a, k_a, v_a = _gqa_prep(pa, gqa_q_gain[l], gqa_k_gain[l], tabs_h)
        o_a, ffn2_in = _gqa_attn(q_a, k_a, v_a, not last, cast=(w_ffn2_in, l))
        q_m, k_m, v_m = _mla_prep(pb, pkr, mla_q_gain[l], mla_kv_gain[l], w_uq_pad[l], w_ukv_b[l], tabs_r)
        if last:
            o_m = _mla_attn(q_m, k_m, v_m, False)
        else:
            o_m, ffn_in = _mla_attn(q_m, k_m, v_m, True, cast=(w_ffn1_in, l + 1))
        o_f = _hgrn(pa, pz, hgrn_lb_logits[0], l, False)
        o_b = _hgrn(pa, pz, hgrn_lb_logits[1], l, True)
        xa = _outproj(xa, mods, o_a, o_f, o_b, pb, o_m, hgrn_norm_gain[l], w_out_t, l, LAT if last else M)
        xa = _ffn(xa, mods, ffn2_in, w_ffn2_out, l, 6, final_gain=final_gain if last else None)
    return xa
```

```python
import functools

import jax
import jax.numpy as jnp
from jax import lax
from jax.experimental import pallas as pl
from jax.experimental.pallas import tpu as pltpu

F32 = jnp.float32
BF16 = jnp.bfloat16

D = 2048
BATCH = 4
SEQ = 2048
DEPTH = 2
GRID_W = 64
CTX = 256
EPS = 1e-6
ROPE_THETA = 10000.0

HEAD_DIM = 128
GQA_HEADS = 8
GQA_KV = 2
GQA_GROUP = GQA_HEADS // GQA_KV
HG_HEADS = 4
HG_DK = 128
HG_W = HG_HEADS * HG_DK
MLA_HEADS = 4
MLA_Q_RANK = 512
MLA_KV_RANK = 256
MLA_NOPE = 128
MLA_ROPE = 64
MLA_V = 128
MLA_QK = MLA_NOPE + MLA_ROPE
MLA_PAD = 256
FFN_H = 5504
N_MOD = 9

LAT = BATCH * SEQ
CTXR = BATCH * CTX
M = LAT + CTXR
OFF_GQ, OFF_GK, OFF_GV = 0, 1024, 1280
OFF_HQ, OFF_HI, OFF_HF, OFF_HB, OFF_HG = 1536, 2048, 2560, 3072, 3584
OFF_CQ, OFF_CKV, OFF_KR = 4096, 4608, 4864
IN_MAIN = OFF_KR
PA_W, PZ_W, PB_W = OFF_HF, OFF_HG - OFF_HF, OFF_KR - OFF_HG

LANES = 128
SUBLANES = 8
MXU_W = 256

TM = 1024
TMP = 512
ROWC = 64
TQ = 256
TP = 1024
QT_SEQ = SEQ // TQ
HCHUNK = 64
FFN_TH = 512
FFN_HP = -(-FFN_H // FFN_TH) * FFN_TH
FFN_STEPS = FFN_HP // FFN_TH
MOD_TN = 1024
PROJ_TN = MXU_W
PREP_ROWS = 128
WO_SLAB = 128
CAST_ROWS = 64
V_AUG = MXU_W
LOG2E = 1.4426950408889634


def _cparams(sem, vmem_mb):
    return pltpu.CompilerParams(dimension_semantics=sem, vmem_limit_bytes=vmem_mb * 1024 * 1024)


def _sigmoid(x):
    return 1.0 / (1.0 + jnp.exp(-x))


def _nt_dot(a, b):
    return lax.dot_general(a, b, (((1,), (1,)), ((), ())), preferred_element_type=F32)


def _mod_kernel(c_ref, w_ref, b_ref, o_ref):
    c = c_ref[...]
    a = (c * _sigmoid(c)).astype(BF16)
    o_ref[...] = jnp.dot(a, w_ref[...].astype(BF16), preferred_element_type=F32) + b_ref[...]


def _modulation(cvec8, w_mod, b_mod):
    n = N_MOD * D
    return pl.pallas_call(
        _mod_kernel,
        grid=(DEPTH, n // MOD_TN),
        in_specs=[
            pl.BlockSpec((SUBLANES, D), lambda l, j: (0, 0)),
            pl.BlockSpec((None, D, MOD_TN), lambda l, j: (l, 0, j)),
            pl.BlockSpec((None, 1, MOD_TN), lambda l, j: (l, 0, j)),
        ],
        out_specs=pl.BlockSpec((None, SUBLANES, MOD_TN), lambda l, j: (l, 0, j)),
        out_shape=jax.ShapeDtypeStruct((DEPTH, SUBLANES, n), F32),
        compiler_params=_cparams(("parallel", "parallel"), 40),
        name="modulation",
    )(cvec8, w_mod, b_mod.reshape(DEPTH, 1, n))


def _mod_vec(ref, tile, tm):
    row = jnp.where(tile < LAT // tm, tile // (SEQ // tm), BATCH)
    return ref[pl.ds(row, 1), :]


def _modulate_tile(x_ref, shift_ref, scale_ref, h_ref, tile, tm):
    sh = _mod_vec(shift_ref, tile, tm)
    sc = 1.0 + _mod_vec(scale_ref, tile, tm)

    def body(c, carry):
        r0 = pl.multiple_of(c * ROWC, ROWC)
        xc = x_ref[pl.ds(r0, ROWC), :]
        ms = jnp.mean(xc * xc, axis=-1, keepdims=True)
        h_ref[pl.ds(r0, ROWC), :] = (xc * lax.rsqrt(ms + EPS) * sc + sh).astype(BF16)
        return carry

    lax.fori_loop(0, tm // ROWC, body, 0, unroll=2)


def _ffn_kernel(*refs, two_inputs, final):
    refs = list(refs)
    x_ref = refs.pop(0)
    c_ref = refs.pop(0) if two_inputs else None
    sh_ref, sc_ref, gt_ref = refs.pop(0), refs.pop(0), refs.pop(0)
    fg_ref = refs.pop(0) if final else None
    wi_ref = refs.pop(0)
    wo_refs = [refs.pop(0) for _ in range(FFN_TH // WO_SLAB)]
    o_ref, h_ref = refs
    i = pl.program_id(0)
    j = pl.program_id(1)

    def per_source(fn):
        if two_inputs:
            pl.when(i < LAT // TM)(lambda: fn(x_ref))
            pl.when(i >= LAT // TM)(lambda: fn(c_ref))
        else:
            fn(x_ref)

    @pl.when(j == 0)
    def _():
        per_source(lambda src: _modulate_tile(src, sh_ref, sc_ref, h_ref, i, TM))
        o_ref[...] = jnp.zeros_like(o_ref)

    pass_rows = TM // 2 if two_inputs else TM
    wo = jnp.concatenate([r_[...].astype(BF16) for r_ in wo_refs], axis=0)
    for rc in range(TM // pass_rows):
        rows = slice(rc * pass_rows, (rc + 1) * pass_rows)
        r = jnp.dot(h_ref[rows, :], wi_ref[...], preferred_element_type=F32)
        g = r[:, :FFN_TH]
        u = r[:, FFN_TH:]
        a = (g * _sigmoid(g) * u).astype(BF16)
        o_ref[rows, :] += jnp.dot(a, wo, preferred_element_type=F32)

    @pl.when(j == FFN_STEPS - 1)
    def _():
        gate = 0.5 * _mod_vec(gt_ref, i, TM)

        def epilogue(src):
            def body(c, carry):
                r0 = pl.multiple_of(c * ROWC, ROWC)
                y = src[pl.ds(r0, ROWC), :] + gate * o_ref[pl.ds(r0, ROWC), :]
                if final:
                    y = y * lax.rsqrt(jnp.mean(y * y, axis=-1, keepdims=True) + EPS) * fg_ref[...]
                o_ref[pl.ds(r0, ROWC), :] = y
                return carry

            lax.fori_loop(0, TM // ROWC, body, 0)

        per_source(epilogue)


def _ffn(x, mods, wi, wo, layer, mod0, ctx=None, final_gain=None):
    two_inputs, final = ctx is not None, final_gain is not None
    rows = M if two_inputs else x.shape[0]
    lat_tiles = LAT // TM

    def mod_spec(k):
        return pl.BlockSpec((None, SUBLANES, D), lambda i, j: (layer, 0, mod0 + k))

    in_specs = [pl.BlockSpec((TM, D), lambda i, j: (jnp.minimum(i, lat_tiles - 1) if two_inputs else i, 0),
                             pipeline_mode=pl.Buffered(1))]
    args = [x]
    if two_inputs:
        in_specs.append(pl.BlockSpec((TM, D), lambda i, j: (0, 0), pipeline_mode=pl.Buffered(1)))
        args.append(ctx)
    in_specs += [mod_spec(0), mod_spec(1), mod_spec(2)]
    args += [mods, mods, mods]
    if final:
        in_specs.append(pl.BlockSpec((1, D), lambda i, j: (0, 0)))
        args.append(final_gain.reshape(1, D))
        out_spec = pl.BlockSpec((None, TM, D), lambda i, j: (i // (SEQ // TM), i % (SEQ // TM), 0))
        out_shape = jax.ShapeDtypeStruct((BATCH, SEQ, D), F32)
    else:
        out_spec = pl.BlockSpec((TM, D), lambda i, j: (i, 0))
        out_shape = jax.ShapeDtypeStruct((rows, D), F32)
    in_specs.append(pl.BlockSpec((None, None, D, 2 * FFN_TH), lambda i, j: (0, j, 0, 0)))
    args.append(wi)
    slabs, last_slab = FFN_TH // WO_SLAB, FFN_H // WO_SLAB - 1
    for r in range(slabs):
        in_specs.append(pl.BlockSpec(
            (None, WO_SLAB, D),
            functools.partial(lambda i, j, r: (layer, jnp.minimum(j * slabs + r, last_slab), 0), r=r)))
        args.append(wo)
    return pl.pallas_call(
        functools.partial(_ffn_kernel, two_inputs=two_inputs, final=final),
        grid=(rows // TM, FFN_STEPS),
        in_specs=in_specs,
        out_specs=out_spec,
        out_shape=out_shape,
        scratch_shapes=[pltpu.VMEM((TM, D), BF16)],
        compiler_params=_cparams(("parallel", "arbitrary"), 58),
        name="ffn",
    )(*args)


def _ffn_win_tiles_body(w_ref, o_ref, rows):
    for j in range(FFN_STEPS):
        valid = min(FFN_TH, FFN_H - j * FFN_TH)
        for part, base in ((0, 0), (1, FFN_H)):
            c0 = part * FFN_TH
            o_ref[j, :, c0:c0 + valid] = w_ref[:, base + j * FFN_TH:base + j * FFN_TH + valid].astype(BF16)
            if valid < FFN_TH:
                o_ref[j, :, c0 + valid:c0 + FFN_TH] = jnp.zeros((rows, FFN_TH - valid), BF16)


def _ffn_win_tiles(w_in, layer):
    return pl.pallas_call(
        functools.partial(_ffn_win_tiles_body, rows=PREP_ROWS),
        grid=(D // PREP_ROWS,),
        in_specs=[pl.BlockSpec((None, PREP_ROWS, 2 * FFN_H), lambda r: (layer, r, 0))],
        out_specs=pl.BlockSpec((None, FFN_STEPS, PREP_ROWS, 2 * FFN_TH), lambda r: (0, 0, r, 0)),
        out_shape=jax.ShapeDtypeStruct((1, FFN_STEPS, D, 2 * FFN_TH), BF16),
        compiler_params=_cparams(("parallel",), 32),
        name="ffn_win_tiles",
    )(w_in)


def _proj_tiles_kernel(w_ref, o_ref):
    for t in range(o_ref.shape[0]):
        o_ref[t] = w_ref[:, t * PROJ_TN:(t + 1) * PROJ_TN].astype(BF16)


def _proj_tiles(w, layer):
    return pl.pallas_call(
        _proj_tiles_kernel,
        grid=(D // PREP_ROWS,),
        in_specs=[pl.BlockSpec((None, PREP_ROWS, D), lambda r: (layer, r, 0))],
        out_specs=pl.BlockSpec((None, D // PROJ_TN, PREP_ROWS, PROJ_TN), lambda r: (0, 0, r, 0)),
        out_shape=jax.ShapeDtypeStruct((1, D // PROJ_TN, D, PROJ_TN), BF16),
        compiler_params=_cparams(("parallel",), 32),
        name="proj_tiles",
    )(w)


def _cast_rows_kernel(w_ref, o_ref):
    o_ref[...] = w_ref[...].astype(BF16)


def _inproj_weight_tiles(w_in_t, layer):
    n_tiles = IN_MAIN // PROJ_TN
    return pl.pallas_call(
        _cast_rows_kernel,
        grid=(n_tiles,),
        in_specs=[pl.BlockSpec((None, PROJ_TN, D), lambda t: (layer, t, 0))],
        out_specs=pl.BlockSpec((None, None, PROJ_TN, D), lambda t: (0, t, 0, 0)),
        out_shape=jax.ShapeDtypeStruct((1, n_tiles, PROJ_TN, D), BF16),
        compiler_params=_cparams(("parallel",), 32),
        name="inproj_weight_tiles",
    )(w_in_t)


def _inproj_kernel(x_ref, sh_ref, sc_ref, w_ref, wkr_ref, pa_ref, pz_ref, pb_ref, kr_ref, h_ref):
    _modulate_tile(x_ref, sh_ref, sc_ref, h_ref, pl.program_id(0), TMP)
    h = h_ref[...]
    kr_ref[...] = _nt_dot(h, wkr_ref[...])
    for t in range(IN_MAIN // PROJ_TN):
        y = _nt_dot(h, w_ref[t])
        c0 = t * PROJ_TN
        if c0 < OFF_HF:
            pa_ref[:, c0:c0 + PROJ_TN] = y.astype(BF16)
        elif c0 < OFF_HG:
            pz_ref[:, c0 - OFF_HF:c0 - OFF_HF + PROJ_TN] = y
        else:
            pb_ref[:, c0 - OFF_HG:c0 - OFF_HG + PROJ_TN] = y.astype(BF16)


def _inproj(x, mods, w_tiles, w_kr, layer):
    def mod_spec(k):
        return pl.BlockSpec((None, SUBLANES, D), lambda i: (layer, 0, k))

    def out_spec(w):
        return pl.BlockSpec((TMP, w), lambda i: (i, 0))

    n_tiles = IN_MAIN // PROJ_TN
    return pl.pallas_call(
        _inproj_kernel,
        grid=(M // TMP,),
        in_specs=[
            pl.BlockSpec((TMP, D), lambda i: (i, 0)),
            mod_spec(3), mod_spec(4),
            pl.BlockSpec((None, n_tiles, PROJ_TN, D), lambda i: (0, 0, 0, 0), pipeline_mode=pl.Buffered(1)),
            pl.BlockSpec((None, LANES, D), lambda i: (layer, 0, 0)),
        ],
        out_specs=[out_spec(PA_W), out_spec(PZ_W), out_spec(PB_W), out_spec(LANES)],
        out_shape=[jax.ShapeDtypeStruct((M, PA_W), BF16), jax.ShapeDtypeStruct((M, PZ_W), F32),
                   jax.ShapeDtypeStruct((M, PB_W), BF16), jax.ShapeDtypeStruct((M, LANES), F32)],
        scratch_shapes=[pltpu.VMEM((TMP, D), BF16)],
        compiler_params=_cparams(("parallel",), 56),
        name="inproj",
    )(x, mods, mods, w_tiles, w_kr)


def _rope_tables(rot_dim):
    rows = SEQ // GRID_W
    row = jnp.repeat(jnp.arange(rows, dtype=F32), GRID_W)
    colp = jnp.tile(jnp.arange(GRID_W, dtype=F32), rows)
    axis_dim = rot_dim // 2
    inv_freq = ROPE_THETA ** (-jnp.arange(0, axis_dim, 2, dtype=F32) / axis_dim)
    ang_r = row[:, None] * inv_freq
    ang_c = colp[:, None] * inv_freq
    ang = jnp.concatenate([ang_r, ang_r, ang_c, ang_c], axis=-1)
    cos, sin = jnp.cos(ang), jnp.sin(ang)
    quarter = rot_dim // 4
    lane = jnp.arange(rot_dim)
    first = (lane % (2 * quarter)) < quarter
    sin_up = jnp.where(first, -sin, 0.0)
    sin_dn = jnp.where(first, 0.0, sin)
    pad = LANES - rot_dim
    if pad:
        cos = jnp.pad(cos, ((0, 0), (0, pad)), constant_values=1.0)
        sin_up = jnp.pad(sin_up, ((0, 0), (0, pad)))
        sin_dn = jnp.pad(sin_dn, ((0, 0), (0, pad)))
    return cos, sin_up, sin_dn


def _head_norm(x, gain):
    return x * lax.rsqrt(jnp.mean(x * x, axis=-1, keepdims=True) + EPS) * gain


def _lane_shift_matrix(quarter):
    j = lax.broadcasted_iota(jnp.int32, (LANES, 2 * LANES), 0)
    i = lax.broadcasted_iota(jnp.int32, (LANES, 2 * LANES), 1)
    source = jnp.where(i < LANES, i + quarter, i - LANES - quarter)
    return jnp.where(j == source, 1.0, 0.0).astype(BF16)


def _rope_mxu(x, shifts, cos, sin_up, sin_dn):
    r = jnp.dot(x.astype(BF16), shifts, preferred_element_type=F32)
    return x * cos + r[:, :LANES] * sin_up + r[:, LANES:] * sin_dn


def _head_norm_mxu(x, gain):
    mean_w = jnp.full((LANES, LANES), 1.0 / LANES, BF16)
    ms = jnp.dot((x * x).astype(BF16), mean_w, preferred_element_type=F32)
    return x * lax.rsqrt(ms + EPS) * gain


def _tile_is_ctx(t):
    return t >= LAT // TP


def _rope_idx(t):
    return jnp.where(_tile_is_ctx(t), 0, t % (SEQ // TP))


def _gqa_prep_kernel(p_ref, qg_ref, kg_ref, cos_ref, su_ref, sd_ref, q_ref, k_ref, v_ref):
    is_ctx = _tile_is_ctx(pl.program_id(0))
    cos, su, sd = cos_ref[...], su_ref[...], sd_ref[...]
    scale = HEAD_DIM ** -0.5 * LOG2E
    for kk in range(GQA_KV):
        v_ref[:, kk * V_AUG:kk * V_AUG + HEAD_DIM] = p_ref[:, OFF_GV + kk * HEAD_DIM:OFF_GV + (kk + 1) * HEAD_DIM]
        v_ref[:, kk * V_AUG + HEAD_DIM:(kk + 1) * V_AUG] = jnp.ones((TP, V_AUG - HEAD_DIM), BF16)
    shifts = _lane_shift_matrix(HEAD_DIM // 4)
    for hh in range(GQA_HEADS + GQA_KV):
        xh = p_ref[:, hh * HEAD_DIM:(hh + 1) * HEAD_DIM].astype(F32)
        is_q = hh < GQA_HEADS
        n = _head_norm_mxu(xh, qg_ref[...] if is_q else kg_ref[...])
        y = jnp.where(is_ctx, n, _rope_mxu(n, shifts, cos, su, sd))
        if is_q:
            q_ref[:, hh * HEAD_DIM:(hh + 1) * HEAD_DIM] = (y * scale).astype(BF16)
        else:
            kk = hh - GQA_HEADS
            k_ref[:, kk * HEAD_DIM:(kk + 1) * HEAD_DIM] = y.astype(BF16)


def _attn_kernel(q_ref, kl_ref, kc_ref, vl_ref, vc_ref, *rest, n_heads, head_cols, with_ctx):
    o_ref = rest[-1] if len(rest) == 1 else rest[1]
    if len(rest) == 3:
        step = pl.program_id(0) * pl.num_programs(1) + pl.program_id(1)
        pl.when(step < D // CAST_ROWS)(lambda: _ffn_win_tiles_body(rest[0], rest[2], CAST_ROWS))

    def finish(pv, os_):
        e_w = os_.stop - os_.start
        o_ref[:, os_] = (pv[:, :e_w] / pv[:, e_w:2 * e_w]).astype(BF16)

    def latent_queries():
        for hh in range(n_heads):
            qs, ks, vs, os_ = head_cols(hh)
            q = q_ref[:, qs]
            s = jnp.concatenate([_nt_dot(q, kl_ref[:, ks]), _nt_dot(q, kc_ref[:, ks])], axis=1)
            eb = jnp.exp2(s - jnp.max(s, axis=-1, keepdims=True)).astype(BF16)
            finish(jnp.dot(eb[:, :SEQ], vl_ref[:, vs], preferred_element_type=F32)
                   + jnp.dot(eb[:, SEQ:], vc_ref[:, vs], preferred_element_type=F32), os_)

    def context_queries():
        for hh in range(n_heads):
            qs, ks, vs, os_ = head_cols(hh)
            s = _nt_dot(q_ref[:, qs], kc_ref[:, ks])
            eb = jnp.exp2(s - jnp.max(s, axis=-1, keepdims=True)).astype(BF16)
            finish(jnp.dot(eb, vc_ref[:, vs], preferred_element_type=F32), os_)

    if with_ctx:
        pl.when(pl.program_id(1) < QT_SEQ)(latent_queries)
        pl.when(pl.program_id(1) == QT_SEQ)(context_queries)
    else:
        latent_queries()


def _attention(name, q, k, v, v_col, qw, kw, vw, ow, n_heads, head_cols, with_ctx, cast=None):
    nq = QT_SEQ + (1 if with_ctx else 0)

    def q_row(b, i):
        return jnp.where(i < QT_SEQ, b * QT_SEQ + i, LAT // TQ + b)

    def ctx_row(b, i):
        return LAT // CTX + b

    in_specs = [
        pl.BlockSpec((TQ, qw), lambda b, i: (q_row(b, i), 0)),
        pl.BlockSpec((SEQ, kw), lambda b, i: (b, 0)),
        pl.BlockSpec((CTX, kw), lambda b, i: (ctx_row(b, i), 0)),
        pl.BlockSpec((SEQ, vw), lambda b, i: (b, v_col)),
        pl.BlockSpec((CTX, vw), lambda b, i: (ctx_row(b, i), v_col)),
    ]
    args = [q, k, k, v, v]
    out_specs = pl.BlockSpec((TQ, ow), lambda b, i: (q_row(b, i), 0))
    out_shape = jax.ShapeDtypeStruct((M if with_ctx else LAT, ow), BF16)
    if cast is not None:
        w_in, layer = cast

        def panel(b, i):
            return jnp.minimum(b * nq + i, D // CAST_ROWS - 1)

        in_specs.append(pl.BlockSpec((None, CAST_ROWS, 2 * FFN_H), lambda b, i: (layer, panel(b, i), 0)))
        args.append(w_in)
        out_specs = [out_specs,
                     pl.BlockSpec((None, FFN_STEPS, CAST_ROWS, 2 * FFN_TH), lambda b, i: (0, 0, panel(b, i), 0))]
        out_shape = [out_shape, jax.ShapeDtypeStruct((1, FFN_STEPS, D, 2 * FFN_TH), BF16)]
    return pl.pallas_call(
        functools.partial(_attn_kernel, n_heads=n_heads, head_cols=head_cols, with_ctx=with_ctx),
        grid=(BATCH, nq),
        in_specs=in_specs,
        out_specs=out_specs,
        out_shape=out_shape,
        compiler_params=_cparams(("arbitrary", "arbitrary"), 48),
        name=name,
    )(*args)


def _gqa_head_cols(hh):
    kk = hh // GQA_GROUP
    kv = slice(kk * HEAD_DIM, (kk + 1) * HEAD_DIM)
    hs = slice(hh * HEAD_DIM, (hh + 1) * HEAD_DIM)
    return hs, kv, slice(kk * V_AUG, (kk + 1) * V_AUG), hs


def _mla_head_cols(hh):
    qs = slice(hh * MLA_PAD, (hh + 1) * MLA_PAD)
    return qs, qs, slice(hh * V_AUG, (hh + 1) * V_AUG), slice(hh * MLA_V, (hh + 1) * MLA_V)


def _gqa_attn(q, k, v_aug, with_ctx, cast=None):
    qw, kw = GQA_HEADS * HEAD_DIM, GQA_KV * HEAD_DIM
    return _attention("gqa_attn", q, k, v_aug, 0, qw, kw, GQA_KV * V_AUG, qw, GQA_HEADS, _gqa_head_cols, with_ctx,
                      cast)


def _mla_attn(q, k, v_aug, with_ctx, cast=None):
    qw = MLA_HEADS * MLA_PAD
    return _attention("mla_attn", q, k, v_aug, 0, qw, qw, MLA_HEADS * V_AUG, MLA_HEADS * MLA_V, MLA_HEADS,
                      _mla_head_cols, with_ctx, cast)


def _mla_prep_kernel(cq_ref, ckv_ref, kr_ref, qg_ref, kvg_ref, wuq_ref, wukv_ref,
                     cos_ref, su_ref, sd_ref, q_ref, k_ref, v_ref):
    is_ctx = _tile_is_ctx(pl.program_id(0))
    cos, su, sd = cos_ref[...], su_ref[...], sd_ref[...]
    quarter = MLA_ROPE // 4
    scale = MLA_QK ** -0.5 * LOG2E

    cq = _head_norm(cq_ref[...].astype(F32), qg_ref[...]).astype(BF16)
    qf = jnp.dot(cq, wuq_ref[...], preferred_element_type=F32)
    ckv = _head_norm(ckv_ref[...].astype(F32), kvg_ref[...]).astype(BF16)
    kvf = jnp.dot(ckv, wukv_ref[...], preferred_element_type=F32)
    kr = kr_ref[...]
    shifts = _lane_shift_matrix(quarter)
    kr = jnp.where(is_ctx, kr, _rope_mxu(kr, shifts, cos, su, sd)).astype(BF16)
    for hh in range(MLA_HEADS):
        base = hh * MLA_PAD
        q_ref[:, base:base + MLA_NOPE] = (qf[:, base:base + MLA_NOPE] * scale).astype(BF16)
        qr = qf[:, base + MLA_NOPE:base + MLA_PAD]
        qr = jnp.where(is_ctx, qr, _rope_mxu(qr, shifts, cos, su, sd))
        q_ref[:, base + MLA_NOPE:base + MLA_PAD] = (qr * scale).astype(BF16)
        k_ref[:, base:base + MLA_NOPE] = kvf[:, base:base + MLA_NOPE].astype(BF16)
        k_ref[:, base + MLA_NOPE:base + MLA_PAD] = kr
        v_ref[:, hh * V_AUG:hh * V_AUG + MLA_V] = kvf[:, base + MLA_NOPE:base + MLA_PAD].astype(BF16)
        v_ref[:, hh * V_AUG + MLA_V:(hh + 1) * V_AUG] = jnp.ones((TP, V_AUG - MLA_V), BF16)


N_GQA_PREP_IN = 6
N_MLA_PREP_IN = 10


def _qkv_prep_kernel(*refs):
    gqa_in, rest = refs[:N_GQA_PREP_IN], refs[N_GQA_PREP_IN:]
    mla_in, outs = rest[:N_MLA_PREP_IN], rest[N_MLA_PREP_IN:]
    _gqa_prep_kernel(*gqa_in, *outs[:3])
    _mla_prep_kernel(*mla_in, *outs[3:])


def _qkv_prep(pa, pb, pkr, gqa_q_gain, gqa_k_gain, mla_q_gain, mla_kv_gain, w_uq_pad, w_ukv, tabs_h, tabs_r):
    tab_spec = pl.BlockSpec((TP, LANES), lambda t: (_rope_idx(t), 0))
    qk_w = MLA_HEADS * MLA_PAD

    def rows(w, col=0):
        return pl.BlockSpec((TP, w), lambda t: (t, col))

    def whole(r, c):
        return pl.BlockSpec((r, c), lambda t: (0, 0))

    widths = [GQA_HEADS * HEAD_DIM, GQA_KV * HEAD_DIM, GQA_KV * V_AUG, qk_w, qk_w, MLA_HEADS * V_AUG]
    return pl.pallas_call(
        _qkv_prep_kernel,
        grid=(M // TP,),
        in_specs=[
            rows(OFF_HQ), whole(1, HEAD_DIM), whole(1, HEAD_DIM), tab_spec, tab_spec, tab_spec,
            rows(MLA_Q_RANK, (OFF_CQ - OFF_HG) // MLA_Q_RANK), rows(MLA_KV_RANK, (OFF_CKV - OFF_HG) // MLA_KV_RANK),
            rows(LANES), whole(1, MLA_Q_RANK), whole(1, MLA_KV_RANK),
            whole(MLA_Q_RANK, qk_w), whole(MLA_KV_RANK, qk_w), tab_spec, tab_spec, tab_spec,
        ],
        out_specs=[rows(w) for w in widths],
        out_shape=[jax.ShapeDtypeStruct((M, w), BF16) for w in widths],
        compiler_params=_cparams(("parallel",), 48),
        name="qkv_prep",
    )(pa, gqa_q_gain.reshape(1, -1), gqa_k_gain.reshape(1, -1), *tabs_h,
      pb, pb, pkr, mla_q_gain.reshape(1, -1), mla_kv_gain.reshape(1, -1), w_uq_pad, w_ukv, *tabs_r)


TILE_CHUNKS = TQ // HCHUNK
DIAG = 8
LEVELS = (8, 16, 32)


def _split3(x):
    hi = x.astype(BF16)
    r1 = x - hi.astype(F32)
    mid = r1.astype(BF16)
    lo = (r1 - mid.astype(F32)).astype(BF16)
    return hi, mid, lo


def _hgrn_kernel(q_ref, v_ref, z_ref, lg_ref, *rest, reverse, layer, cast):
    if cast is None:
        o_ref, scratch = rest[0], rest[1:]
    else:
        o_ref, scratch = rest[1], rest[3:]
        step = pl.program_id(0) * pl.num_programs(1) + pl.program_id(1)
        pl.when(step < cast[1])(lambda: cast[0](rest[0], rest[2]))
    cum_refs = scratch[0:HG_HEADS]
    k_refs = scratch[HG_HEADS:2 * HG_HEADS]
    s_refs = scratch[2 * HG_HEADS:]

    lg = [lg_ref[l:l + 1, :] for l in range(DEPTH)]
    mx = functools.reduce(jnp.maximum, lg)
    ex = [jnp.exp(r - mx) for r in lg]
    den = functools.reduce(lambda a_, b_: a_ + b_, ex)
    lb_all = jnp.zeros((1, HG_W), F32)
    for l in range(1, layer + 1):
        lb_all = lb_all + ex[l] / den
    log_lb_all = jnp.log(lb_all)
    log_1m_all = jnp.log1p(-lb_all)

    ti = lax.broadcasted_iota(jnp.int32, (TQ, TQ), 0)
    si = lax.broadcasted_iota(jnp.int32, (TQ, TQ), 1)
    causal = (si >= ti) if reverse else (si <= ti)
    tri = (causal & ((ti // HCHUNK) == (si // HCHUNK))).astype(BF16)
    tc = lax.broadcasted_iota(jnp.int32, (HCHUNK, HCHUNK), 0)
    sc = lax.broadcasted_iota(jnp.int32, (HCHUNK, HCHUNK), 1)
    diag_mask = (((sc >= tc) if reverse else (sc <= tc)) & ((tc // DIAG) == (sc // DIAG))).astype(F32)
    ri = lax.broadcasted_iota(jnp.int32, (DIAG * HG_DK, LANES), 0)
    ci = lax.broadcasted_iota(jnp.int32, (DIAG * HG_DK, LANES), 1)
    sel = ((ri // HG_DK) == (ci % DIAG)).astype(BF16)
    edge = 0 if reverse else HCHUNK - 1

    @pl.when(pl.program_id(1) == 0)
    def _():
        for s_ref in s_refs:
            s_ref[...] = jnp.zeros_like(s_ref)

    heads = range(HG_HEADS)

    def lanes(hh):
        return slice(hh * HG_DK, (hh + 1) * HG_DK)

    chunks = range(TILE_CHUNKS)

    def crow(c):
        return slice(c * HCHUNK, (c + 1) * HCHUNK)

    q = [q_ref[:, lanes(hh)].astype(F32) * (HG_DK ** -0.5) for hh in heads]
    v = [v_ref[:, lanes(hh)] for hh in heads]

    z = z_ref[...]
    u = jnp.exp(-jnp.abs(z))
    w = 1.0 + u
    b_ = log_1m_all + (jnp.minimum(z, 0.0) - jnp.log(w))
    log_f = jnp.maximum(log_lb_all, b_) + jnp.log(1.0 + jnp.exp(-jnp.abs(log_lb_all - b_)))
    k_all = (1.0 - lb_all) * (jnp.where(z >= 0.0, u, 1.0) / w)
    k = [k_all[:, lanes(hh)] for hh in heads]

    cum3 = jnp.dot(tri, jnp.concatenate(_split3(log_f), axis=1), preferred_element_type=F32)
    cum_all = (cum3[:, :HG_W] + cum3[:, HG_W:2 * HG_W] + cum3[:, 2 * HG_W:]) * LOG2E
    cum = [cum_all[:, lanes(hh)] for hh in heads]
    for hh in heads:
        cum_refs[hh][...] = cum[hh]
        k_refs[hh][...] = k[hh]
    tot = [[cum_refs[hh][c * HCHUNK + edge:c * HCHUNK + edge + 1, :] for c in chunks] for hh in heads]

    def bcast_rows(ref, s):
        return jnp.concatenate(
            [jnp.broadcast_to(ref[blk * DIAG + s:blk * DIAG + s + 1, :], (DIAG, HG_DK))
             for blk in range(TQ // DIAG)], axis=0)

    pair_lhs = []
    for hh in heads:
        cols = [(q[hh] * jnp.exp2(jnp.minimum(cum[hh] - bcast_rows(cum_refs[hh], s), 0.0))
                 * bcast_rows(k_refs[hh], s)).astype(BF16) for s in range(DIAG)]
        pair_lhs.append(jnp.concatenate(cols, axis=1))

    def span_operands(hh, c):
        base = c * HCHUNK
        lhs, rhs = [], []
        for hs in LEVELS:
            for blk in range(HCHUNK // (2 * hs)):
                lo, mid, hi = blk * 2 * hs, blk * 2 * hs + hs, (blk + 1) * 2 * hs
                k_rows, q_rows = ((mid, hi), (lo, mid)) if reverse else ((lo, mid), (mid, hi))
                ref_row = base + (mid if reverse else mid - 1)
                rr = cum_refs[hh][ref_row:ref_row + 1, :]
                qr = slice(base + q_rows[0], base + q_rows[1])
                kr = slice(base + k_rows[0], base + k_rows[1])
                qs = q[hh][qr, :] * jnp.exp2(cum[hh][qr, :] - rr)
                ks = k[hh][kr, :] * jnp.exp2(rr - cum[hh][kr, :])
                zq = [jnp.zeros((q_rows[0], HG_DK), F32), qs, jnp.zeros((HCHUNK - q_rows[1], HG_DK), F32)]
                zk = [jnp.zeros((k_rows[0], HG_DK), F32), ks, jnp.zeros((HCHUNK - k_rows[1], HG_DK), F32)]
                lhs.append(jnp.concatenate([p_ for p_ in zq if p_.shape[0]], axis=0))
                rhs.append(jnp.concatenate([p_ for p_ in zk if p_.shape[0]], axis=0))
        return jnp.concatenate(lhs, axis=1).astype(BF16), jnp.concatenate(rhs, axis=1).astype(BF16)

    spans = [[span_operands(hh, c) for c in chunks] for hh in heads]
    qe = [(q[hh] * jnp.exp2(cum[hh])).astype(BF16) for hh in heads]
    ke = [[(k[hh][crow(c), :] * jnp.exp2(tot[hh][c] - cum[hh][crow(c), :])).astype(BF16) for c in chunks]
          for hh in heads]

    pair = [jnp.dot(pair_lhs[hh], sel, preferred_element_type=F32) for hh in heads]
    a_off = [[_nt_dot(*spans[hh][c]) for c in chunks] for hh in heads]
    upd = [[lax.dot_general(v[hh][crow(c), :], ke[hh][c], (((0,), (0,)), ((), ())),
                            preferred_element_type=F32) for c in chunks] for hh in heads]
    a = [[(a_off[hh][c] + pair[hh][crow(c), :HCHUNK] * diag_mask).astype(BF16) for c in chunks] for hh in heads]
    o_intra = [[jnp.dot(a[hh][c], v[hh][crow(c), :], preferred_element_type=F32) for c in chunks] for hh in heads]

    state = [s_refs[hh][...] for hh in heads]
    for c in (reversed(chunks) if reverse else chunks):
        inter = [_nt_dot(qe[hh][crow(c), :], state[hh].astype(BF16)) for hh in heads]
        for hh in heads:
            o_ref[crow(c), lanes(hh)] = o_intra[hh][c] + inter[hh]
            state[hh] = state[hh] * jnp.exp2(tot[hh][c]) + upd[hh][c]
    for hh in heads:
        s_refs[hh][...] = state[hh]


def _hgrn(pa, pz, lb_logits_dir, layer, reverse, cast=None):
    n_tiles_row = QT_SEQ + 1

    def tile_row(b, j):
        lat = b * QT_SEQ + (QT_SEQ - j if reverse else j - 1)
        return jnp.where(j == 0, LAT // TQ + b, lat)

    def spec(col):
        return pl.BlockSpec((TQ, HG_W), lambda b, j: (tile_row(b, j), col))

    in_specs = [spec(OFF_HQ // HG_W), spec(OFF_HI // HG_W), spec(1 if reverse else 0),
                pl.BlockSpec((DEPTH, HG_W), lambda b, j: (0, 0))]
    args = [pa, pa, pz, lb_logits_dir]
    out_specs = spec(0)
    out_shape = jax.ShapeDtypeStruct((M, HG_W), F32)
    kernel_cast = None
    if cast is not None:
        kind, w, w_layer = cast
        n_blocks = IN_MAIN // PROJ_TN if kind == "inproj" else D // PREP_ROWS

        def blk(b, j):
            return jnp.minimum(b * n_tiles_row + j, n_blocks - 1)

        if kind == "inproj":
            kernel_cast = (_cast_rows_kernel, n_blocks)
            in_specs.append(pl.BlockSpec((None, PROJ_TN, D), lambda b, j: (w_layer, blk(b, j), 0)))
            cast_spec = pl.BlockSpec((None, None, PROJ_TN, D), lambda b, j: (0, blk(b, j), 0, 0))
            cast_shape = jax.ShapeDtypeStruct((1, n_blocks, PROJ_TN, D), BF16)
        else:
            kernel_cast = (_proj_tiles_kernel, n_blocks)
            in_specs.append(pl.BlockSpec((None, PREP_ROWS, D), lambda b, j: (w_layer, blk(b, j), 0)))
            cast_spec = pl.BlockSpec((None, D // PROJ_TN, PREP_ROWS, PROJ_TN), lambda b, j: (0, 0, blk(b, j), 0))
            cast_shape = jax.ShapeDtypeStruct((1, D // PROJ_TN, D, PROJ_TN), BF16)
        args.append(w)
        out_specs, out_shape = [out_specs, cast_spec], [out_shape, cast_shape]
    return pl.pallas_call(
        functools.partial(_hgrn_kernel, reverse=reverse, layer=layer, cast=kernel_cast),
        grid=(BATCH, n_tiles_row),
        in_specs=in_specs,
        out_specs=out_specs,
        out_shape=out_shape,
        scratch_shapes=([pltpu.VMEM((TQ, HG_DK), F32)] * (2 * HG_HEADS)
                        + [pltpu.VMEM((HG_DK, HG_DK), F32)] * HG_HEADS),
        compiler_params=_cparams(("parallel" if cast is None else "arbitrary", "arbitrary"), 32),
        name="hgrn_bwd" if reverse else "hgrn_fwd",
    )(*args)


def _outproj_kernel(x_ref, gt_ref, oa_ref, of_ref, ob_ref, hg_ref, om_ref, ng_ref, w_ref, o_ref, lhs_ref):
    a_w = GQA_HEADS * HEAD_DIM
    lhs_ref[:, 0:a_w] = oa_ref[...]
    lhs_ref[:, a_w + HG_W:] = om_ref[...]
    gain = ng_ref[...]
    for hh in range(HG_HEADS):
        sl = slice(hh * HG_DK, (hh + 1) * HG_DK)
        g = hg_ref[:, sl].astype(F32)
        y = _head_norm(of_ref[:, sl] + ob_ref[:, sl], gain) * (g * _sigmoid(g))
        lhs_ref[:, a_w + hh * HG_DK:a_w + (hh + 1) * HG_DK] = y.astype(BF16)

    lhs = lhs_ref[...]
    gate = _mod_vec(gt_ref, pl.program_id(0), TMP)
    for n in range(D // PROJ_TN):
        cols = slice(n * PROJ_TN, (n + 1) * PROJ_TN)
        y = jnp.dot(lhs, w_ref[n], preferred_element_type=F32)
        o_ref[:, cols] = x_ref[:, cols] + gate[:, cols] * y


def _outproj(x, mods, o_a, o_f, o_b, pb, o_m, norm_gain, w_tiles, layer, rows):
    def row_spec(w, col=0):
        return pl.BlockSpec((TMP, w), lambda i: (i, col))

    return pl.pallas_call(
        _outproj_kernel,
        grid=(rows // TMP,),
        in_specs=[
            row_spec(D),
            pl.BlockSpec((None, SUBLANES, D), lambda i: (layer, 0, 5)),
            row_spec(GQA_HEADS * HEAD_DIM), row_spec(HG_W), row_spec(HG_W),
            row_spec(HG_W, 0),
            row_spec(MLA_HEADS * MLA_V),
            pl.BlockSpec((1, HG_DK), lambda i: (0, 0)),
            pl.BlockSpec((None, D // PROJ_TN, D, PROJ_TN), lambda i: (0, 0, 0, 0),
                         pipeline_mode=pl.Buffered(1)),
        ],
        out_specs=row_spec(D),
        out_shape=jax.ShapeDtypeStruct((rows, D), F32),
        scratch_shapes=[pltpu.VMEM((TMP, D), BF16)],
        compiler_params=_cparams(("parallel",), 56),
        name="outproj",
    )(x, mods, o_a, o_f, o_b, pb, o_m, norm_gain.reshape(1, HG_DK), w_tiles)


def _pad_w_uq(w_uq):
    w = w_uq.reshape(DEPTH, MLA_Q_RANK, MLA_HEADS, MLA_QK)
    w = jnp.pad(w, ((0, 0), (0, 0), (0, 0), (0, MLA_PAD - MLA_QK)))
    return w.reshape(DEPTH, MLA_Q_RANK, MLA_HEADS * MLA_PAD).astype(BF16)


def kernel(x, c, ctx, c_ctx, w_mod, b_mod, w_ffn1_in, w_ffn1_out, w_in, w_uq, w_ukv, w_out,
           gqa_q_gain, gqa_k_gain, mla_q_gain, mla_kv_gain, hgrn_lb_logits, hgrn_norm_gain,
           w_ffn2_in, w_ffn2_out, final_gain):
    cvec8 = jnp.concatenate([c, c_ctx[None, :], jnp.zeros((SUBLANES - BATCH - 1, D), F32)], axis=0)
    mods = _modulation(cvec8, w_mod, b_mod)
    tabs_h = _rope_tables(HEAD_DIM)
    tabs_r = _rope_tables(MLA_ROPE)
    ffn_in = _ffn_win_tiles(w_ffn1_in, 0)
    w_in_rows = jnp.transpose(w_in, (0, 2, 1))
    w_in_t = _inproj_weight_tiles(w_in_rows, 0)
    w_out_t = _proj_tiles(w_out, 0)
    w_kr = jnp.pad(w_in_rows[:, OFF_KR:, :], ((0, 0), (0, LANES - MLA_ROPE), (0, 0))).astype(BF16)
    w_uq_pad = _pad_w_uq(w_uq)
    w_ukv_b = w_ukv.astype(BF16)

    xa = None
    for l in range(DEPTH):
        last = l == DEPTH - 1
        if l == 0:
            xa = _ffn(x.reshape(LAT, D), mods, ffn_in, w_ffn1_out, l, 0, ctx=ctx.reshape(CTXR, D))
        else:
            xa = _ffn(xa, mods, ffn_in, w_ffn1_out, l, 0)
        pa, pz, pb, pkr = _inproj(xa, mods, w_in_t, w_kr, l)
        q_a, k_a, v_a, q_m, k_m, v_m = _qkv_prep(pa, pb, pkr, gqa_q_gain[l], gqa_k_gain[l], mla_q_gain[l],
                                                 mla_kv_gain[l], w_uq_pad[l], w_ukv_b[l], tabs_h, tabs_r)
        o_a, ffn2_in = _gqa_attn(q_a, k_a, v_a, not last, cast=(w_ffn2_in, l))
        if last:
            o_m = _mla_attn(q_m, k_m, v_m, False)
        else:
            o_m, ffn_in = _mla_attn(q_m, k_m, v_m, True, cast=(w_ffn1_in, l + 1))
        if last:
            o_f = _hgrn(pa, pz, hgrn_lb_logits[0], l, False)
            o_b = _hgrn(pa, pz, hgrn_lb_logits[1], l, True)
        else:
            o_f, w_in_next = _hgrn(pa, pz, hgrn_lb_logits[0], l, False, cast=("inproj", w_in_rows, l + 1))
            o_b, w_out_next = _hgrn(pa, pz, hgrn_lb_logits[1], l, True, cast=("outproj", w_out, l + 1))
        xa = _outproj(xa, mods, o_a, o_f, o_b, pb, o_m, hgrn_norm_gain[l], w_out_t, l, LAT if last else M)
        if not last:
            w_in_t, w_out_t = w_in_next, w_out_next
        xa = _ffn(xa, mods, ffn2_in, w_ffn2_out, l, 6, final_gain=final_gain if last else None)
    return xa
```

```python
import functools

import jax
import jax.numpy as jnp
from jax import lax
from jax.experimental import pallas as pl
from jax.experimental.pallas import tpu as pltpu

F32 = jnp.float32
BF16 = jnp.bfloat16

D = 2048
BATCH = 4
SEQ = 2048
DEPTH = 2
GRID_W = 64
CTX = 256
EPS = 1e-6
ROPE_THETA = 10000.0

HEAD_DIM = 128
GQA_HEADS = 8
GQA_KV = 2
GQA_GROUP = GQA_HEADS // GQA_KV
HG_HEADS = 4
HG_DK = 128
HG_W = HG_HEADS * HG_DK
MLA_HEADS = 4
MLA_Q_RANK = 512
MLA_KV_RANK = 256
MLA_NOPE = 128
MLA_ROPE = 64
MLA_V = 128
MLA_QK = MLA_NOPE + MLA_ROPE
MLA_PAD = 256
FFN_H = 5504
N_MOD = 9

LAT = BATCH * SEQ
CTXR = BATCH * CTX
M = LAT + CTXR
OFF_GQ, OFF_GK, OFF_GV = 0, 1024, 1280
OFF_HQ, OFF_HI, OFF_HF, OFF_HB, OFF_HG = 1536, 2048, 2560, 3072, 3584
OFF_CQ, OFF_CKV, OFF_KR = 4096, 4608, 4864
IN_MAIN = OFF_KR
PA_W, PZ_W, PB_W = OFF_HF, OFF_HG - OFF_HF, OFF_KR - OFF_HG

LANES = 128
SUBLANES = 8
MXU_W = 256

TM = 1024
TMP = 512
ROWC = 64
TQ = 256
TP = 1024
QT_SEQ = SEQ // TQ
HCHUNK = 64
FFN_TH = 512
FFN_HP = -(-FFN_H // FFN_TH) * FFN_TH
FFN_STEPS = FFN_HP // FFN_TH
MOD_TN = 1024
PROJ_TN = MXU_W
PREP_ROWS = 128
WO_SLAB = 128
CAST_ROWS = 64
V_AUG = MXU_W
LOG2E = 1.4426950408889634


def _cparams(sem, vmem_mb):
    return pltpu.CompilerParams(dimension_semantics=sem, vmem_limit_bytes=vmem_mb * 1024 * 1024)


def _sigmoid(x):
    return 1.0 / (1.0 + jnp.exp(-x))


def _nt_dot(a, b):
    return lax.dot_general(a, b, (((1,), (1,)), ((), ())), preferred_element_type=F32)


def _mod_kernel(c_ref, w_ref, b_ref, o_ref):
    c = c_ref[...]
    a = (c * _sigmoid(c)).astype(BF16)
    o_ref[...] = jnp.dot(a, w_ref[...].astype(BF16), preferred_element_type=F32) + b_ref[...]


def _modulation(cvec8, w_mod, b_mod):
    n = N_MOD * D
    return pl.pallas_call(
        _mod_kernel,
        grid=(DEPTH, n // MOD_TN),
        in_specs=[
            pl.BlockSpec((SUBLANES, D), lambda l, j: (0, 0)),
            pl.BlockSpec((None, D, MOD_TN), lambda l, j: (l, 0, j)),
            pl.BlockSpec((None, 1, MOD_TN), lambda l, j: (l, 0, j)),
        ],
        out_specs=pl.BlockSpec((None, SUBLANES, MOD_TN), lambda l, j: (l, 0, j)),
        out_shape=jax.ShapeDtypeStruct((DEPTH, SUBLANES, n), F32),
        compiler_params=_cparams(("parallel", "parallel"), 40),
        name="modulation",
    )(cvec8, w_mod, b_mod.reshape(DEPTH, 1, n))


def _mod_vec(ref, tile, tm):
    row = jnp.where(tile < LAT // tm, tile // (SEQ // tm), BATCH)
    return ref[pl.ds(row, 1), :]


def _modulate_tile(x_ref, shift_ref, scale_ref, h_ref, tile, tm):
    sh = _mod_vec(shift_ref, tile, tm)
    sc = 1.0 + _mod_vec(scale_ref, tile, tm)

    def body(c, carry):
        r0 = pl.multiple_of(c * ROWC, ROWC)
        xc = x_ref[pl.ds(r0, ROWC), :]
        ms = jnp.mean(xc * xc, axis=-1, keepdims=True)
        h_ref[pl.ds(r0, ROWC), :] = (xc * lax.rsqrt(ms + EPS) * sc + sh).astype(BF16)
        return carry

    lax.fori_loop(0, tm // ROWC, body, 0, unroll=2)


def _ffn_kernel(*refs, two_inputs, final):
    refs = list(refs)
    x_ref = refs.pop(0)
    c_ref = refs.pop(0) if two_inputs else None
    sh_ref, sc_ref, gt_ref = refs.pop(0), refs.pop(0), refs.pop(0)
    fg_ref = refs.pop(0) if final else None
    wi_ref = refs.pop(0)
    wo_refs = [refs.pop(0) for _ in range(FFN_TH // WO_SLAB)]
    o_ref, h_ref = refs
    i = pl.program_id(0)
    j = pl.program_id(1)

    def per_source(fn):
        if two_inputs:
            pl.when(i < LAT // TM)(lambda: fn(x_ref))
            pl.when(i >= LAT // TM)(lambda: fn(c_ref))
        else:
            fn(x_ref)

    @pl.when(j == 0)
    def _():
        per_source(lambda src: _modulate_tile(src, sh_ref, sc_ref, h_ref, i, TM))
        o_ref[...] = jnp.zeros_like(o_ref)

    pass_rows = TM // 2 if two_inputs else TM
    wo = jnp.concatenate([r_[...].astype(BF16) for r_ in wo_refs], axis=0)
    for rc in range(TM // pass_rows):
        rows = slice(rc * pass_rows, (rc + 1) * pass_rows)
        r = jnp.dot(h_ref[rows, :], wi_ref[...], preferred_element_type=F32)
        g = r[:, :FFN_TH]
        u = r[:, FFN_TH:]
        a = (g * _sigmoid(g) * u).astype(BF16)
        o_ref[rows, :] += jnp.dot(a, wo, preferred_element_type=F32)

    @pl.when(j == FFN_STEPS - 1)
    def _():
        gate = 0.5 * _mod_vec(gt_ref, i, TM)

        def epilogue(src):
            def body(c, carry):
                r0 = pl.multiple_of(c * ROWC, ROWC)
                y = src[pl.ds(r0, ROWC), :] + gate * o_ref[pl.ds(r0, ROWC), :]
                if final:
                    y = y * lax.rsqrt(jnp.mean(y * y, axis=-1, keepdims=True) + EPS) * fg_ref[...]
                o_ref[pl.ds(r0, ROWC), :] = y
                return carry

            lax.fori_loop(0, TM // ROWC, body, 0)

        per_source(epilogue)


def _ffn(x, mods, wi, wo, layer, mod0, ctx=None, final_gain=None):
    two_inputs, final = ctx is not None, final_gain is not None
    rows = M if two_inputs else x.shape[0]
    lat_tiles = LAT // TM

    def mod_spec(k):
        return pl.BlockSpec((None, SUBLANES, D), lambda i, j: (layer, 0, mod0 + k))

    in_specs = [pl.BlockSpec((TM, D), lambda i, j: (jnp.minimum(i, lat_tiles - 1) if two_inputs else i, 0),
                             pipeline_mode=pl.Buffered(1))]
    args = [x]
    if two_inputs:
        in_specs.append(pl.BlockSpec((TM, D), lambda i, j: (0, 0), pipeline_mode=pl.Buffered(1)))
        args.append(ctx)
    in_specs += [mod_spec(0), mod_spec(1), mod_spec(2)]
    args += [mods, mods, mods]
    if final:
        in_specs.append(pl.BlockSpec((1, D), lambda i, j: (0, 0)))
        args.append(final_gain.reshape(1, D))
        out_spec = pl.BlockSpec((None, TM, D), lambda i, j: (i // (SEQ // TM), i % (SEQ // TM), 0))
        out_shape = jax.ShapeDtypeStruct((BATCH, SEQ, D), F32)
    else:
        out_spec = pl.BlockSpec((TM, D), lambda i, j: (i, 0))
        out_shape = jax.ShapeDtypeStruct((rows, D), F32)
    in_specs.append(pl.BlockSpec((None, None, D, 2 * FFN_TH), lambda i, j: (0, j, 0, 0)))
    args.append(wi)
    slabs, last_slab = FFN_TH // WO_SLAB, FFN_H // WO_SLAB - 1
    for r in range(slabs):
        in_specs.append(pl.BlockSpec(
            (None, WO_SLAB, D),
            functools.partial(lambda i, j, r: (layer, jnp.minimum(j * slabs + r, last_slab), 0), r=r)))
        args.append(wo)
    return pl.pallas_call(
        functools.partial(_ffn_kernel, two_inputs=two_inputs, final=final),
        grid=(rows // TM, FFN_STEPS),
        in_specs=in_specs,
        out_specs=out_spec,
        out_shape=out_shape,
        scratch_shapes=[pltpu.VMEM((TM, D), BF16)],
        compiler_params=_cparams(("parallel", "arbitrary"), 58),
        name="ffn",
    )(*args)


def _ffn_win_tiles_body(w_ref, o_ref, rows):
    for j in range(FFN_STEPS):
        valid = min(FFN_TH, FFN_H - j * FFN_TH)
        for part, base in ((0, 0), (1, FFN_H)):
            c0 = part * FFN_TH
            o_ref[j, :, c0:c0 + valid] = w_ref[:, base + j * FFN_TH:base + j * FFN_TH + valid].astype(BF16)
            if valid < FFN_TH:
                o_ref[j, :, c0 + valid:c0 + FFN_TH] = jnp.zeros((rows, FFN_TH - valid), BF16)


def _ffn_win_tiles(w_in, layer):
    return pl.pallas_call(
        functools.partial(_ffn_win_tiles_body, rows=PREP_ROWS),
        grid=(D // PREP_ROWS,),
        in_specs=[pl.BlockSpec((None, PREP_ROWS, 2 * FFN_H), lambda r: (layer, r, 0))],
        out_specs=pl.BlockSpec((None, FFN_STEPS, PREP_ROWS, 2 * FFN_TH), lambda r: (0, 0, r, 0)),
        out_shape=jax.ShapeDtypeStruct((1, FFN_STEPS, D, 2 * FFN_TH), BF16),
        compiler_params=_cparams(("parallel",), 32),
        name="ffn_win_tiles",
    )(w_in)


def _proj_tiles_kernel(w_ref, o_ref):
    for t in range(o_ref.shape[0]):
        o_ref[t] = w_ref[:, t * PROJ_TN:(t + 1) * PROJ_TN].astype(BF16)


def _proj_tiles(w, layer):
    return pl.pallas_call(
        _proj_tiles_kernel,
        grid=(D // PREP_ROWS,),
        in_specs=[pl.BlockSpec((None, PREP_ROWS, D), lambda r: (layer, r, 0))],
        out_specs=pl.BlockSpec((None, D // PROJ_TN, PREP_ROWS, PROJ_TN), lambda r: (0, 0, r, 0)),
        out_shape=jax.ShapeDtypeStruct((1, D // PROJ_TN, D, PROJ_TN), BF16),
        compiler_params=_cparams(("parallel",), 32),
        name="proj_tiles",
    )(w)


def _cast_rows_kernel(w_ref, o_ref):
    o_ref[...] = w_ref[...].astype(BF16)


def _inproj_weight_tiles(w_in_t, layer):
    n_tiles = IN_MAIN // PROJ_TN
    return pl.pallas_call(
        _cast_rows_kernel,
        grid=(n_tiles,),
        in_specs=[pl.BlockSpec((None, PROJ_TN, D), lambda t: (layer, t, 0))],
        out_specs=pl.BlockSpec((None, None, PROJ_TN, D), lambda t: (0, t, 0, 0)),
        out_shape=jax.ShapeDtypeStruct((1, n_tiles, PROJ_TN, D), BF16),
        compiler_params=_cparams(("parallel",), 32),
        name="inproj_weight_tiles",
    )(w_in_t)


def _inproj_kernel(x_ref, sh_ref, sc_ref, w_ref, wkr_ref, pa_ref, pz_ref, pb_ref, kr_ref, h_ref):
    _modulate_tile(x_ref, sh_ref, sc_ref, h_ref, pl.program_id(0), TMP)
    h = h_ref[...]
    kr_ref[...] = _nt_dot(h, wkr_ref[...])
    for t in range(IN_MAIN // PROJ_TN):
        y = _nt_dot(h, w_ref[t])
        c0 = t * PROJ_TN
        if c0 < OFF_HF:
            pa_ref[:, c0:c0 + PROJ_TN] = y.astype(BF16)
        elif c0 < OFF_HG:
            pz_ref[:, c0 - OFF_HF:c0 - OFF_HF + PROJ_TN] = y
        else:
            pb_ref[:, c0 - OFF_HG:c0 - OFF_HG + PROJ_TN] = y.astype(BF16)


def _inproj(x, mods, w_tiles, w_kr, layer):
    def mod_spec(k):
        return pl.BlockSpec((None, SUBLANES, D), lambda i: (layer, 0, k))

    def out_spec(w):
        return pl.BlockSpec((TMP, w), lambda i: (i, 0))

    n_tiles = IN_MAIN // PROJ_TN
    return pl.pallas_call(
        _inproj_kernel,
        grid=(M // TMP,),
        in_specs=[
            pl.BlockSpec((TMP, D), lambda i: (i, 0)),
            mod_spec(3), mod_spec(4),
            pl.BlockSpec((None, n_tiles, PROJ_TN, D), lambda i: (0, 0, 0, 0), pipeline_mode=pl.Buffered(1)),
            pl.BlockSpec((None, LANES, D), lambda i: (layer, 0, 0)),
        ],
        out_specs=[out_spec(PA_W), out_spec(PZ_W), out_spec(PB_W), out_spec(LANES)],
        out_shape=[jax.ShapeDtypeStruct((M, PA_W), BF16), jax.ShapeDtypeStruct((M, PZ_W), F32),
                   jax.ShapeDtypeStruct((M, PB_W), BF16), jax.ShapeDtypeStruct((M, LANES), F32)],
        scratch_shapes=[pltpu.VMEM((TMP, D), BF16)],
        compiler_params=_cparams(("parallel",), 56),
        name="inproj",
    )(x, mods, mods, w_tiles, w_kr)


def _rope_tables(rot_dim):
    rows = SEQ // GRID_W
    row = jnp.repeat(jnp.arange(rows, dtype=F32), GRID_W)
    colp = jnp.tile(jnp.arange(GRID_W, dtype=F32), rows)
    axis_dim = rot_dim // 2
    inv_freq = ROPE_THETA ** (-jnp.arange(0, axis_dim, 2, dtype=F32) / axis_dim)
    ang_r = row[:, None] * inv_freq
    ang_c = colp[:, None] * inv_freq
    ang = jnp.concatenate([ang_r, ang_r, ang_c, ang_c], axis=-1)
    cos, sin = jnp.cos(ang), jnp.sin(ang)
    quarter = rot_dim // 4
    lane = jnp.arange(rot_dim)
    first = (lane % (2 * quarter)) < quarter
    sin_up = jnp.where(first, -sin, 0.0)
    sin_dn = jnp.where(first, 0.0, sin)
    pad = LANES - rot_dim
    if pad:
        cos = jnp.pad(cos, ((0, 0), (0, pad)), constant_values=1.0)
        sin_up = jnp.pad(sin_up, ((0, 0), (0, pad)))
        sin_dn = jnp.pad(sin_dn, ((0, 0), (0, pad)))
    return cos, sin_up, sin_dn


def _head_norm(x, gain):
    return x * lax.rsqrt(jnp.mean(x * x, axis=-1, keepdims=True) + EPS) * gain


def _lane_shift_matrix(quarter):
    j = lax.broadcasted_iota(jnp.int32, (LANES, 2 * LANES), 0)
    i = lax.broadcasted_iota(jnp.int32, (LANES, 2 * LANES), 1)
    source = jnp.where(i < LANES, i + quarter, i - LANES - quarter)
    return jnp.where(j == source, 1.0, 0.0).astype(BF16)


def _rope_mxu(x, shifts, cos, sin_up, sin_dn):
    r = jnp.dot(x.astype(BF16), shifts, preferred_element_type=F32)
    return x * cos + r[:, :LANES] * sin_up + r[:, LANES:] * sin_dn


def _head_norm_mxu(x, gain):
    mean_w = jnp.full((LANES, LANES), 1.0 / LANES, BF16)
    ms = jnp.dot((x * x).astype(BF16), mean_w, preferred_element_type=F32)
    return x * lax.rsqrt(ms + EPS) * gain


def _tile_is_ctx(t):
    return t >= LAT // TP


def _rope_idx(t):
    return jnp.where(_tile_is_ctx(t), 0, t % (SEQ // TP))


def _gqa_prep_kernel(p_ref, qg_ref, kg_ref, cos_ref, su_ref, sd_ref, q_ref, k_ref, v_ref):
    is_ctx = _tile_is_ctx(pl.program_id(0))
    cos, su, sd = cos_ref[...], su_ref[...], sd_ref[...]
    scale = HEAD_DIM ** -0.5 * LOG2E
    for kk in range(GQA_KV):
        v_ref[:, kk * V_AUG:kk * V_AUG + HEAD_DIM] = p_ref[:, OFF_GV + kk * HEAD_DIM:OFF_GV + (kk + 1) * HEAD_DIM]
        v_ref[:, kk * V_AUG + HEAD_DIM:(kk + 1) * V_AUG] = jnp.ones((TP, V_AUG - HEAD_DIM), BF16)
    shifts = _lane_shift_matrix(HEAD_DIM // 4)
    for hh in range(GQA_HEADS + GQA_KV):
        xh = p_ref[:, hh * HEAD_DIM:(hh + 1) * HEAD_DIM].astype(F32)
        is_q = hh < GQA_HEADS
        n = _head_norm_mxu(xh, qg_ref[...] if is_q else kg_ref[...])
        y = jnp.where(is_ctx, n, _rope_mxu(n, shifts, cos, su, sd))
        if is_q:
            q_ref[:, hh * HEAD_DIM:(hh + 1) * HEAD_DIM] = (y * scale).astype(BF16)
        else:
            kk = hh - GQA_HEADS
            k_ref[:, kk * HEAD_DIM:(kk + 1) * HEAD_DIM] = y.astype(BF16)


def _attn_kernel(q_ref, kl_ref, kc_ref, vl_ref, vc_ref, *rest, n_heads, head_cols, with_ctx, cast_rows):
    o_ref = rest[-1] if len(rest) == 1 else rest[1]
    if len(rest) == 3:
        step = pl.program_id(0) * pl.num_programs(1) + pl.program_id(1)
        pl.when(step < D // cast_rows)(lambda: _ffn_win_tiles_body(rest[0], rest[2], cast_rows))

    def finish(pv, os_):
        e_w = os_.stop - os_.start
        o_ref[:, os_] = (pv[:, :e_w] / pv[:, e_w:2 * e_w]).astype(BF16)

    def latent_queries():
        for hh in range(n_heads):
            qs, ks, vs, os_ = head_cols(hh)
            q = q_ref[:, qs]
            s = jnp.concatenate([_nt_dot(q, kl_ref[:, ks]), _nt_dot(q, kc_ref[:, ks])], axis=1)
            eb = jnp.exp2(s - jnp.max(s, axis=-1, keepdims=True)).astype(BF16)
            finish(jnp.dot(eb[:, :SEQ], vl_ref[:, vs], preferred_element_type=F32)
                   + jnp.dot(eb[:, SEQ:], vc_ref[:, vs], preferred_element_type=F32), os_)

    def context_queries():
        for hh in range(n_heads):
            qs, ks, vs, os_ = head_cols(hh)
            s = _nt_dot(q_ref[:, qs], kc_ref[:, ks])
            eb = jnp.exp2(s - jnp.max(s, axis=-1, keepdims=True)).astype(BF16)
            finish(jnp.dot(eb, vc_ref[:, vs], preferred_element_type=F32), os_)

    if with_ctx:
        pl.when(pl.program_id(1) < QT_SEQ)(latent_queries)
        pl.when(pl.program_id(1) == QT_SEQ)(context_queries)
    else:
        latent_queries()


def _attention(name, q, k, v, v_col, qw, kw, vw, ow, n_heads, head_cols, with_ctx, cast=None):
    tq = TQ if with_ctx else 2 * TQ
    nq = SEQ // tq + (1 if with_ctx else 0)
    cast_rows = max(CAST_ROWS, D // (BATCH * nq))

    def q_row(b, i):
        return jnp.where(i < SEQ // tq, b * (SEQ // tq) + i, LAT // tq + b)

    def ctx_row(b, i):
        return LAT // CTX + b

    in_specs = [
        pl.BlockSpec((tq, qw), lambda b, i: (q_row(b, i), 0)),
        pl.BlockSpec((SEQ, kw), lambda b, i: (b, 0)),
        pl.BlockSpec((CTX, kw), lambda b, i: (ctx_row(b, i), 0)),
        pl.BlockSpec((SEQ, vw), lambda b, i: (b, v_col)),
        pl.BlockSpec((CTX, vw), lambda b, i: (ctx_row(b, i), v_col)),
    ]
    args = [q, k, k, v, v]
    out_specs = pl.BlockSpec((tq, ow), lambda b, i: (q_row(b, i), 0))
    out_shape = jax.ShapeDtypeStruct((M if with_ctx else LAT, ow), BF16)
    if cast is not None:
        w_in, layer = cast

        def panel(b, i):
            return jnp.minimum(b * nq + i, D // cast_rows - 1)

        in_specs.append(pl.BlockSpec((None, cast_rows, 2 * FFN_H), lambda b, i: (layer, panel(b, i), 0)))
        args.append(w_in)
        out_specs = [out_specs,
                     pl.BlockSpec((None, FFN_STEPS, cast_rows, 2 * FFN_TH), lambda b, i: (0, 0, panel(b, i), 0))]
        out_shape = [out_shape, jax.ShapeDtypeStruct((1, FFN_STEPS, D, 2 * FFN_TH), BF16)]
    return pl.pallas_call(
        functools.partial(_attn_kernel, n_heads=n_heads, head_cols=head_cols, with_ctx=with_ctx,
                          cast_rows=cast_rows),
        grid=(BATCH, nq),
        in_specs=in_specs,
        out_specs=out_specs,
        out_shape=out_shape,
        compiler_params=_cparams(("arbitrary", "arbitrary"), 56),
        name=name,
    )(*args)


def _gqa_head_cols(hh):
    kk = hh // GQA_GROUP
    kv = slice(kk * HEAD_DIM, (kk + 1) * HEAD_DIM)
    hs = slice(hh * HEAD_DIM, (hh + 1) * HEAD_DIM)
    return hs, kv, slice(kk * V_AUG, (kk + 1) * V_AUG), hs


def _mla_head_cols(hh):
    qs = slice(hh * MLA_PAD, (hh + 1) * MLA_PAD)
    return qs, qs, slice(hh * V_AUG, (hh + 1) * V_AUG), slice(hh * MLA_V, (hh + 1) * MLA_V)


def _gqa_attn(q, k, v_aug, with_ctx, cast=None):
    qw, kw = GQA_HEADS * HEAD_DIM, GQA_KV * HEAD_DIM
    return _attention("gqa_attn", q, k, v_aug, 0, qw, kw, GQA_KV * V_AUG, qw, GQA_HEADS, _gqa_head_cols, with_ctx,
                      cast)


def _mla_attn(q, k, v_aug, with_ctx, cast=None):
    qw = MLA_HEADS * MLA_PAD
    return _attention("mla_attn", q, k, v_aug, 0, qw, qw, MLA_HEADS * V_AUG, MLA_HEADS * MLA_V, MLA_HEADS,
                      _mla_head_cols, with_ctx, cast)


def _mla_prep_kernel(cq_ref, ckv_ref, kr_ref, qg_ref, kvg_ref, wuq_ref, wukv_ref,
                     cos_ref, su_ref, sd_ref, q_ref, k_ref, v_ref):
    is_ctx = _tile_is_ctx(pl.program_id(0))
    cos, su, sd = cos_ref[...], su_ref[...], sd_ref[...]
    quarter = MLA_ROPE // 4
    scale = MLA_QK ** -0.5 * LOG2E

    cq = _head_norm(cq_ref[...].astype(F32), qg_ref[...]).astype(BF16)
    qf = jnp.dot(cq, wuq_ref[...], preferred_element_type=F32)
    ckv = _head_norm(ckv_ref[...].astype(F32), kvg_ref[...]).astype(BF16)
    kvf = jnp.dot(ckv, wukv_ref[...], preferred_element_type=F32)
    kr = kr_ref[...]
    shifts = _lane_shift_matrix(quarter)
    kr = jnp.where(is_ctx, kr, _rope_mxu(kr, shifts, cos, su, sd)).astype(BF16)
    for hh in range(MLA_HEADS):
        base = hh * MLA_PAD
        q_ref[:, base:base + MLA_NOPE] = (qf[:, base:base + MLA_NOPE] * scale).astype(BF16)
        qr = qf[:, base + MLA_NOPE:base + MLA_PAD]
        qr = jnp.where(is_ctx, qr, _rope_mxu(qr, shifts, cos, su, sd))
        q_ref[:, base + MLA_NOPE:base + MLA_PAD] = (qr * scale).astype(BF16)
        k_ref[:, base:base + MLA_NOPE] = kvf[:, base:base + MLA_NOPE].astype(BF16)
        k_ref[:, base + MLA_NOPE:base + MLA_PAD] = kr
        v_ref[:, hh * V_AUG:hh * V_AUG + MLA_V] = kvf[:, base + MLA_NOPE:base + MLA_PAD].astype(BF16)
        v_ref[:, hh * V_AUG + MLA_V:(hh + 1) * V_AUG] = jnp.ones((TP, V_AUG - MLA_V), BF16)


N_GQA_PREP_IN = 6
N_MLA_PREP_IN = 10


def _qkv_prep_kernel(*refs):
    gqa_in, rest = refs[:N_GQA_PREP_IN], refs[N_GQA_PREP_IN:]
    mla_in, outs = rest[:N_MLA_PREP_IN], rest[N_MLA_PREP_IN:]
    _gqa_prep_kernel(*gqa_in, *outs[:3])
    _mla_prep_kernel(*mla_in, *outs[3:])


def _qkv_prep(pa, pb, pkr, gqa_q_gain, gqa_k_gain, mla_q_gain, mla_kv_gain, w_uq_pad, w_ukv, tabs_h, tabs_r):
    tab_spec = pl.BlockSpec((TP, LANES), lambda t: (_rope_idx(t), 0))
    qk_w = MLA_HEADS * MLA_PAD

    def rows(w, col=0):
        return pl.BlockSpec((TP, w), lambda t: (t, col))

    def whole(r, c):
        return pl.BlockSpec((r, c), lambda t: (0, 0))

    widths = [GQA_HEADS * HEAD_DIM, GQA_KV * HEAD_DIM, GQA_KV * V_AUG, qk_w, qk_w, MLA_HEADS * V_AUG]
    return pl.pallas_call(
        _qkv_prep_kernel,
        grid=(M // TP,),
        in_specs=[
            rows(OFF_HQ), whole(1, HEAD_DIM), whole(1, HEAD_DIM), tab_spec, tab_spec, tab_spec,
            rows(MLA_Q_RANK, (OFF_CQ - OFF_HG) // MLA_Q_RANK), rows(MLA_KV_RANK, (OFF_CKV - OFF_HG) // MLA_KV_RANK),
            rows(LANES), whole(1, MLA_Q_RANK), whole(1, MLA_KV_RANK),
            whole(MLA_Q_RANK, qk_w), whole(MLA_KV_RANK, qk_w), tab_spec, tab_spec, tab_spec,
        ],
        out_specs=[rows(w) for w in widths],
        out_shape=[jax.ShapeDtypeStruct((M, w), BF16) for w in widths],
        compiler_params=_cparams(("parallel",), 48),
        name="qkv_prep",
    )(pa, gqa_q_gain.reshape(1, -1), gqa_k_gain.reshape(1, -1), *tabs_h,
      pb, pb, pkr, mla_q_gain.reshape(1, -1), mla_kv_gain.reshape(1, -1), w_uq_pad, w_ukv, *tabs_r)


TILE_CHUNKS = TQ // HCHUNK
DIAG = 8
LEVELS = (8, 16, 32)


def _split3(x):
    hi = x.astype(BF16)
    r1 = x - hi.astype(F32)
    mid = r1.astype(BF16)
    lo = (r1 - mid.astype(F32)).astype(BF16)
    return hi, mid, lo


def _hgrn_kernel(q_ref, v_ref, z_ref, lg_ref, *rest, reverse, layer, cast):
    if cast is None:
        o_ref, scratch = rest[0], rest[1:]
    else:
        o_ref, scratch = rest[1], rest[3:]
        step = pl.program_id(0) * pl.num_programs(1) + pl.program_id(1)
        pl.when(step < cast[1])(lambda: cast[0](rest[0], rest[2]))
    cum_refs = scratch[0:HG_HEADS]
    k_refs = scratch[HG_HEADS:2 * HG_HEADS]
    s_refs = scratch[2 * HG_HEADS:]

    lg = [lg_ref[l:l + 1, :] for l in range(DEPTH)]
    mx = functools.reduce(jnp.maximum, lg)
    ex = [jnp.exp(r - mx) for r in lg]
    den = functools.reduce(lambda a_, b_: a_ + b_, ex)
    lb_all = jnp.zeros((1, HG_W), F32)
    for l in range(1, layer + 1):
        lb_all = lb_all + ex[l] / den
    log_lb_all = jnp.log(lb_all)
    log_1m_all = jnp.log1p(-lb_all)

    ti = lax.broadcasted_iota(jnp.int32, (TQ, TQ), 0)
    si = lax.broadcasted_iota(jnp.int32, (TQ, TQ), 1)
    causal = (si >= ti) if reverse else (si <= ti)
    tri = (causal & ((ti // HCHUNK) == (si // HCHUNK))).astype(BF16)
    tc = lax.broadcasted_iota(jnp.int32, (HCHUNK, HCHUNK), 0)
    sc = lax.broadcasted_iota(jnp.int32, (HCHUNK, HCHUNK), 1)
    diag_mask = (((sc >= tc) if reverse else (sc <= tc)) & ((tc // DIAG) == (sc // DIAG))).astype(F32)
    ri = lax.broadcasted_iota(jnp.int32, (DIAG * HG_DK, LANES), 0)
    ci = lax.broadcasted_iota(jnp.int32, (DIAG * HG_DK, LANES), 1)
    sel = ((ri // HG_DK) == (ci % DIAG)).astype(BF16)
    edge = 0 if reverse else HCHUNK - 1

    @pl.when(pl.program_id(1) == 0)
    def _():
        for s_ref in s_refs:
            s_ref[...] = jnp.zeros_like(s_ref)

    heads = range(HG_HEADS)

    def lanes(hh):
        return slice(hh * HG_DK, (hh + 1) * HG_DK)

    chunks = range(TILE_CHUNKS)

    def crow(c):
        return slice(c * HCHUNK, (c + 1) * HCHUNK)

    q = [q_ref[:, lanes(hh)].astype(F32) * (HG_DK ** -0.5) for hh in heads]
    v = [v_ref[:, lanes(hh)] for hh in heads]

    z = z_ref[...]
    u = jnp.exp(-jnp.abs(z))
    w = 1.0 + u
    b_ = log_1m_all + (jnp.minimum(z, 0.0) - jnp.log(w))
    log_f = jnp.maximum(log_lb_all, b_) + jnp.log(1.0 + jnp.exp(-jnp.abs(log_lb_all - b_)))
    k_all = (1.0 - lb_all) * (jnp.where(z >= 0.0, u, 1.0) / w)
    k = [k_all[:, lanes(hh)] for hh in heads]

    cum3 = jnp.dot(tri, jnp.concatenate(_split3(log_f), axis=1), preferred_element_type=F32)
    cum_all = (cum3[:, :HG_W] + cum3[:, HG_W:2 * HG_W] + cum3[:, 2 * HG_W:]) * LOG2E
    cum = [cum_all[:, lanes(hh)] for hh in heads]
    for hh in heads:
        cum_refs[hh][...] = cum[hh]
        k_refs[hh][...] = k[hh]
    tot = [[cum_refs[hh][c * HCHUNK + edge:c * HCHUNK + edge + 1, :] for c in chunks] for hh in heads]

    def bcast_rows(ref, s):
        return jnp.concatenate(
            [jnp.broadcast_to(ref[blk * DIAG + s:blk * DIAG + s + 1, :], (DIAG, HG_DK))
             for blk in range(TQ // DIAG)], axis=0)

    pair_lhs = []
    for hh in heads:
        cols = [(q[hh] * jnp.exp2(jnp.minimum(cum[hh] - bcast_rows(cum_refs[hh], s), 0.0))
                 * bcast_rows(k_refs[hh], s)).astype(BF16) for s in range(DIAG)]
        pair_lhs.append(jnp.concatenate(cols, axis=1))

    def span_operands(hh, c):
        base = c * HCHUNK
        lhs, rhs = [], []
        for hs in LEVELS:
            for blk in range(HCHUNK // (2 * hs)):
                lo, mid, hi = blk * 2 * hs, blk * 2 * hs + hs, (blk + 1) * 2 * hs
                k_rows, q_rows = ((mid, hi), (lo, mid)) if reverse else ((lo, mid), (mid, hi))
                ref_row = base + (mid if reverse else mid - 1)
                rr = cum_refs[hh][ref_row:ref_row + 1, :]
                qr = slice(base + q_rows[0], base + q_rows[1])
                kr = slice(base + k_rows[0], base + k_rows[1])
                qs = q[hh][qr, :] * jnp.exp2(cum[hh][qr, :] - rr)
                ks = k[hh][kr, :] * jnp.exp2(rr - cum[hh][kr, :])
                zq = [jnp.zeros((q_rows[0], HG_DK), F32), qs, jnp.zeros((HCHUNK - q_rows[1], HG_DK), F32)]
                zk = [jnp.zeros((k_rows[0], HG_DK), F32), ks, jnp.zeros((HCHUNK - k_rows[1], HG_DK), F32)]
                lhs.append(jnp.concatenate([p_ for p_ in zq if p_.shape[0]], axis=0))
                rhs.append(jnp.concatenate([p_ for p_ in zk if p_.shape[0]], axis=0))
        return jnp.concatenate(lhs, axis=1).astype(BF16), jnp.concatenate(rhs, axis=1).astype(BF16)

    spans = [[span_operands(hh, c) for c in chunks] for hh in heads]
    qe = [(q[hh] * jnp.exp2(cum[hh])).astype(BF16) for hh in heads]
    ke = [[(k[hh][crow(c), :] * jnp.exp2(tot[hh][c] - cum[hh][crow(c), :])).astype(BF16) for c in chunks]
          for hh in heads]

    pair = [jnp.dot(pair_lhs[hh], sel, preferred_element_type=F32) for hh in heads]
    a_off = [[_nt_dot(*spans[hh][c]) for c in chunks] for hh in heads]
    upd = [[lax.dot_general(v[hh][crow(c), :], ke[hh][c], (((0,), (0,)), ((), ())),
                            preferred_element_type=F32) for c in chunks] for hh in heads]
    a = [[(a_off[hh][c] + pair[hh][crow(c), :HCHUNK] * diag_mask).astype(BF16) for c in chunks] for hh in heads]
    o_intra = [[jnp.dot(a[hh][c], v[hh][crow(c), :], preferred_element_type=F32) for c in chunks] for hh in heads]

    state = [s_refs[hh][...] for hh in heads]
    for c in (reversed(chunks) if reverse else chunks):
        inter = [_nt_dot(qe[hh][crow(c), :], state[hh].astype(BF16)) for hh in heads]
        for hh in heads:
            o_ref[crow(c), lanes(hh)] = o_intra[hh][c] + inter[hh]
            state[hh] = state[hh] * jnp.exp2(tot[hh][c]) + upd[hh][c]
    for hh in heads:
        s_refs[hh][...] = state[hh]


def _hgrn(pa, pz, lb_logits_dir, layer, reverse, cast=None):
    n_tiles_row = QT_SEQ + 1

    def tile_row(b, j):
        lat = b * QT_SEQ + (QT_SEQ - j if reverse else j - 1)
        return jnp.where(j == 0, LAT // TQ + b, lat)

    def spec(col):
        return pl.BlockSpec((TQ, HG_W), lambda b, j: (tile_row(b, j), col))

    in_specs = [spec(OFF_HQ // HG_W), spec(OFF_HI // HG_W), spec(1 if reverse else 0),
                pl.BlockSpec((DEPTH, HG_W), lambda b, j: (0, 0))]
    args = [pa, pa, pz, lb_logits_dir]
    out_specs = spec(0)
    out_shape = jax.ShapeDtypeStruct((M, HG_W), F32)
    kernel_cast = None
    if cast is not None:
        kind, w, w_layer = cast
        n_blocks = IN_MAIN // PROJ_TN if kind == "inproj" else D // PREP_ROWS

        def blk(b, j):
            return jnp.minimum(b * n_tiles_row + j, n_blocks - 1)

        if kind == "inproj":
            kernel_cast = (_cast_rows_kernel, n_blocks)
            in_specs.append(pl.BlockSpec((None, PROJ_TN, D), lambda b, j: (w_layer, blk(b, j), 0)))
            cast_spec = pl.BlockSpec((None, None, PROJ_TN, D), lambda b, j: (0, blk(b, j), 0, 0))
            cast_shape = jax.ShapeDtypeStruct((1, n_blocks, PROJ_TN, D), BF16)
        else:
            kernel_cast = (_proj_tiles_kernel, n_blocks)
            in_specs.append(pl.BlockSpec((None, PREP_ROWS, D), lambda b, j: (w_layer, blk(b, j), 0)))
            cast_spec = pl.BlockSpec((None, D // PROJ_TN, PREP_ROWS, PROJ_TN), lambda b, j: (0, 0, blk(b, j), 0))
            cast_shape = jax.ShapeDtypeStruct((1, D // PROJ_TN, D, PROJ_TN), BF16)
        args.append(w)
        out_specs, out_shape = [out_specs, cast_spec], [out_shape, cast_shape]
    return pl.pallas_call(
        functools.partial(_hgrn_kernel, reverse=reverse, layer=layer, cast=kernel_cast),
        grid=(BATCH, n_tiles_row),
        in_specs=in_specs,
        out_specs=out_specs,
        out_shape=out_shape,
        scratch_shapes=([pltpu.VMEM((TQ, HG_DK), F32)] * (2 * HG_HEADS)
                        + [pltpu.VMEM((HG_DK, HG_DK), F32)] * HG_HEADS),
        compiler_params=_cparams(("parallel" if cast is None else "arbitrary", "arbitrary"), 32),
        name="hgrn_bwd" if reverse else "hgrn_fwd",
    )(*args)


def _outproj_kernel(x_ref, gt_ref, oa_ref, of_ref, ob_ref, hg_ref, om_ref, ng_ref, w_ref, o_ref, lhs_ref):
    a_w = GQA_HEADS * HEAD_DIM
    lhs_ref[:, 0:a_w] = oa_ref[...]
    lhs_ref[:, a_w + HG_W:] = om_ref[...]
    gain = ng_ref[...]
    for hh in range(HG_HEADS):
        sl = slice(hh * HG_DK, (hh + 1) * HG_DK)
        g = hg_ref[:, sl].astype(F32)
        y = _head_norm(of_ref[:, sl] + ob_ref[:, sl], gain) * (g * _sigmoid(g))
        lhs_ref[:, a_w + hh * HG_DK:a_w + (hh + 1) * HG_DK] = y.astype(BF16)

    lhs = lhs_ref[...]
    gate = _mod_vec(gt_ref, pl.program_id(0), TMP)
    for n in range(D // PROJ_TN):
        cols = slice(n * PROJ_TN, (n + 1) * PROJ_TN)
        y = jnp.dot(lhs, w_ref[n], preferred_element_type=F32)
        o_ref[:, cols] = x_ref[:, cols] + gate[:, cols] * y


def _outproj(x, mods, o_a, o_f, o_b, pb, o_m, norm_gain, w_tiles, layer, rows):
    def row_spec(w, col=0):
        return pl.BlockSpec((TMP, w), lambda i: (i, col))

    return pl.pallas_call(
        _outproj_kernel,
        grid=(rows // TMP,),
        in_specs=[
            row_spec(D),
            pl.BlockSpec((None, SUBLANES, D), lambda i: (layer, 0, 5)),
            row_spec(GQA_HEADS * HEAD_DIM), row_spec(HG_W), row_spec(HG_W),
            row_spec(HG_W, 0),
            row_spec(MLA_HEADS * MLA_V),
            pl.BlockSpec((1, HG_DK), lambda i: (0, 0)),
            pl.BlockSpec((None, D // PROJ_TN, D, PROJ_TN), lambda i: (0, 0, 0, 0),
                         pipeline_mode=pl.Buffered(1)),
        ],
        out_specs=row_spec(D),
        out_shape=jax.ShapeDtypeStruct((rows, D), F32),
        scratch_shapes=[pltpu.VMEM((TMP, D), BF16)],
        compiler_params=_cparams(("parallel",), 56),
        name="outproj",
    )(x, mods, o_a, o_f, o_b, pb, o_m, norm_gain.reshape(1, HG_DK), w_tiles)


def _pad_w_uq(w_uq):
    w = w_uq.reshape(DEPTH, MLA_Q_RANK, MLA_HEADS, MLA_QK)
    w = jnp.pad(w, ((0, 0), (0, 0), (0, 0), (0, MLA_PAD - MLA_QK)))
    return w.reshape(DEPTH, MLA_Q_RANK, MLA_HEADS * MLA_PAD).astype(BF16)


def kernel(x, c, ctx, c_ctx, w_mod, b_mod, w_ffn1_in, w_ffn1_out, w_in, w_uq, w_ukv, w_out,
           gqa_q_gain, gqa_k_gain, mla_q_gain, mla_kv_gain, hgrn_lb_logits, hgrn_norm_gain,
           w_ffn2_in, w_ffn2_out, final_gain):
    cvec8 = jnp.concatenate([c, c_ctx[None, :], jnp.zeros((SUBLANES - BATCH - 1, D), F32)], axis=0)
    mods = _modulation(cvec8, w_mod, b_mod)
    tabs_h = _rope_tables(HEAD_DIM)
    tabs_r = _rope_tables(MLA_ROPE)
    ffn_in = _ffn_win_tiles(w_ffn1_in, 0)
    w_in_rows = jnp.transpose(w_in, (0, 2, 1))
    w_in_t = _inproj_weight_tiles(w_in_rows, 0)
    w_out_t = _proj_tiles(w_out, 0)
    w_kr = jnp.pad(w_in_rows[:, OFF_KR:, :], ((0, 0), (0, LANES - MLA_ROPE), (0, 0))).astype(BF16)
    w_uq_pad = _pad_w_uq(w_uq)
    w_ukv_b = w_ukv.astype(BF16)

    xa = None
    for l in range(DEPTH):
        last = l == DEPTH - 1
        if l == 0:
            xa = _ffn(x.reshape(LAT, D), mods, ffn_in, w_ffn1_out, l, 0, ctx=ctx.reshape(CTXR, D))
        else:
            xa = _ffn(xa, mods, ffn_in, w_ffn1_out, l, 0)
        pa, pz, pb, pkr = _inproj(xa, mods, w_in_t, w_kr, l)
        q_a, k_a, v_a, q_m, k_m, v_m = _qkv_prep(pa, pb, pkr, gqa_q_gain[l], gqa_k_gain[l], mla_q_gain[l],
                                                 mla_kv_gain[l], w_uq_pad[l], w_ukv_b[l], tabs_h, tabs_r)
        o_a, ffn2_in = _gqa_attn(q_a, k_a, v_a, not last, cast=(w_ffn2_in, l))
        if last:
            o_m = _mla_attn(q_m, k_m, v_m, False)
        else:
            o_m, ffn_in = _mla_attn(q_m, k_m, v_m, True, cast=(w_ffn1_in, l + 1))
        if last:
            o_f = _hgrn(pa, pz, hgrn_lb_logits[0], l, False)
            o_b = _hgrn(pa, pz, hgrn_lb_logits[1], l, True)
        else:
            o_f, w_in_next = _hgrn(pa, pz, hgrn_lb_logits[0], l, False, cast=("inproj", w_in_rows, l + 1))
            o_b, w_out_next = _hgrn(pa, pz, hgrn_lb_logits[1], l, True, cast=("outproj", w_out, l + 1))
        xa = _outproj(xa, mods, o_a, o_f, o_b, pb, o_m, hgrn_norm_gain[l], w_out_t, l, LAT if last else M)
        if not last:
            w_in_t, w_out_t = w_in_next, w_out_next
        xa = _ffn(xa, mods, ffn2_in, w_ffn2_out, l, 6, final_gain=final_gain if last else None)
    return xa
```

```python
import functools

import jax
import jax.numpy as jnp
from jax import lax
from jax.experimental import pallas as pl
from jax.experimental.pallas import tpu as pltpu

F32 = jnp.float32
BF16 = jnp.bfloat16

D = 2048
BATCH = 4
SEQ = 2048
DEPTH = 2
GRID_W = 64
CTX = 256
EPS = 1e-6
ROPE_THETA = 10000.0

HEAD_DIM = 128
GQA_HEADS = 8
GQA_KV = 2
GQA_GROUP = GQA_HEADS // GQA_KV
HG_HEADS = 4
HG_DK = 128
HG_W = HG_HEADS * HG_DK
MLA_HEADS = 4
MLA_Q_RANK = 512
MLA_KV_RANK = 256
MLA_NOPE = 128
MLA_ROPE = 64
MLA_V = 128
MLA_QK = MLA_NOPE + MLA_ROPE
MLA_PAD = 256
FFN_H = 5504
N_MOD = 9

LAT = BATCH * SEQ
CTXR = BATCH * CTX
M = LAT + CTXR
OFF_GQ, OFF_GK, OFF_GV = 0, 1024, 1280
OFF_HQ, OFF_HI, OFF_HF, OFF_HB, OFF_HG = 1536, 2048, 2560, 3072, 3584
OFF_CQ, OFF_CKV, OFF_KR = 4096, 4608, 4864
IN_MAIN = OFF_KR
PA_W, PZ_W, PB_W = OFF_HF, OFF_HG - OFF_HF, OFF_KR - OFF_HG

LANES = 128
SUBLANES = 8
MXU_W = 256

TM = 1024
TMP = 512
ROWC = 64
TQ = 256
TP = 1024
QT_SEQ = SEQ // TQ
HCHUNK = 64
FFN_TH = 512
FFN_HP = -(-FFN_H // FFN_TH) * FFN_TH
FFN_STEPS = FFN_HP // FFN_TH
MOD_TN = 1024
PROJ_TN = MXU_W
PREP_ROWS = 128
WO_SLAB = 128
CAST_ROWS = 64
V_AUG = MXU_W
LOG2E = 1.4426950408889634


def _cparams(sem, vmem_mb):
    return pltpu.CompilerParams(dimension_semantics=sem, vmem_limit_bytes=vmem_mb * 1024 * 1024)


def _sigmoid(x):
    return 1.0 / (1.0 + jnp.exp(-x))


def _nt_dot(a, b):
    return lax.dot_general(a, b, (((1,), (1,)), ((), ())), preferred_element_type=F32)


def _mod_kernel(c_ref, w_ref, b_ref, o_ref):
    c = c_ref[...]
    a = (c * _sigmoid(c)).astype(BF16)
    o_ref[...] = jnp.dot(a, w_ref[...].astype(BF16), preferred_element_type=F32) + b_ref[...]


def _modulation(cvec8, w_mod, b_mod):
    n = N_MOD * D
    return pl.pallas_call(
        _mod_kernel,
        grid=(DEPTH, n // MOD_TN),
        in_specs=[
            pl.BlockSpec((SUBLANES, D), lambda l, j: (0, 0)),
            pl.BlockSpec((None, D, MOD_TN), lambda l, j: (l, 0, j)),
            pl.BlockSpec((None, 1, MOD_TN), lambda l, j: (l, 0, j)),
        ],
        out_specs=pl.BlockSpec((None, SUBLANES, MOD_TN), lambda l, j: (l, 0, j)),
        out_shape=jax.ShapeDtypeStruct((DEPTH, SUBLANES, n), F32),
        compiler_params=_cparams(("parallel", "parallel"), 40),
        name="modulation",
    )(cvec8, w_mod, b_mod.reshape(DEPTH, 1, n))


def _mod_vec(ref, tile, tm):
    row = jnp.where(tile < LAT // tm, tile // (SEQ // tm), BATCH)
    return ref[pl.ds(row, 1), :]


def _modulate_tile(x_ref, shift_ref, scale_ref, h_ref, tile, tm):
    sh = _mod_vec(shift_ref, tile, tm)
    sc = 1.0 + _mod_vec(scale_ref, tile, tm)

    def body(c, carry):
        r0 = pl.multiple_of(c * ROWC, ROWC)
        xc = x_ref[pl.ds(r0, ROWC), :]
        ms = jnp.mean(xc * xc, axis=-1, keepdims=True)
        h_ref[pl.ds(r0, ROWC), :] = (xc * lax.rsqrt(ms + EPS) * sc + sh).astype(BF16)
        return carry

    lax.fori_loop(0, tm // ROWC, body, 0, unroll=2)


def _ffn_kernel(*refs, two_inputs, final):
    refs = list(refs)
    x_ref = refs.pop(0)
    c_ref = refs.pop(0) if two_inputs else None
    sh_ref, sc_ref, gt_ref = refs.pop(0), refs.pop(0), refs.pop(0)
    fg_ref = refs.pop(0) if final else None
    wi_ref = refs.pop(0)
    wo_refs = [refs.pop(0) for _ in range(FFN_TH // WO_SLAB)]
    o_ref, h_ref = refs
    i = pl.program_id(0)
    j = pl.program_id(1)

    def per_source(fn):
        if two_inputs:
            pl.when(i < LAT // TM)(lambda: fn(x_ref))
            pl.when(i >= LAT // TM)(lambda: fn(c_ref))
        else:
            fn(x_ref)

    @pl.when(j == 0)
    def _():
        per_source(lambda src: _modulate_tile(src, sh_ref, sc_ref, h_ref, i, TM))
        o_ref[...] = jnp.zeros_like(o_ref)

    pass_rows = TM
    wo = jnp.concatenate([r_[...].astype(BF16) for r_ in wo_refs], axis=0)
    for rc in range(TM // pass_rows):
        rows = slice(rc * pass_rows, (rc + 1) * pass_rows)
        r = jnp.dot(h_ref[rows, :], wi_ref[...], preferred_element_type=F32)
        g = r[:, :FFN_TH]
        u = r[:, FFN_TH:]
        a = (g * _sigmoid(g) * u).astype(BF16)
        o_ref[rows, :] += jnp.dot(a, wo, preferred_element_type=F32)

    @pl.when(j == FFN_STEPS - 1)
    def _():
        gate = 0.5 * _mod_vec(gt_ref, i, TM)

        def epilogue(src):
            def body(c, carry):
                r0 = pl.multiple_of(c * ROWC, ROWC)
                y = src[pl.ds(r0, ROWC), :] + gate * o_ref[pl.ds(r0, ROWC), :]
                if final:
                    y = y * lax.rsqrt(jnp.mean(y * y, axis=-1, keepdims=True) + EPS) * fg_ref[...]
                o_ref[pl.ds(r0, ROWC), :] = y
                return carry

            lax.fori_loop(0, TM // ROWC, body, 0)

        per_source(epilogue)


def _ffn(x, mods, wi, wo, layer, mod0, ctx=None, final_gain=None):
    two_inputs, final = ctx is not None, final_gain is not None
    rows = M if two_inputs else x.shape[0]
    lat_tiles = LAT // TM

    def mod_spec(k):
        return pl.BlockSpec((None, SUBLANES, D), lambda i, j: (layer, 0, mod0 + k))

    in_specs = [pl.BlockSpec((TM, D), lambda i, j: (jnp.minimum(i, lat_tiles - 1) if two_inputs else i, 0),
                             pipeline_mode=pl.Buffered(1))]
    args = [x]
    if two_inputs:
        in_specs.append(pl.BlockSpec((TM, D), lambda i, j: (0, 0), pipeline_mode=pl.Buffered(1)))
        args.append(ctx)
    in_specs += [mod_spec(0), mod_spec(1), mod_spec(2)]
    args += [mods, mods, mods]
    if final:
        in_specs.append(pl.BlockSpec((1, D), lambda i, j: (0, 0)))
        args.append(final_gain.reshape(1, D))
        out_spec = pl.BlockSpec((None, TM, D), lambda i, j: (i // (SEQ // TM), i % (SEQ // TM), 0))
        out_shape = jax.ShapeDtypeStruct((BATCH, SEQ, D), F32)
    else:
        out_spec = pl.BlockSpec((TM, D), lambda i, j: (i, 0))
        out_shape = jax.ShapeDtypeStruct((rows, D), F32)
    in_specs.append(pl.BlockSpec((None, None, D, 2 * FFN_TH), lambda i, j: (0, j, 0, 0)))
    args.append(wi)
    slabs, last_slab = FFN_TH // WO_SLAB, FFN_H // WO_SLAB - 1
    for r in range(slabs):
        in_specs.append(pl.BlockSpec(
            (None, WO_SLAB, D),
            functools.partial(lambda i, j, r: (layer, jnp.minimum(j * slabs + r, last_slab), 0), r=r)))
        args.append(wo)
    return pl.pallas_call(
        functools.partial(_ffn_kernel, two_inputs=two_inputs, final=final),
        grid=(rows // TM, FFN_STEPS),
        in_specs=in_specs,
        out_specs=out_spec,
        out_shape=out_shape,
        scratch_shapes=[pltpu.VMEM((TM, D), BF16)],
        compiler_params=_cparams(("parallel", "arbitrary"), 58),
        name="ffn",
    )(*args)


def _ffn_win_tiles_body(w_ref, o_ref, rows):
    for j in range(FFN_STEPS):
        valid = min(FFN_TH, FFN_H - j * FFN_TH)
        for part, base in ((0, 0), (1, FFN_H)):
            c0 = part * FFN_TH
            o_ref[j, :, c0:c0 + valid] = w_ref[:, base + j * FFN_TH:base + j * FFN_TH + valid].astype(BF16)
            if valid < FFN_TH:
                o_ref[j, :, c0 + valid:c0 + FFN_TH] = jnp.zeros((rows, FFN_TH - valid), BF16)


def _ffn_win_tiles(w_in, layer):
    return pl.pallas_call(
        functools.partial(_ffn_win_tiles_body, rows=PREP_ROWS),
        grid=(D // PREP_ROWS,),
        in_specs=[pl.BlockSpec((None, PREP_ROWS, 2 * FFN_H), lambda r: (layer, r, 0))],
        out_specs=pl.BlockSpec((None, FFN_STEPS, PREP_ROWS, 2 * FFN_TH), lambda r: (0, 0, r, 0)),
        out_shape=jax.ShapeDtypeStruct((1, FFN_STEPS, D, 2 * FFN_TH), BF16),
        compiler_params=_cparams(("parallel",), 32),
        name="ffn_win_tiles",
    )(w_in)


def _proj_tiles_kernel(w_ref, o_ref):
    for t in range(o_ref.shape[0]):
        o_ref[t] = w_ref[:, t * PROJ_TN:(t + 1) * PROJ_TN].astype(BF16)


def _proj_tiles(w, layer):
    return pl.pallas_call(
        _proj_tiles_kernel,
        grid=(D // PREP_ROWS,),
        in_specs=[pl.BlockSpec((None, PREP_ROWS, D), lambda r: (layer, r, 0))],
        out_specs=pl.BlockSpec((None, D // PROJ_TN, PREP_ROWS, PROJ_TN), lambda r: (0, 0, r, 0)),
        out_shape=jax.ShapeDtypeStruct((1, D // PROJ_TN, D, PROJ_TN), BF16),
        compiler_params=_cparams(("parallel",), 32),
        name="proj_tiles",
    )(w)


def _cast_rows_kernel(w_ref, o_ref):
    o_ref[...] = w_ref[...].astype(BF16)


def _inproj_weight_tiles(w_in_t, layer):
    n_tiles = IN_MAIN // PROJ_TN
    return pl.pallas_call(
        _cast_rows_kernel,
        grid=(n_tiles,),
        in_specs=[pl.BlockSpec((None, PROJ_TN, D), lambda t: (layer, t, 0))],
        out_specs=pl.BlockSpec((None, None, PROJ_TN, D), lambda t: (0, t, 0, 0)),
        out_shape=jax.ShapeDtypeStruct((1, n_tiles, PROJ_TN, D), BF16),
        compiler_params=_cparams(("parallel",), 32),
        name="inproj_weight_tiles",
    )(w_in_t)


def _inproj_kernel(x_ref, sh_ref, sc_ref, w_ref, wkr_ref, pa_ref, pz_ref, pb_ref, kr_ref, h_ref):
    _modulate_tile(x_ref, sh_ref, sc_ref, h_ref, pl.program_id(0), TMP)
    h = h_ref[...]
    kr_ref[...] = _nt_dot(h, wkr_ref[...])
    for t in range(IN_MAIN // PROJ_TN):
        y = _nt_dot(h, w_ref[t])
        c0 = t * PROJ_TN
        if c0 < OFF_HF:
            pa_ref[:, c0:c0 + PROJ_TN] = y.astype(BF16)
        elif c0 < OFF_HG:
            pz_ref[:, c0 - OFF_HF:c0 - OFF_HF + PROJ_TN] = y
        else:
            pb_ref[:, c0 - OFF_HG:c0 - OFF_HG + PROJ_TN] = y.astype(BF16)


def _inproj(x, mods, w_tiles, w_kr, layer):
    def mod_spec(k):
        return pl.BlockSpec((None, SUBLANES, D), lambda i: (layer, 0, k))

    def out_spec(w):
        return pl.BlockSpec((TMP, w), lambda i: (i, 0))

    n_tiles = IN_MAIN // PROJ_TN
    return pl.pallas_call(
        _inproj_kernel,
        grid=(M // TMP,),
        in_specs=[
            pl.BlockSpec((TMP, D), lambda i: (i, 0)),
            mod_spec(3), mod_spec(4),
            pl.BlockSpec((None, n_tiles, PROJ_TN, D), lambda i: (0, 0, 0, 0), pipeline_mode=pl.Buffered(1)),
            pl.BlockSpec((None, LANES, D), lambda i: (layer, 0, 0)),
        ],
        out_specs=[out_spec(PA_W), out_spec(PZ_W), out_spec(PB_W), out_spec(LANES)],
        out_shape=[jax.ShapeDtypeStruct((M, PA_W), BF16), jax.ShapeDtypeStruct((M, PZ_W), F32),
                   jax.ShapeDtypeStruct((M, PB_W), BF16), jax.ShapeDtypeStruct((M, LANES), F32)],
        scratch_shapes=[pltpu.VMEM((TMP, D), BF16)],
        compiler_params=_cparams(("parallel",), 56),
        name="inproj",
    )(x, mods, mods, w_tiles, w_kr)


def _rope_tables(rot_dim):
    rows = SEQ // GRID_W
    row = jnp.repeat(jnp.arange(rows, dtype=F32), GRID_W)
    colp = jnp.tile(jnp.arange(GRID_W, dtype=F32), rows)
    axis_dim = rot_dim // 2
    inv_freq = ROPE_THETA ** (-jnp.arange(0, axis_dim, 2, dtype=F32) / axis_dim)
    ang_r = row[:, None] * inv_freq
    ang_c = colp[:, None] * inv_freq
    ang = jnp.concatenate([ang_r, ang_r, ang_c, ang_c], axis=-1)
    cos, sin = jnp.cos(ang), jnp.sin(ang)
    quarter = rot_dim // 4
    lane = jnp.arange(rot_dim)
    first = (lane % (2 * quarter)) < quarter
    sin_up = jnp.where(first, -sin, 0.0)
    sin_dn = jnp.where(first, 0.0, sin)
    pad = LANES - rot_dim
    if pad:
        cos = jnp.pad(cos, ((0, 0), (0, pad)), constant_values=1.0)
        sin_up = jnp.pad(sin_up, ((0, 0), (0, pad)))
        sin_dn = jnp.pad(sin_dn, ((0, 0), (0, pad)))
    return cos, sin_up, sin_dn


def _head_norm(x, gain):
    return x * lax.rsqrt(jnp.mean(x * x, axis=-1, keepdims=True) + EPS) * gain


def _lane_shift_matrix(quarter):
    j = lax.broadcasted_iota(jnp.int32, (LANES, 2 * LANES), 0)
    i = lax.broadcasted_iota(jnp.int32, (LANES, 2 * LANES), 1)
    source = jnp.where(i < LANES, i + quarter, i - LANES - quarter)
    return jnp.where(j == source, 1.0, 0.0).astype(BF16)


def _rope_mxu(x, shifts, cos, sin_up, sin_dn):
    r = jnp.dot(x.astype(BF16), shifts, preferred_element_type=F32)
    return x * cos + r[:, :LANES] * sin_up + r[:, LANES:] * sin_dn


def _head_norm_mxu(x, gain):
    mean_w = jnp.full((LANES, LANES), 1.0 / LANES, BF16)
    ms = jnp.dot((x * x).astype(BF16), mean_w, preferred_element_type=F32)
    return x * lax.rsqrt(ms + EPS) * gain


def _tile_is_ctx(t):
    return t >= LAT // TP


def _rope_idx(t):
    return jnp.where(_tile_is_ctx(t), 0, t % (SEQ // TP))


def _gqa_prep_kernel(p_ref, qg_ref, kg_ref, cos_ref, su_ref, sd_ref, q_ref, k_ref, v_ref):
    is_ctx = _tile_is_ctx(pl.program_id(0))
    cos, su, sd = cos_ref[...], su_ref[...], sd_ref[...]
    scale = HEAD_DIM ** -0.5 * LOG2E
    for kk in range(GQA_KV):
        v_ref[:, kk * V_AUG:kk * V_AUG + HEAD_DIM] = p_ref[:, OFF_GV + kk * HEAD_DIM:OFF_GV + (kk + 1) * HEAD_DIM]
        v_ref[:, kk * V_AUG + HEAD_DIM:(kk + 1) * V_AUG] = jnp.ones((TP, V_AUG - HEAD_DIM), BF16)
    shifts = _lane_shift_matrix(HEAD_DIM // 4)
    for hh in range(GQA_HEADS + GQA_KV):
        xh = p_ref[:, hh * HEAD_DIM:(hh + 1) * HEAD_DIM].astype(F32)
        is_q = hh < GQA_HEADS
        n = _head_norm_mxu(xh, qg_ref[...] if is_q else kg_ref[...])
        y = jnp.where(is_ctx, n, _rope_mxu(n, shifts, cos, su, sd))
        if is_q:
            q_ref[:, hh * HEAD_DIM:(hh + 1) * HEAD_DIM] = (y * scale).astype(BF16)
        else:
            kk = hh - GQA_HEADS
            k_ref[:, kk * HEAD_DIM:(kk + 1) * HEAD_DIM] = y.astype(BF16)


def _attn_kernel(q_ref, kl_ref, kc_ref, vl_ref, vc_ref, *rest, n_heads, head_cols, with_ctx, cast_rows):
    o_ref = rest[-1] if len(rest) == 1 else rest[1]
    if len(rest) == 3:
        step = pl.program_id(0) * pl.num_programs(1) + pl.program_id(1)
        pl.when(step < D // cast_rows)(lambda: _ffn_win_tiles_body(rest[0], rest[2], cast_rows))

    def finish(pv, os_):
        e_w = os_.stop - os_.start
        o_ref[:, os_] = (pv[:, :e_w] / pv[:, e_w:2 * e_w]).astype(BF16)

    def latent_queries():
        for hh in range(n_heads):
            qs, ks, vs, os_ = head_cols(hh)
            q = q_ref[:, qs]
            s = jnp.concatenate([_nt_dot(q, kl_ref[:, ks]), _nt_dot(q, kc_ref[:, ks])], axis=1)
            eb = jnp.exp2(s - jnp.max(s, axis=-1, keepdims=True)).astype(BF16)
            finish(jnp.dot(eb[:, :SEQ], vl_ref[:, vs], preferred_element_type=F32)
                   + jnp.dot(eb[:, SEQ:], vc_ref[:, vs], preferred_element_type=F32), os_)

    def context_queries():
        for hh in range(n_heads):
            qs, ks, vs, os_ = head_cols(hh)
            s = _nt_dot(q_ref[:, qs], kc_ref[:, ks])
            eb = jnp.exp2(s - jnp.max(s, axis=-1, keepdims=True)).astype(BF16)
            finish(jnp.dot(eb, vc_ref[:, vs], preferred_element_type=F32), os_)

    if with_ctx:
        pl.when(pl.program_id(1) < QT_SEQ)(latent_queries)
        pl.when(pl.program_id(1) == QT_SEQ)(context_queries)
    else:
        latent_queries()


def _attention(name, q, k, v, v_col, qw, kw, vw, ow, n_heads, head_cols, with_ctx, cast=None):
    tq = TQ if with_ctx else 2 * TQ
    nq = SEQ // tq + (1 if with_ctx else 0)
    cast_rows = max(CAST_ROWS, D // (BATCH * nq))

    def q_row(b, i):
        return jnp.where(i < SEQ // tq, b * (SEQ // tq) + i, LAT // tq + b)

    def ctx_row(b, i):
        return LAT // CTX + b

    in_specs = [
        pl.BlockSpec((tq, qw), lambda b, i: (q_row(b, i), 0)),
        pl.BlockSpec((SEQ, kw), lambda b, i: (b, 0)),
        pl.BlockSpec((CTX, kw), lambda b, i: (ctx_row(b, i), 0)),
        pl.BlockSpec((SEQ, vw), lambda b, i: (b, v_col)),
        pl.BlockSpec((CTX, vw), lambda b, i: (ctx_row(b, i), v_col)),
    ]
    args = [q, k, k, v, v]
    out_specs = pl.BlockSpec((tq, ow), lambda b, i: (q_row(b, i), 0))
    out_shape = jax.ShapeDtypeStruct((M if with_ctx else LAT, ow), BF16)
    if cast is not None:
        w_in, layer = cast

        def panel(b, i):
            return jnp.minimum(b * nq + i, D // cast_rows - 1)

        in_specs.append(pl.BlockSpec((None, cast_rows, 2 * FFN_H), lambda b, i: (layer, panel(b, i), 0)))
        args.append(w_in)
        out_specs = [out_specs,
                     pl.BlockSpec((None, FFN_STEPS, cast_rows, 2 * FFN_TH), lambda b, i: (0, 0, panel(b, i), 0))]
        out_shape = [out_shape, jax.ShapeDtypeStruct((1, FFN_STEPS, D, 2 * FFN_TH), BF16)]
    return pl.pallas_call(
        functools.partial(_attn_kernel, n_heads=n_heads, head_cols=head_cols, with_ctx=with_ctx,
                          cast_rows=cast_rows),
        grid=(BATCH, nq),
        in_specs=in_specs,
        out_specs=out_specs,
        out_shape=out_shape,
        compiler_params=_cparams(("arbitrary", "arbitrary"), 56),
        name=name,
    )(*args)


def _gqa_head_cols(hh):
    kk = hh // GQA_GROUP
    kv = slice(kk * HEAD_DIM, (kk + 1) * HEAD_DIM)
    hs = slice(hh * HEAD_DIM, (hh + 1) * HEAD_DIM)
    return hs, kv, slice(kk * V_AUG, (kk + 1) * V_AUG), hs


def _mla_head_cols(hh):
    qs = slice(hh * MLA_PAD, (hh + 1) * MLA_PAD)
    return qs, qs, slice(hh * V_AUG, (hh + 1) * V_AUG), slice(hh * MLA_V, (hh + 1) * MLA_V)


def _gqa_attn(q, k, v_aug, with_ctx, cast=None):
    qw, kw = GQA_HEADS * HEAD_DIM, GQA_KV * HEAD_DIM
    return _attention("gqa_attn", q, k, v_aug, 0, qw, kw, GQA_KV * V_AUG, qw, GQA_HEADS, _gqa_head_cols, with_ctx,
                      cast)


def _mla_attn(q, k, v_aug, with_ctx, cast=None):
    qw = MLA_HEADS * MLA_PAD
    return _attention("mla_attn", q, k, v_aug, 0, qw, qw, MLA_HEADS * V_AUG, MLA_HEADS * MLA_V, MLA_HEADS,
                      _mla_head_cols, with_ctx, cast)


def _mla_prep_kernel(cq_ref, ckv_ref, kr_ref, qg_ref, kvg_ref, wuq_ref, wukv_ref,
                     cos_ref, su_ref, sd_ref, q_ref, k_ref, v_ref):
    is_ctx = _tile_is_ctx(pl.program_id(0))
    cos, su, sd = cos_ref[...], su_ref[...], sd_ref[...]
    quarter = MLA_ROPE // 4
    scale = MLA_QK ** -0.5 * LOG2E

    cq = _head_norm(cq_ref[...].astype(F32), qg_ref[...]).astype(BF16)
    qf = jnp.dot(cq, wuq_ref[...], preferred_element_type=F32)
    ckv = _head_norm(ckv_ref[...].astype(F32), kvg_ref[...]).astype(BF16)
    kvf = jnp.dot(ckv, wukv_ref[...], preferred_element_type=F32)
    kr = kr_ref[...]
    shifts = _lane_shift_matrix(quarter)
    kr = jnp.where(is_ctx, kr, _rope_mxu(kr, shifts, cos, su, sd)).astype(BF16)
    for hh in range(MLA_HEADS):
        base = hh * MLA_PAD
        q_ref[:, base:base + MLA_NOPE] = (qf[:, base:base + MLA_NOPE] * scale).astype(BF16)
        qr = qf[:, base + MLA_NOPE:base + MLA_PAD]
        qr = jnp.where(is_ctx, qr, _rope_mxu(qr, shifts, cos, su, sd))
        q_ref[:, base + MLA_NOPE:base + MLA_PAD] = (qr * scale).astype(BF16)
        k_ref[:, base:base + MLA_NOPE] = kvf[:, base:base + MLA_NOPE].astype(BF16)
        k_ref[:, base + MLA_NOPE:base + MLA_PAD] = kr
        v_ref[:, hh * V_AUG:hh * V_AUG + MLA_V] = kvf[:, base + MLA_NOPE:base + MLA_PAD].astype(BF16)
        v_ref[:, hh * V_AUG + MLA_V:(hh + 1) * V_AUG] = jnp.ones((TP, V_AUG - MLA_V), BF16)


N_GQA_PREP_IN = 6
N_MLA_PREP_IN = 10


def _qkv_prep_kernel(*refs):
    gqa_in, rest = refs[:N_GQA_PREP_IN], refs[N_GQA_PREP_IN:]
    mla_in, outs = rest[:N_MLA_PREP_IN], rest[N_MLA_PREP_IN:]
    _gqa_prep_kernel(*gqa_in, *outs[:3])
    _mla_prep_kernel(*mla_in, *outs[3:])


def _qkv_prep(pa, pb, pkr, gqa_q_gain, gqa_k_gain, mla_q_gain, mla_kv_gain, w_uq_pad, w_ukv, tabs_h, tabs_r):
    tab_spec = pl.BlockSpec((TP, LANES), lambda t: (_rope_idx(t), 0))
    qk_w = MLA_HEADS * MLA_PAD

    def rows(w, col=0):
        return pl.BlockSpec((TP, w), lambda t: (t, col))

    def whole(r, c):
        return pl.BlockSpec((r, c), lambda t: (0, 0))

    widths = [GQA_HEADS * HEAD_DIM, GQA_KV * HEAD_DIM, GQA_KV * V_AUG, qk_w, qk_w, MLA_HEADS * V_AUG]
    return pl.pallas_call(
        _qkv_prep_kernel,
        grid=(M // TP,),
        in_specs=[
            rows(OFF_HQ), whole(1, HEAD_DIM), whole(1, HEAD_DIM), tab_spec, tab_spec, tab_spec,
            rows(MLA_Q_RANK, (OFF_CQ - OFF_HG) // MLA_Q_RANK), rows(MLA_KV_RANK, (OFF_CKV - OFF_HG) // MLA_KV_RANK),
            rows(LANES), whole(1, MLA_Q_RANK), whole(1, MLA_KV_RANK),
            whole(MLA_Q_RANK, qk_w), whole(MLA_KV_RANK, qk_w), tab_spec, tab_spec, tab_spec,
        ],
        out_specs=[rows(w) for w in widths],
        out_shape=[jax.ShapeDtypeStruct((M, w), BF16) for w in widths],
        compiler_params=_cparams(("parallel",), 48),
        name="qkv_prep",
    )(pa, gqa_q_gain.reshape(1, -1), gqa_k_gain.reshape(1, -1), *tabs_h,
      pb, pb, pkr, mla_q_gain.reshape(1, -1), mla_kv_gain.reshape(1, -1), w_uq_pad, w_ukv, *tabs_r)


TILE_CHUNKS = TQ // HCHUNK
DIAG = 8
LEVELS = (8, 16, 32)


def _split3(x):
    hi = x.astype(BF16)
    r1 = x - hi.astype(F32)
    mid = r1.astype(BF16)
    lo = (r1 - mid.astype(F32)).astype(BF16)
    return hi, mid, lo


def _hgrn_kernel(q_ref, v_ref, z_ref, lg_ref, *rest, reverse, layer, cast):
    if cast is None:
        o_ref, scratch = rest[0], rest[1:]
    else:
        o_ref, scratch = rest[1], rest[3:]
        step = pl.program_id(0) * pl.num_programs(1) + pl.program_id(1)
        pl.when(step < cast[1])(lambda: cast[0](rest[0], rest[2]))
    cum_refs = scratch[0:HG_HEADS]
    k_refs = scratch[HG_HEADS:2 * HG_HEADS]
    s_refs = scratch[2 * HG_HEADS:]

    lg = [lg_ref[l:l + 1, :] for l in range(DEPTH)]
    mx = functools.reduce(jnp.maximum, lg)
    ex = [jnp.exp(r - mx) for r in lg]
    den = functools.reduce(lambda a_, b_: a_ + b_, ex)
    lb_all = jnp.zeros((1, HG_W), F32)
    for l in range(1, layer + 1):
        lb_all = lb_all + ex[l] / den
    log_lb_all = jnp.log(lb_all)
    log_1m_all = jnp.log1p(-lb_all)

    ti = lax.broadcasted_iota(jnp.int32, (TQ, TQ), 0)
    si = lax.broadcasted_iota(jnp.int32, (TQ, TQ), 1)
    causal = (si >= ti) if reverse else (si <= ti)
    tri = (causal & ((ti // HCHUNK) == (si // HCHUNK))).astype(BF16)
    tc = lax.broadcasted_iota(jnp.int32, (HCHUNK, HCHUNK), 0)
    sc = lax.broadcasted_iota(jnp.int32, (HCHUNK, HCHUNK), 1)
    diag_mask = (((sc >= tc) if reverse else (sc <= tc)) & ((tc // DIAG) == (sc // DIAG))).astype(F32)
    ri = lax.broadcasted_iota(jnp.int32, (DIAG * HG_DK, LANES), 0)
    ci = lax.broadcasted_iota(jnp.int32, (DIAG * HG_DK, LANES), 1)
    sel = ((ri // HG_DK) == (ci % DIAG)).astype(BF16)
    edge = 0 if reverse else HCHUNK - 1

    @pl.when(pl.program_id(1) == 0)
    def _():
        for s_ref in s_refs:
            s_ref[...] = jnp.zeros_like(s_ref)

    heads = range(HG_HEADS)

    def lanes(hh):
        return slice(hh * HG_DK, (hh + 1) * HG_DK)

    chunks = range(TILE_CHUNKS)

    def crow(c):
        return slice(c * HCHUNK, (c + 1) * HCHUNK)

    q = [q_ref[:, lanes(hh)].astype(F32) * (HG_DK ** -0.5) for hh in heads]
    v = [v_ref[:, lanes(hh)] for hh in heads]

    z = z_ref[...]
    u = jnp.exp(-jnp.abs(z))
    w = 1.0 + u
    b_ = log_1m_all + (jnp.minimum(z, 0.0) - jnp.log(w))
    log_f = jnp.maximum(log_lb_all, b_) + jnp.log(1.0 + jnp.exp(-jnp.abs(log_lb_all - b_)))
    k_all = (1.0 - lb_all) * (jnp.where(z >= 0.0, u, 1.0) / w)
    k = [k_all[:, lanes(hh)] for hh in heads]

    cum3 = jnp.dot(tri, jnp.concatenate(_split3(log_f), axis=1), preferred_element_type=F32)
    cum_all = (cum3[:, :HG_W] + cum3[:, HG_W:2 * HG_W] + cum3[:, 2 * HG_W:]) * LOG2E
    cum = [cum_all[:, lanes(hh)] for hh in heads]
    for hh in heads:
        cum_refs[hh][...] = cum[hh]
        k_refs[hh][...] = k[hh]
    tot = [[cum_refs[hh][c * HCHUNK + edge:c * HCHUNK + edge + 1, :] for c in chunks] for hh in heads]

    def bcast_rows(ref, s):
        return jnp.concatenate(
            [jnp.broadcast_to(ref[blk * DIAG + s:blk * DIAG + s + 1, :], (DIAG, HG_DK))
             for blk in range(TQ // DIAG)], axis=0)

    pair_lhs = []
    for hh in heads:
        cols = [(q[hh] * jnp.exp2(jnp.minimum(cum[hh] - bcast_rows(cum_refs[hh], s), 0.0))
                 * bcast_rows(k_refs[hh], s)).astype(BF16) for s in range(DIAG)]
        pair_lhs.append(jnp.concatenate(cols, axis=1))

    def span_operands(hh, c):
        base = c * HCHUNK
        lhs, rhs = [], []
        for hs in LEVELS:
            for blk in range(HCHUNK // (2 * hs)):
                lo, mid, hi = blk * 2 * hs, blk * 2 * hs + hs, (blk + 1) * 2 * hs
                k_rows, q_rows = ((mid, hi), (lo, mid)) if reverse else ((lo, mid), (mid, hi))
                ref_row = base + (mid if reverse else mid - 1)
                rr = cum_refs[hh][ref_row:ref_row + 1, :]
                qr = slice(base + q_rows[0], base + q_rows[1])
                kr = slice(base + k_rows[0], base + k_rows[1])
                qs = q[hh][qr, :] * jnp.exp2(cum[hh][qr, :] - rr)
                ks = k[hh][kr, :] * jnp.exp2(rr - cum[hh][kr, :])
                zq = [jnp.zeros((q_rows[0], HG_DK), F32), qs, jnp.zeros((HCHUNK - q_rows[1], HG_DK), F32)]
                zk = [jnp.zeros((k_rows[0], HG_DK), F32), ks, jnp.zeros((HCHUNK - k_rows[1], HG_DK), F32)]
                lhs.append(jnp.concatenate([p_ for p_ in zq if p_.shape[0]], axis=0))
                rhs.append(jnp.concatenate([p_ for p_ in zk if p_.shape[0]], axis=0))
        return jnp.concatenate(lhs, axis=1).astype(BF16), jnp.concatenate(rhs, axis=1).astype(BF16)

    spans = [[span_operands(hh, c) for c in chunks] for hh in heads]
    qe = [(q[hh] * jnp.exp2(cum[hh])).astype(BF16) for hh in heads]
    ke = [[(k[hh][crow(c), :] * jnp.exp2(tot[hh][c] - cum[hh][crow(c), :])).astype(BF16) for c in chunks]
          for hh in heads]

    pair = [jnp.dot(pair_lhs[hh], sel, preferred_element_type=F32) for hh in heads]
    a_off = [[_nt_dot(*spans[hh][c]) for c in chunks] for hh in heads]
    upd = [[lax.dot_general(v[hh][crow(c), :], ke[hh][c], (((0,), (0,)), ((), ())),
                            preferred_element_type=F32) for c in chunks] for hh in heads]
    a = [[(a_off[hh][c] + pair[hh][crow(c), :HCHUNK] * diag_mask).astype(BF16) for c in chunks] for hh in heads]
    o_intra = [[jnp.dot(a[hh][c], v[hh][crow(c), :], preferred_element_type=F32) for c in chunks] for hh in heads]

    state = [s_refs[hh][...] for hh in heads]
    for c in (reversed(chunks) if reverse else chunks):
        inter = [_nt_dot(qe[hh][crow(c), :], state[hh].astype(BF16)) for hh in heads]
        for hh in heads:
            o_ref[crow(c), lanes(hh)] = o_intra[hh][c] + inter[hh]
            state[hh] = state[hh] * jnp.exp2(tot[hh][c]) + upd[hh][c]
    for hh in heads:
        s_refs[hh][...] = state[hh]


def _hgrn(pa, pz, lb_logits_dir, layer, reverse, cast=None):
    n_tiles_row = QT_SEQ + 1

    def tile_row(b, j):
        lat = b * QT_SEQ + (QT_SEQ - j if reverse else j - 1)
        return jnp.where(j == 0, LAT // TQ + b, lat)

    def spec(col):
        return pl.BlockSpec((TQ, HG_W), lambda b, j: (tile_row(b, j), col))

    in_specs = [spec(OFF_HQ // HG_W), spec(OFF_HI // HG_W), spec(1 if reverse else 0),
                pl.BlockSpec((DEPTH, HG_W), lambda b, j: (0, 0))]
    args = [pa, pa, pz, lb_logits_dir]
    out_specs = spec(0)
    out_shape = jax.ShapeDtypeStruct((M, HG_W), F32)
    kernel_cast = None
    if cast is not None:
        kind, w, w_layer = cast
        n_blocks = IN_MAIN // PROJ_TN if kind == "inproj" else D // PREP_ROWS

        def blk(b, j):
            return jnp.minimum(b * n_tiles_row + j, n_blocks - 1)

        if kind == "inproj":
            kernel_cast = (_cast_rows_kernel, n_blocks)
            in_specs.append(pl.BlockSpec((None, PROJ_TN, D), lambda b, j: (w_layer, blk(b, j), 0)))
            cast_spec = pl.BlockSpec((None, None, PROJ_TN, D), lambda b, j: (0, blk(b, j), 0, 0))
            cast_shape = jax.ShapeDtypeStruct((1, n_blocks, PROJ_TN, D), BF16)
        else:
            kernel_cast = (_proj_tiles_kernel, n_blocks)
            in_specs.append(pl.BlockSpec((None, PREP_ROWS, D), lambda b, j: (w_layer, blk(b, j), 0)))
            cast_spec = pl.BlockSpec((None, D // PROJ_TN, PREP_ROWS, PROJ_TN), lambda b, j: (0, 0, blk(b, j), 0))
            cast_shape = jax.ShapeDtypeStruct((1, D // PROJ_TN, D, PROJ_TN), BF16)
        args.append(w)
        out_specs, out_shape = [out_specs, cast_spec], [out_shape, cast_shape]
    return pl.pallas_call(
        functools.partial(_hgrn_kernel, reverse=reverse, layer=layer, cast=kernel_cast),
        grid=(BATCH, n_tiles_row),
        in_specs=in_specs,
        out_specs=out_specs,
        out_shape=out_shape,
        scratch_shapes=([pltpu.VMEM((TQ, HG_DK), F32)] * (2 * HG_HEADS)
                        + [pltpu.VMEM((HG_DK, HG_DK), F32)] * HG_HEADS),
        compiler_params=_cparams(("parallel" if cast is None else "arbitrary", "arbitrary"), 32),
        name="hgrn_bwd" if reverse else "hgrn_fwd",
    )(*args)


def _outproj_kernel(x_ref, gt_ref, oa_ref, of_ref, ob_ref, hg_ref, om_ref, ng_ref, w_ref, o_ref, lhs_ref):
    a_w = GQA_HEADS * HEAD_DIM
    lhs_ref[:, 0:a_w] = oa_ref[...]
    lhs_ref[:, a_w + HG_W:] = om_ref[...]
    gain = ng_ref[...]
    for hh in range(HG_HEADS):
        sl = slice(hh * HG_DK, (hh + 1) * HG_DK)
        g = hg_ref[:, sl].astype(F32)
        y = _head_norm(of_ref[:, sl] + ob_ref[:, sl], gain) * (g * _sigmoid(g))
        lhs_ref[:, a_w + hh * HG_DK:a_w + (hh + 1) * HG_DK] = y.astype(BF16)

    lhs = lhs_ref[...]
    gate = _mod_vec(gt_ref, pl.program_id(0), TMP)
    for n in range(D // PROJ_TN):
        cols = slice(n * PROJ_TN, (n + 1) * PROJ_TN)
        y = jnp.dot(lhs, w_ref[n], preferred_element_type=F32)
        o_ref[:, cols] = x_ref[:, cols] + gate[:, cols] * y


def _outproj(x, mods, o_a, o_f, o_b, pb, o_m, norm_gain, w_tiles, layer, rows):
    def row_spec(w, col=0):
        return pl.BlockSpec((TMP, w), lambda i: (i, col))

    return pl.pallas_call(
        _outproj_kernel,
        grid=(rows // TMP,),
        in_specs=[
            row_spec(D),
            pl.BlockSpec((None, SUBLANES, D), lambda i: (layer, 0, 5)),
            row_spec(GQA_HEADS * HEAD_DIM), row_spec(HG_W), row_spec(HG_W),
            row_spec(HG_W, 0),
            row_spec(MLA_HEADS * MLA_V),
            pl.BlockSpec((1, HG_DK), lambda i: (0, 0)),
            pl.BlockSpec((None, D // PROJ_TN, D, PROJ_TN), lambda i: (0, 0, 0, 0),
                         pipeline_mode=pl.Buffered(1)),
        ],
        out_specs=row_spec(D),
        out_shape=jax.ShapeDtypeStruct((rows, D), F32),
        scratch_shapes=[pltpu.VMEM((TMP, D), BF16)],
        compiler_params=_cparams(("parallel",), 56),
        name="outproj",
    )(x, mods, o_a, o_f, o_b, pb, o_m, norm_gain.reshape(1, HG_DK), w_tiles)


def _pad_w_uq(w_uq):
    w = w_uq.reshape(DEPTH, MLA_Q_RANK, MLA_HEADS, MLA_QK)
    w = jnp.pad(w, ((0, 0), (0, 0), (0, 0), (0, MLA_PAD - MLA_QK)))
    return w.reshape(DEPTH, MLA_Q_RANK, MLA_HEADS * MLA_PAD).astype(BF16)


def kernel(x, c, ctx, c_ctx, w_mod, b_mod, w_ffn1_in, w_ffn1_out, w_in, w_uq, w_ukv, w_out,
           gqa_q_gain, gqa_k_gain, mla_q_gain, mla_kv_gain, hgrn_lb_logits, hgrn_norm_gain,
           w_ffn2_in, w_ffn2_out, final_gain):
    cvec8 = jnp.concatenate([c, c_ctx[None, :], jnp.zeros((SUBLANES - BATCH - 1, D), F32)], axis=0)
    mods = _modulation(cvec8, w_mod, b_mod)
    tabs_h = _rope_tables(HEAD_DIM)
    tabs_r = _rope_tables(MLA_ROPE)
    ffn_in = _ffn_win_tiles(w_ffn1_in, 0)
    w_in_rows = jnp.transpose(w_in, (0, 2, 1))
    w_in_t = _inproj_weight_tiles(w_in_rows, 0)
    w_out_t = _proj_tiles(w_out, 0)
    w_kr = jnp.pad(w_in_rows[:, OFF_KR:, :], ((0, 0), (0, LANES - MLA_ROPE), (0, 0))).astype(BF16)
    w_uq_pad = _pad_w_uq(w_uq)
    w_ukv_b = w_ukv.astype(BF16)

    xa = None
    for l in range(DEPTH):
        last = l == DEPTH - 1
        if l == 0:
            xa = _ffn(x.reshape(LAT, D), mods, ffn_in, w_ffn1_out, l, 0, ctx=ctx.reshape(CTXR, D))
        else:
            xa = _ffn(xa, mods, ffn_in, w_ffn1_out, l, 0)
        pa, pz, pb, pkr = _inproj(xa, mods, w_in_t, w_kr, l)
        q_a, k_a, v_a, q_m, k_m, v_m = _qkv_prep(pa, pb, pkr, gqa_q_gain[l], gqa_k_gain[l], mla_q_gain[l],
                                                 mla_kv_gain[l], w_uq_pad[l], w_ukv_b[l], tabs_h, tabs_r)
        o_a, ffn2_in = _gqa_attn(q_a, k_a, v_a, not last, cast=(w_ffn2_in, l))
        if last:
            o_m = _mla_attn(q_m, k_m, v_m, False)
        else:
            o_m, ffn_in = _mla_attn(q_m, k_m, v_m, True, cast=(w_ffn1_in, l + 1))
        if last:
            o_f = _hgrn(pa, pz, hgrn_lb_logits[0], l, False)
            o_b = _hgrn(pa, pz, hgrn_lb_logits[1], l, True)
        else:
            o_f, w_in_next = _hgrn(pa, pz, hgrn_lb_logits[0], l, False, cast=("inproj", w_in_rows, l + 1))
            o_b, w_out_next = _hgrn(pa, pz, hgrn_lb_logits[1], l, True, cast=("outproj", w_out, l + 1))
        xa = _outproj(xa, mods, o_a, o_f, o_b, pb, o_m, hgrn_norm_gain[l], w_out_t, l, LAT if last else M)
        if not last:
            w_in_t, w_out_t = w_in_next, w_out_next
        xa = _ffn(xa, mods, ffn2_in, w_ffn2_out, l, 6, final_gain=final_gain if last else None)
    return xa
```
